```python
import math
import jax
import jax.numpy as jnp
from jax import lax
import numpy as np

D_MODEL = 2048
BATCH = 16
SEQ = 2048
DEPTH = 4

N_MIXERS = 3
HEAD_DIM = 128
ROPE_THETA = 500000.0
ROPE_FRACTION = 4
QUERY_BLOCK = 128
NEG_FILL = -1e30

A_HEADS = D_MODEL // HEAD_DIM
A_QK_DIM = HEAD_DIM // 2

B_HEADS = D_MODEL // HEAD_DIM
B_KV_GROUPS = 4
B_CMP_LEN = 32
B_CMP_STRIDE = 16
B_SEL_LEN = 64
B_SEL_TOPN = 16
B_WINDOW = 512
B_SEL_CHUNK = 8
B_N_KV = 6

C_PATTERNS = ((128, 1), (512, 4), (2048, 16))
C_HEADS_PER_GROUP = D_MODEL // (2 * HEAD_DIM)

MOE_GROUPS = 4
MOE_EXPERTS_PER_GROUP = 8
MOE_EXPERTS = MOE_GROUPS * MOE_EXPERTS_PER_GROUP
MOE_TOPK = 2
MOE_D_FF = D_MODEL // 4
MOE_ROW_BLOCK = 256

DN_ALPHA = (2 * DEPTH) ** 0.25
DN_BETA = (8 * DEPTH) ** -0.25
NORM_EPS = 1e-5

kernel_name = 'hybrid_diff_nsa_dilated_hmoe_deepnorm'


def layer_norm(x, g, b):
    xf = x.astype(jnp.float32)
    mu = jnp.mean(xf, axis=-1, keepdims=True)
    var = jnp.mean(jnp.square(xf - mu), axis=-1, keepdims=True)
    return ((xf - mu) * lax.rsqrt(var + NORM_EPS) * g + b).astype(x.dtype)


def rms_norm(x, g):
    xf = x.astype(jnp.float32)
    return (xf * lax.rsqrt(jnp.mean(xf * xf, axis=-1, keepdims=True) + NORM_EPS) * g).astype(x.dtype)


def rope_angles(pos, dim):
    rot = dim // ROPE_FRACTION
    inv_freq = ROPE_THETA ** (-jnp.arange(0, rot, 2, dtype=jnp.float32) / rot)
    ang = pos.astype(jnp.float32)[:, None] * inv_freq[None, :]
    return jnp.cos(ang), jnp.sin(ang)


def partial_rope(x, cos, sin):
    half = cos.shape[-1]
    x1 = x[..., :half].astype(jnp.float32)
    x2 = x[..., half:2 * half].astype(jnp.float32)
    r1 = (x1 * cos - x2 * sin).astype(x.dtype)
    r2 = (x2 * cos + x1 * sin).astype(x.dtype)
    return jnp.concatenate([r1, r2, x[..., 2 * half:]], axis=-1)


def banded_attention(q, k, v, span, block):
    B, G, R, L, hd = q.shape
    nb = L // block
    pad = -(-span // block) * block
    kp = jnp.pad(k, ((0, 0), (0, 0), (pad, 0), (0, 0)))
    vp = jnp.pad(v, ((0, 0), (0, 0), (pad, 0), (0, 0)))
    qb = jnp.moveaxis(q.reshape(B, G, R, nb, block, hd), 3, 0)
    scale = hd ** -0.5
    offs_q = jnp.arange(block)
    offs_k = jnp.arange(pad + block) - pad

    def one(args):
        qi, i = args
        kb = lax.dynamic_slice_in_dim(kp, i * block, pad + block, axis=2)
        vb = lax.dynamic_slice_in_dim(vp, i * block, pad + block, axis=2)
        s = jnp.einsum('bgrqd,bgkd->bgrqk', qi, kb).astype(jnp.float32) * scale
        qpos = i * block + offs_q
        kpos = i * block + offs_k
        diff = qpos[:, None] - kpos[None, :]
        mask = (diff >= 0) & (diff <= span) & (kpos >= 0)[None, :]
        s = jnp.where(mask, s, -jnp.inf)
        m = jnp.max(s, axis=-1, keepdims=True)
        e = jnp.exp(s - m)
        den = jnp.sum(e, axis=-1, keepdims=True)
        o = jnp.einsum('bgrqk,bgkd->bgrqd', (e / den).astype(v.dtype), vb)
        return o, (m + jnp.log(den))[..., 0]

    o, lse = lax.map(one, (qb, jnp.arange(nb)))
    o = jnp.moveaxis(o, 0, 3).reshape(B, G, R, L, hd)
    lse = jnp.moveaxis(lse, 0, 3).reshape(B, G, R, L)
    return o, lse


def causal_diff_attention(q1, q2, k1, k2, v, lam):
    B, H, S, dq = q1.shape
    blk = math.gcd(QUERY_BLOCK, S)
    nb = S // blk
    scale = dq ** -0.5
    kpos = jnp.arange(S)

    def to_blocks(t):
        return jnp.moveaxis(t.reshape(B, H, nb, blk, dq), 2, 0)

    def one(args):
        q1b, q2b, i = args
        mask = kpos[None, :] <= (i * blk + jnp.arange(blk))[:, None]

        def probs(qb, kk):
            s = jnp.einsum('bhqd,bhkd->bhqk', qb, kk).astype(jnp.float32) * scale
            return jax.nn.softmax(jnp.where(mask, s, -jnp.inf), axis=-1)

        a = probs(q1b, k1) - lam * probs(q2b, k2)
        return jnp.einsum('bhqk,bhkd->bhqd', a.astype(v.dtype), v)

    o = lax.map(one, (to_blocks(q1), to_blocks(q2), jnp.arange(nb)))
    return jnp.moveaxis(o, 0, 2).reshape(B, H, S, v.shape[-1])


def diff_attention_mixer(x, w_in, lam, subln_g, w_out, layer_idx):
    B, S, _ = x.shape
    H, dq, dv = A_HEADS, A_QK_DIM, HEAD_DIM
    q, k, v = jnp.split(x @ w_in, 3, axis=-1)
    q = q.reshape(B, S, H, 2, dq).transpose(3, 0, 2, 1, 4)
    k = k.reshape(B, S, H, 2, dq).transpose(3, 0, 2, 1, 4)
    v = v.reshape(B, S, H, dv).transpose(0, 2, 1, 3)
    cos, sin = rope_angles(jnp.arange(S), dq)
    q = partial_rope(q, cos, sin)
    k = partial_rope(k, cos, sin)
    lam_init = 0.8 - 0.6 * math.exp(-0.3 * layer_idx)
    lf = lam.astype(jnp.float32)
    lam_full = jnp.exp(jnp.sum(lf[0] * lf[1])) - jnp.exp(jnp.sum(lf[2] * lf[3])) + lam_init
    o = causal_diff_attention(q[0], q[1], k[0], k[1], v, lam_full)
    o = rms_norm(o, subln_g) * (1.0 - lam_init)
    return o.transpose(0, 2, 1, 3).reshape(B, S, H * dv) @ w_out


def nsa_mixer(x, w_in, cmp_pos, cmp_w1, cmp_w2, w_out):
    B, S, _ = x.shape
    H, G, hd = B_HEADS, B_KV_GROUPS, HEAD_DIM
    R = H // G
    scale = hd ** -0.5
    pos = jnp.arange(S)
    proj = x @ w_in
    q_w, kv_w = H * hd, B_N_KV * G * hd
    q = proj[..., :q_w].reshape(B, S, G, R, hd).transpose(0, 2, 3, 1, 4)
    kvs = proj[..., q_w:q_w + kv_w].reshape(B, S, B_N_KV, G, hd).transpose(2, 0, 3, 1, 4)
    gates = jax.nn.sigmoid(proj[..., q_w + kv_w:].astype(jnp.float32))
    gates = gates.reshape(B, S, G, R, 3).transpose(4, 0, 2, 3, 1)[..., None]
    k_cmp, v_cmp, k_sel, v_sel, k_win, v_win = kvs
    cos, sin = rope_angles(pos, hd)
    q = partial_rope(q, cos, sin)
    k_sel = partial_rope(k_sel, cos, sin)
    k_win = partial_rope(k_win, cos, sin)

    n_chunk = S // B_CMP_STRIDE
    per = B_CMP_LEN // B_CMP_STRIDE
    n_cmp = n_chunk - per + 1

    def compress(t, j):
        c = t.reshape(B, G, n_chunk, B_CMP_STRIDE, hd)
        blocks = jnp.concatenate([c[:, :, m:m + n_cmp] for m in range(per)], axis=3)
        blocks = (blocks + cmp_pos[j]).reshape(B, G, n_cmp, B_CMP_LEN * hd)
        return jax.nn.gelu(blocks @ cmp_w1[j]) @ cmp_w2[j]

    cmp_start = jnp.arange(n_cmp) * B_CMP_STRIDE
    cmp_end = cmp_start + B_CMP_LEN - 1
    cos_c, sin_c = rope_angles(cmp_end, hd)
    kc = partial_rope(compress(k_cmp, 0), cos_c, sin_c)
    vc = compress(v_cmp, 1)
    cmask = cmp_end[None, :] <= pos[:, None]
    s = jnp.einsum('bgrsd,bgcd->bgrsc', q, kc).astype(jnp.float32) * scale
    p_cmp = jnp.where(cmask, jax.nn.softmax(jnp.where(cmask, s, NEG_FILL), axis=-1), 0.0)
    o_cmp = jnp.einsum('bgrsc,bgcd->bgrsd', p_cmp.astype(vc.dtype), vc)

    n_sel = S // B_SEL_LEN
    sel_start = jnp.arange(n_sel) * B_SEL_LEN
    cover = ((cmp_start[:, None] < sel_start[None, :] + B_SEL_LEN)
             & (cmp_end[:, None] >= sel_start[None, :])).astype(jnp.float32)
    importance = jnp.einsum('bgrsc,cn->bgsn', p_cmp, cover)
    cur = pos // B_SEL_LEN
    blk = jnp.arange(n_sel)
    forced = (blk[None, :] == 0) | (blk[None, :] == cur[:, None]) | (blk[None, :] == cur[:, None] - 1)
    score = jnp.where(forced, jnp.inf, importance)
    score = jnp.where(sel_start[None, :] <= pos[:, None], score, -jnp.inf)
    n_top = min(B_SEL_TOPN, n_sel)
    _, sel_idx = lax.top_k(score, n_top)

    chunk = math.gcd(B_SEL_CHUNK, S)
    n_chunks = S // chunk
    kb = k_sel.reshape(B, G, n_sel, B_SEL_LEN, hd)
    vb = v_sel.reshape(B, G, n_sel, B_SEL_LEN, hd)
    qc = jnp.moveaxis(q.reshape(B, G, R, n_chunks, chunk, hd), 3, 0)
    ic = jnp.moveaxis(sel_idx.reshape(B, G, n_chunks, chunk, n_top), 2, 0)
    gather = jax.vmap(jax.vmap(lambda blocks, ix: blocks[ix]))
    offs = jnp.arange(B_SEL_LEN)

    def sel_one(args):
        qi, ii, c = args
        kg = gather(kb, ii)
        vg = gather(vb, ii)
        sc = jnp.einsum('bgrcd,bgcnld->bgrcnl', qi, kg).astype(jnp.float32) * scale
        qpos = c * chunk + jnp.arange(chunk)
        kpos = ii[..., None] * B_SEL_LEN + offs
        mask = kpos <= qpos[:, None, None]
        sc = jnp.where(mask[:, :, None], sc, -jnp.inf).reshape(B, G, R, chunk, n_top * B_SEL_LEN)
        pr = jax.nn.softmax(sc, axis=-1).reshape(B, G, R, chunk, n_top, B_SEL_LEN)
        return jnp.einsum('bgrcnl,bgcnld->bgrcd', pr.astype(vg.dtype), vg)

    o_sel = lax.map(sel_one, (qc, ic, jnp.arange(n_chunks)))
    o_sel = jnp.moveaxis(o_sel, 0, 3).reshape(B, G, R, S, hd)

    o_win, _ = banded_attention(q, k_win, v_win, B_WINDOW - 1, math.gcd(QUERY_BLOCK, S))

    o = (gates[0] * o_cmp + gates[1] * o_sel + gates[2] * o_win).astype(x.dtype)
    return o.transpose(0, 3, 1, 2, 4).reshape(B, S, H * hd) @ w_out


def dilated_mixer(x, w_in, w_out):
    B, S, _ = x.shape
    Hg, hd, P = C_HEADS_PER_GROUP, HEAD_DIM, len(C_PATTERNS)
    proj = (x @ w_in).reshape(B, S, 3, P, Hg, hd).transpose(2, 3, 0, 4, 1, 5)
    cos, sin = rope_angles(jnp.arange(S), hd)
    outs, lses = [], []
    for g, (window, dil) in enumerate(C_PATTERNS):
        L = S // dil

        def to_sub(t):
            return t.reshape(B, Hg, L, dil, hd).transpose(0, 1, 3, 2, 4).reshape(B, Hg * dil, L, hd)

        q = to_sub(partial_rope(proj[0, g], cos, sin))
        k = to_sub(partial_rope(proj[1, g], cos, sin))
        v = to_sub(proj[2, g])
        o, lse = banded_attention(q[:, :, None], k, v, window // dil, math.gcd(QUERY_BLOCK, L))
        outs.append(o[:, :, 0].reshape(B, Hg, dil, L, hd).transpose(0, 1, 3, 2, 4).reshape(B, Hg, S, hd))
        lses.append(lse[:, :, 0].reshape(B, Hg, dil, L).transpose(0, 1, 3, 2).reshape(B, Hg, S))
    w = jax.nn.softmax(jnp.stack(lses), axis=0)
    o = jnp.sum(w[..., None] * jnp.stack(outs).astype(jnp.float32), axis=0).astype(x.dtype)
    return o.transpose(0, 2, 1, 3).reshape(B, S, Hg * hd) @ w_out


def hierarchical_moe(x, router_w, router_b, w_in, w_out):
    B, S, D = x.shape
    T = B * S
    K, E, R = MOE_TOPK, MOE_EXPERTS, MOE_ROW_BLOCK
    xt = x.reshape(T, D)
    logits = (xt @ router_w).astype(jnp.float32) + router_b
    g_prob = jax.nn.softmax(logits[:, :MOE_GROUPS], axis=-1)
    g_w, g_idx = lax.top_k(g_prob, 1)
    e_logits = logits[:, MOE_GROUPS:].reshape(T, MOE_GROUPS, MOE_EXPERTS_PER_GROUP)
    e_in = jnp.take_along_axis(e_logits, g_idx[:, :, None], axis=1)[:, 0]
    e_val, e_idx = lax.top_k(e_in, K)
    gate = jax.nn.softmax(e_val, axis=-1) * g_w
    flat_e = (g_idx * MOE_EXPERTS_PER_GROUP + e_idx).reshape(-1)
    flat_w = gate.reshape(-1)
    flat_tok = jnp.repeat(jnp.arange(T, dtype=jnp.int32), K)
    A = T * K
    n_blk = -(-A // R) + E
    order = jnp.argsort(flat_e)
    se = flat_e[order]
    sizes = jnp.bincount(flat_e, length=E)
    padded = (sizes + R - 1) // R * R
    pad_end = jnp.cumsum(padded)
    pad_start = pad_end - padded
    start = jnp.cumsum(sizes) - sizes
    dest = pad_start[se] + jnp.arange(A, dtype=jnp.int32) - start[se]
    slot_tok = jnp.full((n_blk * R,), T, jnp.int32).at[dest].set(flat_tok[order])
    slot_w = jnp.zeros((n_blk * R,), jnp.float32).at[dest].set(flat_w[order])
    blk_e = jnp.minimum(jnp.searchsorted(pad_end, jnp.arange(n_blk) * R, side='right'), E - 1)
    xpad = jnp.concatenate([xt, jnp.zeros((1, D), xt.dtype)], axis=0)
    xb = xpad[slot_tok].reshape(n_blk, R, D)

    def expert_block(args):
        xr, e = args
        gu = xr @ w_in[e]
        g, u = jnp.split(gu, 2, axis=-1)
        return (jax.nn.silu(g) * u) @ w_out[e]

    yb = lax.map(expert_block, (xb, blk_e)).reshape(n_blk * R, D)
    y = jnp.zeros((T + 1, D), jnp.float32).at[slot_tok].add(yb.astype(jnp.float32) * slot_w[:, None])
    return y[:T].astype(x.dtype).reshape(B, S, D)


def setup_inputs(seed: int = 0) -> dict:
    key = jax.random.key(seed)
    keys = iter(jax.random.split(key, 128))
    D, hd = D_MODEL, HEAD_DIM

    def normal(shape, fan_in, scale=1.0):
        return jax.random.normal(next(keys), shape, jnp.float32) * (scale * fan_in ** -0.5)

    def noise(shape, std):
        return jax.random.normal(next(keys), shape, jnp.float32) * std

    params = {'x': jax.random.normal(next(keys), (BATCH, SEQ, D), jnp.float32)}
    for i in range(DEPTH):
        kind = i % N_MIXERS
        p = 'l%d_' % i
        if kind == 0:
            w = A_HEADS * hd
            params[p + 'a_w_in'] = jnp.concatenate(
                [normal((D, w), D), normal((D, w), D), normal((D, w), D, DN_BETA)], axis=1)
            params[p + 'a_lam'] = noise((4, A_QK_DIM), 0.1)
            params[p + 'a_subln'] = 1.0 + noise((hd,), 0.02)
            params[p + 'a_w_out'] = normal((w, D), w, DN_BETA)
        elif kind == 1:
            kv = B_KV_GROUPS * hd
            cols = [normal((D, B_HEADS * hd), D)]
            cols += [normal((D, kv), D, DN_BETA if j % 2 else 1.0) for j in range(B_N_KV)]
            cols.append(normal((D, B_HEADS * 3), D))
            params[p + 'b_w_in'] = jnp.concatenate(cols, axis=1)
            params[p + 'b_cmp_pos'] = noise((2, B_CMP_LEN, hd), 0.02)
            params[p + 'b_cmp_w1'] = normal((2, B_CMP_LEN * hd, hd), B_CMP_LEN * hd)
            params[p + 'b_cmp_w2'] = normal((2, hd, hd), hd)
            params[p + 'b_w_out'] = normal((B_HEADS * hd, D), B_HEADS * hd, DN_BETA)
        else:
            w = len(C_PATTERNS) * C_HEADS_PER_GROUP * hd
            params[p + 'c_w_in'] = jnp.concatenate(
                [normal((D, w), D), normal((D, w), D), normal((D, w), D, DN_BETA)], axis=1)
            params[p + 'c_w_out'] = normal((C_HEADS_PER_GROUP * hd, D), C_HEADS_PER_GROUP * hd, DN_BETA)
        params[p + 'ln'] = jnp.stack([1.0 + noise((D,), 0.02), noise((D,), 0.02),
                                      1.0 + noise((D,), 0.02), noise((D,), 0.02)])
        n_route = MOE_GROUPS + MOE_EXPERTS
        params[p + 'router_w'] = normal((D, n_route), D)
        params[p + 'router_b'] = noise((n_route,), 0.01)
        params[p + 'moe_w_in'] = jnp.concatenate(
            [normal((MOE_EXPERTS, D, MOE_D_FF), D), normal((MOE_EXPERTS, D, MOE_D_FF), D, DN_BETA)], axis=-1)
        params[p + 'moe_w_out'] = normal((MOE_EXPERTS, MOE_D_FF, D), MOE_D_FF, DN_BETA)
    return params


def reference(x,
              l0_a_w_in, l0_a_lam, l0_a_subln, l0_a_w_out,
              l0_ln, l0_router_w, l0_router_b, l0_moe_w_in, l0_moe_w_out,
              l1_b_w_in, l1_b_cmp_pos, l1_b_cmp_w1, l1_b_cmp_w2, l1_b_w_out,
              l1_ln, l1_router_w, l1_router_b, l1_moe_w_in, l1_moe_w_out,
              l2_c_w_in, l2_c_w_out,
              l2_ln, l2_router_w, l2_router_b, l2_moe_w_in, l2_moe_w_out,
              l3_a_w_in, l3_a_lam, l3_a_subln, l3_a_w_out,
              l3_ln, l3_router_w, l3_router_b, l3_moe_w_in, l3_moe_w_out):
    mixer_args = [(l0_a_w_in, l0_a_lam, l0_a_subln, l0_a_w_out),
                  (l1_b_w_in, l1_b_cmp_pos, l1_b_cmp_w1, l1_b_cmp_w2, l1_b_w_out),
                  (l2_c_w_in, l2_c_w_out),
                  (l3_a_w_in, l3_a_lam, l3_a_subln, l3_a_w_out)]
    ffn_args = [(l0_ln, l0_router_w, l0_router_b, l0_moe_w_in, l0_moe_w_out),
                (l1_ln, l1_router_w, l1_router_b, l1_moe_w_in, l1_moe_w_out),
                (l2_ln, l2_router_w, l2_router_b, l2_moe_w_in, l2_moe_w_out),
                (l3_ln, l3_router_w, l3_router_b, l3_moe_w_in, l3_moe_w_out)]
    h = x
    for i in range(DEPTH):
        kind = i % N_MIXERS
        ln, router_w, router_b, moe_w_in, moe_w_out = ffn_args[i]
        if kind == 0:
            m = diff_attention_mixer(h, *mixer_args[i], layer_idx=i)
        elif kind == 1:
            m = nsa_mixer(h, *mixer_args[i])
        else:
            m = dilated_mixer(h, *mixer_args[i])
        h = layer_norm(DN_ALPHA * h + m, ln[0], ln[1])
        f = hierarchical_moe(h, router_w, router_b, moe_w_in, moe_w_out)
        h = layer_norm(DN_ALPHA * h + f, ln[2], ln[3])
    return h
```

```python
import functools
import math

import jax
import jax.numpy as jnp
from jax import lax
from jax.experimental import pallas as pl
from jax.experimental.pallas import tpu as pltpu

D_MODEL = 2048
DEPTH = 4
HEAD_DIM = 128
ROPE_THETA = 500000.0
ROPE_FRACTION = 4
NEG_FILL = -1e30

A_HEADS = D_MODEL // HEAD_DIM
A_QK_DIM = HEAD_DIM // 2

B_HEADS = D_MODEL // HEAD_DIM
B_KV_GROUPS = 4
B_REP = B_HEADS // B_KV_GROUPS
B_CMP_LEN = 32
B_CMP_STRIDE = 16
B_SEL_LEN = 64
B_SEL_TOPN = 16
B_WINDOW = 512
B_N_KV = 6

C_PATTERNS = ((128, 1), (512, 4), (2048, 16))
C_HEADS_PER_GROUP = D_MODEL // (2 * HEAD_DIM)

MOE_GROUPS = 4
MOE_EXPERTS_PER_GROUP = 8
MOE_EXPERTS = MOE_GROUPS * MOE_EXPERTS_PER_GROUP
MOE_D_FF = D_MODEL // 4
MOE_ROWS = 256

DN_ALPHA = (2 * DEPTH) ** 0.25
NORM_EPS = 1e-5

LANES = 128
VMEM_LIMIT = 56 * 1024 * 1024

BF16 = jnp.bfloat16
F32 = jnp.float32


def _cparams(sem):
    return pltpu.CompilerParams(dimension_semantics=sem, vmem_limit_bytes=VMEM_LIMIT)


def _dot(a, b):
    return jnp.dot(a, b, preferred_element_type=F32)


def _dot_nt(a, b):
    return lax.dot_general(a, b, (((1,), (1,)), ((), ())), preferred_element_type=F32)


def _rope_tables(pos, dim):
    rot = dim // ROPE_FRACTION
    half = rot // 2
    inv_freq = ROPE_THETA ** (-jnp.arange(0, rot, 2, dtype=F32) / rot)
    ang = pos.astype(F32)[:, None] * inv_freq[None, :]
    cos, sin = jnp.cos(ang), jnp.sin(ang)
    n = pos.shape[0]
    ones = jnp.ones((n, dim - rot), F32)
    zeros = jnp.zeros((n, dim - rot), F32)
    zh = jnp.zeros((n, half), F32)
    c = jnp.concatenate([cos, cos, ones], axis=1)
    sp = jnp.concatenate([zh, sin, zeros], axis=1)
    sm = jnp.concatenate([-sin, zh, zeros], axis=1)
    reps = LANES // dim
    return tuple(jnp.tile(t, (1, reps)) for t in (c, sp, sm))


def _apply_rope(a, c, sp, sm, half):
    return a * c + pltpu.roll(a, half, 1) * sp + pltpu.roll(a, LANES - half, 1) * sm


def _proj_kernel(flags_ref, x_ref, w_ref, c_ref, sp_ref, sm_ref, o_ref, *, half):
    j = pl.program_id(1)
    acc = _dot(x_ref[...], w_ref[...])
    tn = acc.shape[1]

    @pl.when(flags_ref[j] == 0)
    def _():
        o_ref[...] = acc.astype(o_ref.dtype)

    @pl.when(flags_ref[j] != 0)
    def _():
        c, sp, sm = c_ref[...], sp_ref[...], sm_ref[...]
        for s in range(tn // LANES):
            a = acc[:, s * LANES:(s + 1) * LANES]
            o_ref[:, s * LANES:(s + 1) * LANES] = _apply_rope(a, c, sp, sm, half).astype(o_ref.dtype)


def _proj(xb, w, flags, tables, seq, half, tm=1024, tn=512):
    m, k = xb.shape
    n = w.shape[1]
    assert m % tm == 0 and n % tn == 0 and seq % tm == 0
    pos_blocks = seq // tm
    tab_spec = pl.BlockSpec((tm, LANES), lambda i, j, f: (i % pos_blocks, 0))
    return pl.pallas_call(
        functools.partial(_proj_kernel, half=half),
        grid_spec=pltpu.PrefetchScalarGridSpec(
            num_scalar_prefetch=1,
            grid=(m // tm, n // tn),
            in_specs=[pl.BlockSpec((tm, k), lambda i, j, f: (i, 0)),
                      pl.BlockSpec((k, tn), lambda i, j, f: (0, j)),
                      tab_spec, tab_spec, tab_spec],
            out_specs=pl.BlockSpec((tm, tn), lambda i, j, f: (i, j))),
        out_shape=jax.ShapeDtypeStruct((m, n), BF16),
        compiler_params=_cparams(("parallel", "arbitrary")),
        name="proj",
    )(flags, xb, w, *tables)


def _layer_norm(z, g, b):
    mu = jnp.mean(z, axis=-1, keepdims=True)
    zc = z - mu
    var = jnp.mean(zc * zc, axis=-1, keepdims=True)
    return zc * lax.rsqrt(var + NORM_EPS) * g + b


def _outproj_ln_kernel(o_ref, w_ref, h_ref, ln_ref, hf_ref, hb_ref):
    m = _dot(o_ref[...], w_ref[...])
    y = _layer_norm(DN_ALPHA * h_ref[...] + m, ln_ref[0:1, :], ln_ref[1:2, :])
    hf_ref[...] = y
    hb_ref[...] = y.astype(BF16)


def _outproj_ln(o, w, h, ln, tm=256):
    m, k = o.shape
    d = w.shape[1]
    return pl.pallas_call(
        _outproj_ln_kernel,
        grid=(m // tm,),
        in_specs=[pl.BlockSpec((tm, k), lambda i: (i, 0)),
                  pl.BlockSpec((k, d), lambda i: (0, 0)),
                  pl.BlockSpec((tm, d), lambda i: (i, 0)),
                  pl.BlockSpec((4, d), lambda i: (0, 0))],
        out_specs=[pl.BlockSpec((tm, d), lambda i: (i, 0)),
                   pl.BlockSpec((tm, d), lambda i: (i, 0))],
        out_shape=[jax.ShapeDtypeStruct((m, d), F32), jax.ShapeDtypeStruct((m, d), BF16)],
        compiler_params=_cparams(("parallel",)),
        name="outproj_ln",
    )(o, w, h, ln)


def _attn_a_kernel(lam_ref, q_ref, k_ref, v_ref, g_ref, o_ref, *, tq, tk, out_scale):
    qi = pl.program_id(2)
    q = q_ref[0] * jnp.asarray(A_QK_DIM ** -0.5, BF16)
    lane = lax.broadcasted_iota(jnp.int32, q.shape, 1)
    zero = jnp.zeros_like(q)
    q1 = jnp.where(lane < A_QK_DIM, q, zero)
    q2 = jnp.where(lane >= A_QK_DIM, q, zero)
    qpos = qi * tq + lax.broadcasted_iota(jnp.int32, (tq, tk), 0)
    kiota = lax.broadcasted_iota(jnp.int32, (tq, tk), 1)

    def step(qx, k, v, ok, m, l, acc):
        s = jnp.where(ok, _dot_nt(qx, k), -jnp.inf)
        m_new = jnp.maximum(m, jnp.max(s, axis=-1, keepdims=True))
        alpha = jnp.exp(m - m_new)
        e = jnp.exp(s - m_new)
        l_new = alpha * l + jnp.sum(e, axis=-1, keepdims=True)
        acc_new = alpha * acc + _dot(e.astype(BF16), v)
        return m_new, l_new, acc_new

    def body(kj, carry):
        m1, l1, a1, m2, l2, a2 = carry
        start = pl.multiple_of(kj * tk, tk)
        k = k_ref[0, pl.ds(start, tk), :]
        v = v_ref[0, pl.ds(start, tk), :]
        ok = (kj * tk + kiota) <= qpos
        m1, l1, a1 = step(q1, k, v, ok, m1, l1, a1)
        m2, l2, a2 = step(q2, k, v, ok, m2, l2, a2)
        return m1, l1, a1, m2, l2, a2

    minf = jnp.full((tq, 1), -jnp.inf, F32)
    z1 = jnp.zeros((tq, 1), F32)
    za = jnp.zeros((tq, HEAD_DIM), F32)
    n_chunks = (qi * tq + tq + tk - 1) // tk
    m1, l1, a1, m2, l2, a2 = lax.fori_loop(0, n_chunks, body, (minf, z1, za, minf, z1, za))
    o = a1 / l1 - lam_ref[0] * (a2 / l2)
    o = o * lax.rsqrt(jnp.mean(o * o, axis=-1, keepdims=True) + NORM_EPS) * g_ref[...] * out_scale
    o_ref[0] = o.astype(o_ref.dtype)


def _attn_a(proj, lam_full, subln, batch, seq, lam_init, tq=256, tk=256):
    h = A_HEADS
    return pl.pallas_call(
        functools.partial(_attn_a_kernel, tq=tq, tk=tk, out_scale=1.0 - lam_init),
        grid_spec=pltpu.PrefetchScalarGridSpec(
            num_scalar_prefetch=1,
            grid=(batch, h, seq // tq),
            in_specs=[pl.BlockSpec((1, tq, HEAD_DIM), lambda b, hh, i, s: (b, i, hh)),
                      pl.BlockSpec((1, seq, HEAD_DIM), lambda b, hh, i, s: (b, 0, h + hh)),
                      pl.BlockSpec((1, seq, HEAD_DIM), lambda b, hh, i, s: (b, 0, 2 * h + hh)),
                      pl.BlockSpec((1, HEAD_DIM), lambda b, hh, i, s: (0, 0))],
            out_specs=pl.BlockSpec((1, tq, HEAD_DIM), lambda b, hh, i, s: (b, i, hh))),
        out_shape=jax.ShapeDtypeStruct((batch, seq, h * HEAD_DIM), BF16),
        compiler_params=_cparams(("parallel", "parallel", "arbitrary")),
        name="attn_a",
    )(lam_full, proj, proj, proj, subln)


def _mixer_a(hb, h, w_in, lam, subln, w_out, ln, layer_idx, batch, seq):
    pos = jnp.arange(seq)
    tables = _rope_tables(pos, A_QK_DIM)
    n_q = A_HEADS * HEAD_DIM
    tn = 512
    flags = (jnp.arange(3 * n_q // tn) < 2 * n_q // tn).astype(jnp.int32)
    proj = _proj(hb, w_in.astype(BF16), flags, tables, seq, A_QK_DIM // ROPE_FRACTION // 2, tn=tn)
    lam_init = 0.8 - 0.6 * math.exp(-0.3 * layer_idx)
    lf = lam.astype(F32)
    lam_full = jnp.exp(jnp.sum(lf[0] * lf[1])) - jnp.exp(jnp.sum(lf[2] * lf[3])) + lam_init
    o = _attn_a(proj.reshape(batch, seq, -1), lam_full.reshape(1), subln.reshape(1, HEAD_DIM), batch, seq, lam_init)
    return _outproj_ln(o.reshape(batch * seq, -1), w_out.astype(BF16), h, ln)


def _dil_kernel(q_ref, k_ref, v_ref, o_ref, l_ref, *, length, span):
    blk = 128
    scale = HEAD_DIM ** -0.5
    for i in range(length // blk):
        q = q_ref[0, i * blk:(i + 1) * blk, :]
        k0 = max(i - 1, 0) * blk
        kl = (i + 1) * blk - k0
        k = k_ref[0, k0:k0 + kl, :]
        v = v_ref[0, k0:k0 + kl, :]
        s = _dot_nt(q, k) * scale
        diff = (i * blk + lax.broadcasted_iota(jnp.int32, s.shape, 0)) - (k0 + lax.broadcasted_iota(jnp.int32, s.shape, 1))
        s = jnp.where((diff >= 0) & (diff <= span), s, -jnp.inf)
        m = jnp.max(s, axis=-1, keepdims=True)
        e = jnp.exp(s - m)
        den = jnp.sum(e, axis=-1, keepdims=True)
        o_ref[0, i * blk:(i + 1) * blk, :] = _dot((e / den).astype(BF16), v)
        l_ref[0, i * blk:(i + 1) * blk, :] = jnp.broadcast_to(m + jnp.log(den), (blk, HEAD_DIM))


def _dil_attn(proj, g, window, dil, batch, seq):
    hg, p = C_HEADS_PER_GROUP, len(C_PATTERNS)
    length = seq // dil
    ncol = 3 * p * hg
    view = proj.reshape(batch, length, dil * ncol * HEAD_DIM)

    def spec(kind):
        return pl.BlockSpec((1, length, HEAD_DIM), lambda b, hh, r: (b, 0, r * ncol + kind * p * hg + g * hg + hh))

    out_spec = pl.BlockSpec((1, length, HEAD_DIM), lambda b, hh, r: (b, 0, r * hg + hh))
    out_sds = jax.ShapeDtypeStruct((batch, length, dil * hg * HEAD_DIM), F32)
    o, l = pl.pallas_call(
        functools.partial(_dil_kernel, length=length, span=window // dil),
        grid=(batch, hg, dil),
        in_specs=[spec(0), spec(1), spec(2)],
        out_specs=[out_spec, out_spec],
        out_shape=[out_sds, out_sds],
        compiler_params=_cparams(("parallel", "parallel", "arbitrary")),
        name="dil_attn_%d" % g,
    )(view, view, view)
    return o.reshape(batch * seq, hg * HEAD_DIM), l.reshape(batch * seq, hg * HEAD_DIM)


def _dil_combine_kernel(o0, o1, o2, l0, l1, l2, out_ref):
    a0, a1, a2 = l0[...], l1[...], l2[...]
    m = jnp.maximum(jnp.maximum(a0, a1), a2)
    e0, e1, e2 = jnp.exp(a0 - m), jnp.exp(a1 - m), jnp.exp(a2 - m)
    den = e0 + e1 + e2
    out_ref[...] = ((e0 / den) * o0[...] + (e1 / den) * o1[...] + (e2 / den) * o2[...]).astype(out_ref.dtype)


def _dil_combine(outs, lses, tm=512):
    m, n = outs[0].shape
    spec = pl.BlockSpec((tm, n), lambda i: (i, 0))
    return pl.pallas_call(
        _dil_combine_kernel,
        grid=(m // tm,),
        in_specs=[spec] * 6,
        out_specs=spec,
        out_shape=jax.ShapeDtypeStruct((m, n), BF16),
        compiler_params=_cparams(("parallel",)),
        name="dil_combine",
    )(*outs, *lses)


def _mixer_c(hb, h, w_in, w_out, ln, batch, seq):
    tables = _rope_tables(jnp.arange(seq), HEAD_DIM)
    tn = 512
    n = w_in.shape[1]
    flags = (jnp.arange(n // tn) < 2 * (n // 3) // tn).astype(jnp.int32)
    proj = _proj(hb, w_in.astype(BF16), flags, tables, seq, HEAD_DIM // ROPE_FRACTION // 2, tn=tn)
    outs, lses = [], []
    for g, (window, dil) in enumerate(C_PATTERNS):
        o, l = _dil_attn(proj, g, window, dil, batch, seq)
        outs.append(o)
        lses.append(l)
    o = _dil_combine(outs, lses)
    return _outproj_ln(o, w_out.astype(BF16), h, ln)


def _gates_kernel(x_ref, w_ref, o_ref):
    z = _dot(x_ref[...], w_ref[...])
    o_ref[...] = 1.0 / (1.0 + jnp.exp(-z))


def _gates(xb, wg, tm=1024):
    m, k = xb.shape
    return pl.pallas_call(
        _gates_kernel,
        grid=(m // tm,),
        in_specs=[pl.BlockSpec((tm, k), lambda i: (i, 0)), pl.BlockSpec((k, LANES), lambda i: (0, 0))],
        out_specs=pl.BlockSpec((tm, LANES), lambda i: (i, 0)),
        out_shape=jax.ShapeDtypeStruct((m, LANES), F32),
        compiler_params=_cparams(("parallel",)),
        name="nsa_gates",
    )(xb, wg)


def _gelu_tanh(x):
    return 0.5 * x * (1.0 + jnp.tanh(math.sqrt(2.0 / math.pi) * (x + 0.044715 * (x * x * x))))


def _compress_kernel(a_ref, pos_ref, w1a_ref, w1b_ref, w2_ref, c_ref, sp_ref, sm_ref, o_ref, p_acc, q_acc, *, half):
    kind = pl.program_id(2)
    t = pl.program_id(3)
    nt = pl.num_programs(3)

    @pl.when(t == 0)
    def _():
        p_acc[...] = jnp.zeros_like(p_acc)
        q_acc[...] = jnp.zeros_like(q_acc)

    a = a_ref[0].astype(F32)
    p_acc[...] += _dot((a + pos_ref[0, pl.ds(t, 1), :]).astype(BF16), w1a_ref[0])
    q_acc[...] += _dot((a + pos_ref[0, pl.ds(B_CMP_STRIDE + t, 1), :]).astype(BF16), w1b_ref[0])

    @pl.when(t == nt - 1)
    def _():
        n = q_acc.shape[0]
        hmid = p_acc[...] + pltpu.roll(q_acc[...], n - 1, 0)
        out = _dot(_gelu_tanh(hmid).astype(BF16), w2_ref[0])

        @pl.when(kind == 0)
        def _():
            o_ref[0, 0] = _apply_rope(out, c_ref[...], sp_ref[...], sm_ref[...], half).astype(o_ref.dtype)

        @pl.when(kind != 0)
        def _():
            o_ref[0, 0] = out.astype(o_ref.dtype)


def _compress(proj, cmp_pos, cmp_w1, cmp_w2, batch, seq):
    g = B_KV_GROUPS
    n_chunk = seq // B_CMP_STRIDE
    ncol = (B_HEADS + B_N_KV * g)
    view = proj.reshape(batch, n_chunk, B_CMP_STRIDE * ncol * HEAD_DIM)
    cmp_end = jnp.arange(n_chunk) * B_CMP_STRIDE + B_CMP_LEN - 1
    tables = _rope_tables(cmp_end, HEAD_DIM)
    tab_spec = pl.BlockSpec((n_chunk, LANES), lambda b, gg, kk, t: (0, 0))
    w1 = cmp_w1.astype(BF16)
    return pl.pallas_call(
        functools.partial(_compress_kernel, half=HEAD_DIM // ROPE_FRACTION // 2),
        grid=(batch, g, 2, B_CMP_STRIDE),
        in_specs=[pl.BlockSpec((1, n_chunk, HEAD_DIM), lambda b, gg, kk, t: (b, 0, t * ncol + B_HEADS + kk * g + gg)),
                  pl.BlockSpec((1, B_CMP_LEN, HEAD_DIM), lambda b, gg, kk, t: (kk, 0, 0)),
                  pl.BlockSpec((1, HEAD_DIM, HEAD_DIM), lambda b, gg, kk, t: (kk, t, 0)),
                  pl.BlockSpec((1, HEAD_DIM, HEAD_DIM), lambda b, gg, kk, t: (kk, B_CMP_STRIDE + t, 0)),
                  pl.BlockSpec((1, HEAD_DIM, HEAD_DIM), lambda b, gg, kk, t: (kk, 0, 0)),
                  tab_spec, tab_spec, tab_spec],
        out_specs=pl.BlockSpec((1, 1, n_chunk, HEAD_DIM), lambda b, gg, kk, t: (kk * batch + b, gg, 0, 0)),
        out_shape=jax.ShapeDtypeStruct((2 * batch, g, n_chunk, HEAD_DIM), BF16),
        scratch_shapes=[pltpu.VMEM((n_chunk, HEAD_DIM), F32), pltpu.VMEM((n_chunk, HEAD_DIM), F32)],
        compiler_params=_cparams(("parallel", "parallel", "arbitrary", "arbitrary")),
        name="nsa_compress",
    )(view, cmp_pos, w1, w1, cmp_w2.astype(BF16), *tables)


def _split3(x):
    hi = x.astype(BF16)
    r = x - hi.astype(F32)
    mid = r.astype(BF16)
    lo = (r - mid.astype(F32)).astype(BF16)
    return hi, mid, lo


def _online_step(q, k, v, bias, m, l, acc, scale):
    s = _dot_nt(q, k) * scale + bias
    m_new = jnp.maximum(m, jnp.max(s, axis=-1, keepdims=True))
    alpha = jnp.exp(m - m_new)
    e = jnp.exp(s - m_new)
    return m_new, alpha * l + jnp.sum(e, axis=-1, keepdims=True), alpha * acc + _dot(e.astype(BF16), v)


def _nsa_kernel(q_ref, kc_ref, vc_ref, ks_ref, vs_ref, kw_ref, vw_ref, gate_ref, o_ref, *, tq, tk, n_cmp, n_sel):
    g = pl.program_id(1)
    qi = pl.program_id(2)
    rep = B_REP
    scale = HEAD_DIM ** -0.5
    big = 1e30
    qs = [q_ref[0, :, r * HEAD_DIM:(r + 1) * HEAD_DIM] for r in range(rep)]
    qpos = qi * tq + lax.broadcasted_iota(jnp.int32, (tq, LANES), 0)
    lane = lax.broadcasted_iota(jnp.int32, (tq, LANES), 1)

    kc, vc = kc_ref[0, 0], vc_ref[0, 0]
    cmask = (lane * B_CMP_STRIDE + (B_CMP_LEN - 1) <= qpos) & (lane < n_cmp)
    o_cmp = []
    p_sum = jnp.zeros((tq, LANES), F32)
    for r in range(rep):
        s = jnp.where(cmask, _dot_nt(qs[r], kc) * scale, NEG_FILL)
        e = jnp.exp(s - jnp.max(s, axis=-1, keepdims=True))
        p = jnp.where(cmask, e / jnp.sum(e, axis=-1, keepdims=True), 0.0)
        o_cmp.append(_dot(p.astype(BF16), vc))
        p_sum = p_sum + p

    ci = lax.broadcasted_iota(jnp.int32, (LANES, LANES), 0)
    ni = lax.broadcasted_iota(jnp.int32, (LANES, LANES), 1)
    per = B_SEL_LEN // B_CMP_STRIDE
    cover = ((ci >= per * ni - (B_CMP_LEN // B_CMP_STRIDE - 1)) & (ci < per * ni + per) & (ci < n_cmp) & (ni < n_sel))
    cover = jnp.where(cover, 1.0, 0.0).astype(BF16)
    importance = sum(_dot(part, cover) for part in _split3(p_sum))
    cur = jnp.right_shift(qpos, int(math.log2(B_SEL_LEN)))
    forced = (lane == 0) | (lane == cur) | (lane == cur - 1)
    score = jnp.where(forced, big, importance)
    score = jnp.where(lane <= cur, score, -big)
    score = jnp.where(lane < n_sel, score, -2.0 * big)
    rank = jnp.zeros((tq, LANES), jnp.int32)
    for mm in range(n_sel):
        col = jnp.broadcast_to(score[:, mm:mm + 1], (tq, LANES))
        ahead = (col > score) | ((col == score) & (lane > mm))
        rank = rank + jnp.where(ahead, 1, 0)
    sel = jnp.where((rank < min(B_SEL_TOPN, n_sel)) & (lane < n_sel), 1.0, 0.0).astype(BF16)

    qrow = qi * tq + lax.broadcasted_iota(jnp.int32, (tq, tk), 0)
    kiota = lax.broadcasted_iota(jnp.int32, (tq, tk), 1)
    eb = lax.broadcasted_iota(jnp.int32, (LANES, tk), 0)
    ek = lax.broadcasted_iota(jnp.int32, (LANES, tk), 1)

    def sel_body(kj, carry):
        start = pl.multiple_of(kj * tk, tk)
        k = ks_ref[0, pl.ds(start, tk), :]
        v = vs_ref[0, pl.ds(start, tk), :]
        expand = jnp.where(jnp.right_shift(kj * tk + ek, int(math.log2(B_SEL_LEN))) == eb, 1.0, 0.0).astype(BF16)
        picked = _dot(sel, expand)
        bias = jnp.where((picked > 0.5) & (kj * tk + kiota <= qrow), 0.0, -jnp.inf)
        out = []
        for r in range(rep):
            out.extend(_online_step(qs[r], k, v, bias, *carry[3 * r:3 * r + 3], scale))
        return tuple(out)

    minf = jnp.full((tq, 1), -jnp.inf, F32)
    z1 = jnp.zeros((tq, 1), F32)
    za = jnp.zeros((tq, HEAD_DIM), F32)
    init = (minf, z1, za) * rep
    res = lax.fori_loop(0, (qi * tq + tq + tk - 1) // tk, sel_body, init)
    o_sel = [res[3 * r + 2] / res[3 * r + 1] for r in range(rep)]

    state = list(init)
    n_back = (B_WINDOW - 1 + tk - 1) // tk
    for back in range(n_back + 1):
        kj = qi * tq // tk - back
        kjc = jnp.maximum(kj, 0)
        start = pl.multiple_of(kjc * tk, tk)
        k = kw_ref[0, pl.ds(start, tk), :]
        v = vw_ref[0, pl.ds(start, tk), :]
        diff = qrow - (kjc * tk + kiota)
        ok = (diff >= 0) & (diff <= B_WINDOW - 1) & (kj >= 0)
        bias = jnp.where(ok, 0.0, -jnp.inf)
        for r in range(rep):
            state[3 * r:3 * r + 3] = _online_step(qs[r], k, v, bias, *state[3 * r:3 * r + 3], scale)
    o_win = [state[3 * r + 2] / state[3 * r + 1] for r in range(rep)]

    gates = gate_ref[0]
    for r in range(rep):
        base = (g * rep + r) * 3

        def gate(j):
            idx = base + j
            return jnp.sum(jnp.where(lane == idx, gates, 0.0), axis=-1, keepdims=True)

        o = gate(0) * o_cmp[r] + gate(1) * o_sel[r] + gate(2) * o_win[r]
        o_ref[0, :, r * HEAD_DIM:(r + 1) * HEAD_DIM] = o.astype(o_ref.dtype)


def _nsa_attn(proj, kvc, gates, batch, seq, tq=256, tk=256):
    g, rep = B_KV_GROUPS, B_REP
    n_chunk = seq // B_CMP_STRIDE
    n_cmp = n_chunk - B_CMP_LEN // B_CMP_STRIDE + 1
    n_sel = seq // B_SEL_LEN
    assert tq == tk and n_chunk <= LANES and n_sel <= LANES
    proj3 = proj.reshape(batch, seq, -1)

    def kv_spec(j):
        return pl.BlockSpec((1, seq, HEAD_DIM), lambda b, gg, i: (b, 0, B_HEADS + j * g + gg))

    return pl.pallas_call(
        functools.partial(_nsa_kernel, tq=tq, tk=tk, n_cmp=n_cmp, n_sel=n_sel),
        grid=(batch, g, seq // tq),
        in_specs=[pl.BlockSpec((1, tq, rep * HEAD_DIM), lambda b, gg, i: (b, i, gg)),
                  pl.BlockSpec((1, 1, n_chunk, HEAD_DIM), lambda b, gg, i: (b, gg, 0, 0)),
                  pl.BlockSpec((1, 1, n_chunk, HEAD_DIM), lambda b, gg, i: (batch + b, gg, 0, 0)),
                  kv_spec(2), kv_spec(3), kv_spec(4), kv_spec(5),
                  pl.BlockSpec((1, tq, LANES), lambda b, gg, i: (b, i, 0))],
        out_specs=pl.BlockSpec((1, tq, rep * HEAD_DIM), lambda b, gg, i: (b, i, gg)),
        out_shape=jax.ShapeDtypeStruct((batch, seq, B_HEADS * HEAD_DIM), BF16),
        compiler_params=_cparams(("parallel", "parallel", "arbitrary")),
        name="nsa_attn",
    )(proj3, kvc, kvc, proj3, proj3, proj3, proj3, gates.reshape(batch, seq, LANES))


def _mixer_b(hb, h, w_in, cmp_pos, cmp_w1, cmp_w2, w_out, ln, batch, seq):
    g = B_KV_GROUPS
    n_main = (B_HEADS + B_N_KV * g) * HEAD_DIM
    tn = 512
    per = g * HEAD_DIM // tn
    nq = B_HEADS * HEAD_DIM // tn
    jb = jnp.arange(n_main // tn)
    flags = ((jb < nq) | ((jb >= nq + 2 * per) & (jb < nq + 3 * per))
             | ((jb >= nq + 4 * per) & (jb < nq + 5 * per))).astype(jnp.int32)
    tables = _rope_tables(jnp.arange(seq), HEAD_DIM)
    wb = w_in.astype(BF16)
    proj = _proj(hb, wb[:, :n_main], flags, tables, seq, HEAD_DIM // ROPE_FRACTION // 2, tn=tn)
    n_gate = w_in.shape[1] - n_main
    gates = _gates(hb, jnp.pad(wb[:, n_main:], ((0, 0), (0, LANES - n_gate))))
    kvc = _compress(proj, cmp_pos, cmp_w1, cmp_w2, batch, seq)
    o = _nsa_attn(proj, kvc, gates, batch, seq)
    return _outproj_ln(o.reshape(batch * seq, -1), w_out.astype(BF16), h, ln)


def _router_kernel(x_ref, w_ref, b_ref, o_ref):
    logits = jnp.dot(x_ref[...], w_ref[...], precision=lax.Precision.HIGHEST, preferred_element_type=F32) + b_ref[...]
    lane = lax.broadcasted_iota(jnp.int32, logits.shape, 1)
    lane_f = lane.astype(F32)

    def first_max(vals):
        top = jnp.max(vals, axis=-1, keepdims=True)
        first = jnp.min(jnp.where(vals == top, lane_f, float(4 * LANES)), axis=-1, keepdims=True)
        return top, first.astype(jnp.int32)

    gl = jnp.where(lane < MOE_GROUPS, logits, -jnp.inf)
    gmax, gidx = first_max(gl)
    g_w = 1.0 / jnp.sum(jnp.exp(gl - gmax), axis=-1, keepdims=True)
    lo = MOE_GROUPS + MOE_EXPERTS_PER_GROUP * gidx
    ev = jnp.where((lane >= lo) & (lane < lo + MOE_EXPERTS_PER_GROUP), logits, -jnp.inf)
    v1, i1 = first_max(ev)
    v2, i2 = first_max(jnp.where(lane == i1, -jnp.inf, ev))
    e2 = jnp.exp(v2 - v1)
    w1 = g_w / (1.0 + e2)
    w2 = g_w * e2 / (1.0 + e2)
    out = jnp.where(lane == 0, (i1 - MOE_GROUPS).astype(F32),
                    jnp.where(lane == 1, (i2 - MOE_GROUPS).astype(F32),
                              jnp.where(lane == 2, w1, jnp.where(lane == 3, w2, 0.0))))
    o_ref[...] = out


def _router(h, router_w, router_b, tm=512):
    m, d = h.shape
    n = router_w.shape[1]
    wp = jnp.pad(router_w, ((0, 0), (0, LANES - n)))
    bp = jnp.pad(router_b, (0, LANES - n)).reshape(1, LANES)
    return pl.pallas_call(
        _router_kernel,
        grid=(m // tm,),
        in_specs=[pl.BlockSpec((tm, d), lambda i: (i, 0)),
                  pl.BlockSpec((d, LANES), lambda i: (0, 0)),
                  pl.BlockSpec((1, LANES), lambda i: (0, 0))],
        out_specs=pl.BlockSpec((tm, LANES), lambda i: (i, 0)),
        out_shape=jax.ShapeDtypeStruct((m, LANES), F32),
        compiler_params=_cparams(("parallel",)),
        name="moe_router",
    )(h, wp, bp)


def _experts_kernel(be_ref, nu_ref, x_ref, wi_ref, wo_ref, y_ref):
    i = pl.program_id(0)

    @pl.when(i < nu_ref[0])
    def _():
        gu = _dot(x_ref[...], wi_ref[0])
        gate, up = gu[:, :MOE_D_FF], gu[:, MOE_D_FF:]
        act = gate * (1.0 / (1.0 + jnp.exp(-gate))) * up
        y_ref[...] = _dot(act.astype(BF16), wo_ref[0]).astype(y_ref.dtype)

    @pl.when(i >= nu_ref[0])
    def _():
        y_ref[...] = jnp.zeros_like(y_ref)


def _experts(xb, blk_e, n_used, w_in, w_out):
    rows, d = xb.shape
    n_blk = rows // MOE_ROWS
    return pl.pallas_call(
        _experts_kernel,
        grid_spec=pltpu.PrefetchScalarGridSpec(
            num_scalar_prefetch=2,
            grid=(n_blk,),
            in_specs=[pl.BlockSpec((MOE_ROWS, d), lambda i, be, nu: (i, 0)),
                      pl.BlockSpec((1, d, 2 * MOE_D_FF), lambda i, be, nu: (be[i], 0, 0)),
                      pl.BlockSpec((1, MOE_D_FF, d), lambda i, be, nu: (be[i], 0, 0))],
            out_specs=pl.BlockSpec((MOE_ROWS, d), lambda i, be, nu: (i, 0))),
        out_shape=jax.ShapeDtypeStruct((rows, d), BF16),
        compiler_params=_cparams(("arbitrary",)),
        name="moe_experts",
    )(blk_e, n_used, xb, w_in, w_out)


def _moe_ln_kernel(h_ref, y0_ref, y1_ref, r_ref, ln_ref, hf_ref, hb_ref):
    r = r_ref[...]
    f = y0_ref[...].astype(F32) * r[:, 2:3] + y1_ref[...].astype(F32) * r[:, 3:4]
    y = _layer_norm(DN_ALPHA * h_ref[...] + f, ln_ref[2:3, :], ln_ref[3:4, :])
    hf_ref[...] = y
    hb_ref[...] = y.astype(BF16)


def _moe_ln(h, y0, y1, route, ln, tm=512):
    m, d = h.shape
    row = lambda i: (i, 0)
    return pl.pallas_call(
        _moe_ln_kernel,
        grid=(m // tm,),
        in_specs=[pl.BlockSpec((tm, d), row), pl.BlockSpec((tm, d), row), pl.BlockSpec((tm, d), row),
                  pl.BlockSpec((tm, LANES), row), pl.BlockSpec((4, d), lambda i: (0, 0))],
        out_specs=[pl.BlockSpec((tm, d), row), pl.BlockSpec((tm, d), row)],
        out_shape=[jax.ShapeDtypeStruct((m, d), F32), jax.ShapeDtypeStruct((m, d), BF16)],
        compiler_params=_cparams(("parallel",)),
        name="moe_ln",
    )(h, y0, y1, route, ln)


def _moe(h, hb, router_w, router_b, w_in, w_out, ln):
    t, d = h.shape
    e, rows = MOE_EXPERTS, MOE_ROWS
    route = _router(h, router_w, router_b)
    flat_e = route[:, 0:2].astype(jnp.int32).reshape(-1)
    n_assign = flat_e.shape[0]
    n_blk = -(-n_assign // rows) + e
    onehot = (flat_e[:, None] == jnp.arange(e, dtype=jnp.int32)[None, :]).astype(jnp.int32)
    csum = jnp.cumsum(onehot, axis=0)
    rank = jnp.take_along_axis(csum, flat_e[:, None], axis=1)[:, 0] - 1
    sizes = csum[-1]
    padded = (sizes + rows - 1) // rows * rows
    pad_end = jnp.cumsum(padded)
    pad_start = pad_end - padded
    dest = pad_start[flat_e] + rank
    flat_tok = jnp.arange(n_assign, dtype=jnp.int32) // 2
    slot_tok = jnp.full((n_blk * rows,), t, jnp.int32).at[dest].set(flat_tok)
    blk_e = jnp.minimum(jnp.searchsorted(pad_end, jnp.arange(n_blk, dtype=jnp.int32) * rows, side='right'),
                        e - 1).astype(jnp.int32)
    n_used = (pad_end[-1] // rows).astype(jnp.int32).reshape(1)
    xpad = jnp.concatenate([hb, jnp.zeros((1, d), hb.dtype)], axis=0)
    xb = xpad[slot_tok]
    yb = _experts(xb, blk_e, n_used, w_in.astype(BF16), w_out.astype(BF16))
    dest2 = dest.reshape(t, 2)
    return _moe_ln(h, yb[dest2[:, 0]], yb[dest2[:, 1]], route, ln)


def kernel(x, l0_a_w_in, l0_a_lam, l0_a_subln, l0_a_w_out, l0_ln, l0_router_w, l0_router_b, l0_moe_w_in, l0_moe_w_out, l1_b_w_in, l1_b_cmp_pos, l1_b_cmp_w1, l1_b_cmp_w2, l1_b_w_out, l1_ln, l1_router_w, l1_router_b, l1_moe_w_in, l1_moe_w_out, l2_c_w_in, l2_c_w_out, l2_ln, l2_router_w, l2_router_b, l2_moe_w_in, l2_moe_w_out, l3_a_w_in, l3_a_lam, l3_a_subln, l3_a_w_out, l3_ln, l3_router_w, l3_router_b, l3_moe_w_in, l3_moe_w_out):
    batch, seq, d = x.shape
    h = x.reshape(batch * seq, d)
    hb = h.astype(BF16)
    h, hb = _mixer_a(hb, h, l0_a_w_in, l0_a_lam, l0_a_subln, l0_a_w_out, l0_ln, 0, batch, seq)
    h, hb = _moe(h, hb, l0_router_w, l0_router_b, l0_moe_w_in, l0_moe_w_out, l0_ln)
    h, hb = _mixer_b(hb, h, l1_b_w_in, l1_b_cmp_pos, l1_b_cmp_w1, l1_b_cmp_w2, l1_b_w_out, l1_ln, batch, seq)
    h, hb = _moe(h, hb, l1_router_w, l1_router_b, l1_moe_w_in, l1_moe_w_out, l1_ln)
    h, hb = _mixer_c(hb, h, l2_c_w_in, l2_c_w_out, l2_ln, batch, seq)
    h, hb = _moe(h, hb, l2_router_w, l2_router_b, l2_moe_w_in, l2_moe_w_out, l2_ln)
    h, hb = _mixer_a(hb, h, l3_a_w_in, l3_a_lam, l3_a_subln, l3_a_w_out, l3_ln, 3, batch, seq)
    h, hb = _moe(h, hb, l3_router_w, l3_router_b, l3_moe_w_in, l3_moe_w_out, l3_ln)
    return h.reshape(batch, seq, d)
```

```python
import functools
import math

import jax
import jax.numpy as jnp
from jax import lax
from jax.experimental import pallas as pl
from jax.experimental.pallas import tpu as pltpu

D_MODEL = 2048
DEPTH = 4
HEAD_DIM = 128
ROPE_THETA = 500000.0
ROPE_FRACTION = 4
NEG_FILL = -1e30

A_HEADS = D_MODEL // HEAD_DIM
A_QK_DIM = HEAD_DIM // 2

B_HEADS = D_MODEL // HEAD_DIM
B_KV_GROUPS = 4
B_REP = B_HEADS // B_KV_GROUPS
B_CMP_LEN = 32
B_CMP_STRIDE = 16
B_SEL_LEN = 64
B_SEL_TOPN = 16
B_WINDOW = 512
B_N_KV = 6

C_PATTERNS = ((128, 1), (512, 4), (2048, 16))
C_HEADS_PER_GROUP = D_MODEL // (2 * HEAD_DIM)

MOE_GROUPS = 4
MOE_EXPERTS_PER_GROUP = 8
MOE_EXPERTS = MOE_GROUPS * MOE_EXPERTS_PER_GROUP
MOE_D_FF = D_MODEL // 4
MOE_ROWS = 256

DN_ALPHA = (2 * DEPTH) ** 0.25
NORM_EPS = 1e-5

LANES = 128
VMEM_LIMIT = 56 * 1024 * 1024

BF16 = jnp.bfloat16
F32 = jnp.float32


def _cparams(sem):
    return pltpu.CompilerParams(dimension_semantics=sem, vmem_limit_bytes=VMEM_LIMIT)


def _dot(a, b):
    return jnp.dot(a, b, preferred_element_type=F32)


def _dot_nt(a, b):
    return lax.dot_general(a, b, (((1,), (1,)), ((), ())), preferred_element_type=F32)


def _rope_tables(pos, dim):
    rot = dim // ROPE_FRACTION
    half = rot // 2
    inv_freq = ROPE_THETA ** (-jnp.arange(0, rot, 2, dtype=F32) / rot)
    ang = pos.astype(F32)[:, None] * inv_freq[None, :]
    cos, sin = jnp.cos(ang), jnp.sin(ang)
    n = pos.shape[0]
    ones = jnp.ones((n, dim - rot), F32)
    zeros = jnp.zeros((n, dim - rot), F32)
    zh = jnp.zeros((n, half), F32)
    c = jnp.concatenate([cos, cos, ones], axis=1)
    sp = jnp.concatenate([zh, sin, zeros], axis=1)
    sm = jnp.concatenate([-sin, zh, zeros], axis=1)
    reps = LANES // dim
    return tuple(jnp.tile(t, (1, reps)) for t in (c, sp, sm))


def _apply_rope(a, c, sp, sm, half):
    return a * c + pltpu.roll(a, half, 1) * sp + pltpu.roll(a, LANES - half, 1) * sm


def _proj_kernel(flags_ref, x_ref, w_ref, c_ref, sp_ref, sm_ref, o_ref, *, half):
    j = pl.program_id(1)
    acc = _dot(x_ref[...], w_ref[...])
    tn = acc.shape[1]

    @pl.when(flags_ref[j] == 0)
    def _():
        o_ref[...] = acc.astype(o_ref.dtype)

    @pl.when(flags_ref[j] != 0)
    def _():
        c, sp, sm = c_ref[...], sp_ref[...], sm_ref[...]
        for s in range(tn // LANES):
            a = acc[:, s * LANES:(s + 1) * LANES]
            o_ref[:, s * LANES:(s + 1) * LANES] = _apply_rope(a, c, sp, sm, half).astype(o_ref.dtype)


def _proj(xb, w, flags, tables, seq, half, tm=1024, tn=512):
    m, k = xb.shape
    n = w.shape[1]
    assert m % tm == 0 and n % tn == 0 and seq % tm == 0
    pos_blocks = seq // tm
    tab_spec = pl.BlockSpec((tm, LANES), lambda i, j, f: (i % pos_blocks, 0))
    return pl.pallas_call(
        functools.partial(_proj_kernel, half=half),
        grid_spec=pltpu.PrefetchScalarGridSpec(
            num_scalar_prefetch=1,
            grid=(m // tm, n // tn),
            in_specs=[pl.BlockSpec((tm, k), lambda i, j, f: (i, 0)),
                      pl.BlockSpec((k, tn), lambda i, j, f: (0, j)),
                      tab_spec, tab_spec, tab_spec],
            out_specs=pl.BlockSpec((tm, tn), lambda i, j, f: (i, j))),
        out_shape=jax.ShapeDtypeStruct((m, n), BF16),
        compiler_params=_cparams(("parallel", "arbitrary")),
        name="proj",
    )(flags, xb, w, *tables)


def _layer_norm(z, g, b):
    mu = jnp.mean(z, axis=-1, keepdims=True)
    zc = z - mu
    var = jnp.mean(zc * zc, axis=-1, keepdims=True)
    return zc * lax.rsqrt(var + NORM_EPS) * g + b


def _outproj_ln_kernel(o_ref, w_ref, h_ref, ln_ref, hf_ref, hb_ref):
    m = _dot(o_ref[...], w_ref[...])
    y = _layer_norm(DN_ALPHA * h_ref[...] + m, ln_ref[0:1, :], ln_ref[1:2, :])
    hf_ref[...] = y
    hb_ref[...] = y.astype(BF16)


def _outproj_ln(o, w, h, ln, tm=256):
    m, k = o.shape
    d = w.shape[1]
    return pl.pallas_call(
        _outproj_ln_kernel,
        grid=(m // tm,),
        in_specs=[pl.BlockSpec((tm, k), lambda i: (i, 0)),
                  pl.BlockSpec((k, d), lambda i: (0, 0)),
                  pl.BlockSpec((tm, d), lambda i: (i, 0)),
                  pl.BlockSpec((4, d), lambda i: (0, 0))],
        out_specs=[pl.BlockSpec((tm, d), lambda i: (i, 0)),
                   pl.BlockSpec((tm, d), lambda i: (i, 0))],
        out_shape=[jax.ShapeDtypeStruct((m, d), F32), jax.ShapeDtypeStruct((m, d), BF16)],
        compiler_params=_cparams(("parallel",)),
        name="outproj_ln",
    )(o, w, h, ln)


def _causal_block(q, k, v, c, tq):
    s = _dot_nt(q, k)
    causal = lax.broadcasted_iota(jnp.int32, (tq, tq), 1) <= lax.broadcasted_iota(jnp.int32, (tq, tq), 0)
    s_diag = jnp.where(causal, s[:, c * tq:], -jnp.inf)
    m = jnp.max(s_diag, axis=-1, keepdims=True)
    if c > 0:
        s_off = s[:, :c * tq]
        m = jnp.maximum(m, jnp.max(s_off, axis=-1, keepdims=True))
        e = jnp.concatenate([jnp.exp(s_off - m), jnp.exp(s_diag - m)], axis=1)
    else:
        e = jnp.exp(s_diag - m)
    inv = 1.0 / jnp.sum(e, axis=-1, keepdims=True)
    return _dot(e.astype(BF16), v) * inv


def _attn_a_kernel(lam_ref, q_ref, k_ref, v_ref, g_ref, o_ref, *, tq, seq, out_scale):
    for c in range(seq // tq):
        kv = (c + 1) * tq
        q = q_ref[0, c * tq:kv, :] * jnp.asarray(A_QK_DIM ** -0.5, BF16)
        lane = lax.broadcasted_iota(jnp.int32, q.shape, 1)
        zero = jnp.zeros_like(q)
        k, v = k_ref[0, :kv, :], v_ref[0, :kv, :]
        o = (_causal_block(jnp.where(lane < A_QK_DIM, q, zero), k, v, c, tq)
             - lam_ref[0] * _causal_block(jnp.where(lane >= A_QK_DIM, q, zero), k, v, c, tq))
        o = o * lax.rsqrt(jnp.mean(o * o, axis=-1, keepdims=True) + NORM_EPS) * (g_ref[...] * out_scale)
        o_ref[0, c * tq:kv, :] = o.astype(o_ref.dtype)


def _attn_a(proj, lam_full, subln, batch, seq, lam_init, tq=256):
    h = A_HEADS
    return pl.pallas_call(
        functools.partial(_attn_a_kernel, tq=tq, seq=seq, out_scale=1.0 - lam_init),
        grid_spec=pltpu.PrefetchScalarGridSpec(
            num_scalar_prefetch=1,
            grid=(batch, h),
            in_specs=[pl.BlockSpec((1, seq, HEAD_DIM), lambda b, hh, s: (b, 0, hh)),
                      pl.BlockSpec((1, seq, HEAD_DIM), lambda b, hh, s: (b, 0, h + hh)),
                      pl.BlockSpec((1, seq, HEAD_DIM), lambda b, hh, s: (b, 0, 2 * h + hh)),
                      pl.BlockSpec((1, HEAD_DIM), lambda b, hh, s: (0, 0))],
            out_specs=pl.BlockSpec((1, seq, HEAD_DIM), lambda b, hh, s: (b, 0, hh))),
        out_shape=jax.ShapeDtypeStruct((batch, seq, h * HEAD_DIM), BF16),
        compiler_params=_cparams(("parallel", "parallel")),
        name="attn_a",
    )(lam_full, proj, proj, proj, subln)


def _mixer_a(hb, h, w_in, lam, subln, w_out, ln, layer_idx, batch, seq):
    pos = jnp.arange(seq)
    tables = _rope_tables(pos, A_QK_DIM)
    n_q = A_HEADS * HEAD_DIM
    tn = 512
    flags = (jnp.arange(3 * n_q // tn) < 2 * n_q // tn).astype(jnp.int32)
    proj = _proj(hb, w_in.astype(BF16), flags, tables, seq, A_QK_DIM // ROPE_FRACTION // 2, tn=tn)
    lam_init = 0.8 - 0.6 * math.exp(-0.3 * layer_idx)
    lf = lam.astype(F32)
    lam_full = jnp.exp(jnp.sum(lf[0] * lf[1])) - jnp.exp(jnp.sum(lf[2] * lf[3])) + lam_init
    o = _attn_a(proj.reshape(batch, seq, -1), lam_full.reshape(1), subln.reshape(1, HEAD_DIM), batch, seq, lam_init)
    return _outproj_ln(o.reshape(batch * seq, -1), w_out.astype(BF16), h, ln)


DIL_BLOCK = 128


def _band_block(q, k, v, q0, k0, span):
    s = _dot_nt(q, k)
    diff = (q0 + lax.broadcasted_iota(jnp.int32, s.shape, 0)) - (k0 + lax.broadcasted_iota(jnp.int32, s.shape, 1))
    s = jnp.where((diff >= 0) & (diff <= span), s, -jnp.inf)
    m = jnp.max(s, axis=-1, keepdims=True)
    e = jnp.exp(s - m)
    den = jnp.sum(e, axis=-1, keepdims=True)
    return _dot((e * (1.0 / den)).astype(BF16), v), m + jnp.log(den)


def _dil_kernel(*refs, seq):
    n_grp = len(C_PATTERNS)
    qkv = refs[:3 * n_grp]
    o_ref = refs[3 * n_grp]
    stage, o_scr, l_scr = refs[3 * n_grp + 1:]
    blk = DIL_BLOCK
    for g, (window, dil) in enumerate(C_PATTERNS):
        q_ref, k_ref, v_ref = qkv[3 * g:3 * g + 3]
        length = seq // dil
        span = window // dil
        n_blk = length // blk
        if dil > 1:
            for j, ref in enumerate((q_ref, k_ref, v_ref)):
                stage[j] = ref[0].astype(F32)

        def rows(j, ref, r, start, size):
            if dil == 1:
                return ref[0, start:start + size, :]
            return stage[j, pl.ds(r + start * dil, size, stride=dil), :].astype(BF16)

        def residue(r, carry):
            for i in range(n_blk):
                k0 = max(i - 1, 0) * blk
                kl = (i + 1) * blk - k0
                o, lse = _band_block(rows(0, q_ref, r, i * blk, blk), rows(1, k_ref, r, k0, kl),
                                     rows(2, v_ref, r, k0, kl), i * blk, k0, span)
                lse = jnp.broadcast_to(lse, (blk, HEAD_DIM))
                if dil == 1:
                    o_scr[g, i * blk:(i + 1) * blk, :] = o
                    l_scr[g, i * blk:(i + 1) * blk, :] = lse
                else:
                    o_scr[g, pl.ds(r + i * blk * dil, blk, stride=dil), :] = o
                    l_scr[g, pl.ds(r + i * blk * dil, blk, stride=dil), :] = lse
            return carry

        if dil == 1:
            residue(0, 0)
        else:
            lax.fori_loop(0, dil, residue, 0, unroll=2 if n_blk == 1 else 1)

    chunk = 256
    for c in range(seq // chunk):
        sl = slice(c * chunk, (c + 1) * chunk)
        ls = [l_scr[g, sl, :] for g in range(n_grp)]
        m = functools.reduce(jnp.maximum, ls)
        es = [jnp.exp(l - m) for l in ls]
        inv = 1.0 / functools.reduce(lambda a, b: a + b, es)
        o = functools.reduce(lambda a, b: a + b, [(es[g] * inv) * o_scr[g, sl, :] for g in range(n_grp)])
        o_ref[0, sl, :] = o.astype(o_ref.dtype)


def _dil_attn(proj, batch, seq):
    hg, p = C_HEADS_PER_GROUP, len(C_PATTERNS)

    def spec(kind, g):
        return pl.BlockSpec((1, seq, HEAD_DIM), lambda b, hh: (b, 0, kind * p * hg + g * hg + hh))

    in_specs = [spec(kind, g) for g in range(p) for kind in range(3)]
    return pl.pallas_call(
        functools.partial(_dil_kernel, seq=seq),
        grid=(batch, hg),
        in_specs=in_specs,
        out_specs=pl.BlockSpec((1, seq, HEAD_DIM), lambda b, hh: (b, 0, hh)),
        out_shape=jax.ShapeDtypeStruct((batch, seq, hg * HEAD_DIM), BF16),
        scratch_shapes=[pltpu.VMEM((3, seq, HEAD_DIM), F32), pltpu.VMEM((p, seq, HEAD_DIM), F32),
                        pltpu.VMEM((p, seq, HEAD_DIM), F32)],
        compiler_params=_cparams(("parallel", "parallel")),
        name="dil_attn",
    )(*([proj] * len(in_specs)))


def _mixer_c(hb, h, w_in, w_out, ln, batch, seq):
    tables = _rope_tables(jnp.arange(seq), HEAD_DIM)
    tn = 512
    n = w_in.shape[1]
    flags = (jnp.arange(n // tn) < 2 * (n // 3) // tn).astype(jnp.int32)
    wb = _scaled_bf16(w_in, n // 3, HEAD_DIM ** -0.5)
    proj = _proj(hb, wb, flags, tables, seq, HEAD_DIM // ROPE_FRACTION // 2, tn=tn)
    o = _dil_attn(proj.reshape(batch, seq, n), batch, seq)
    return _outproj_ln(o.reshape(batch * seq, -1), w_out.astype(BF16), h, ln)


def _gates_kernel(x_ref, w_ref, o_ref):
    z = _dot(x_ref[...], w_ref[...])
    o_ref[...] = 1.0 / (1.0 + jnp.exp(-z))


def _gates(xb, wg, tm=1024):
    m, k = xb.shape
    n = wg.shape[1]
    return pl.pallas_call(
        _gates_kernel,
        grid=(m // tm,),
        in_specs=[pl.BlockSpec((tm, k), lambda i: (i, 0)), pl.BlockSpec((k, n), lambda i: (0, 0))],
        out_specs=pl.BlockSpec((tm, n), lambda i: (i, 0)),
        out_shape=jax.ShapeDtypeStruct((m, n), F32),
        compiler_params=_cparams(("parallel",)),
        name="nsa_gates",
    )(xb, wg)


def _gelu_tanh(x):
    return 0.5 * x * (1.0 + jnp.tanh(math.sqrt(2.0 / math.pi) * (x + 0.044715 * (x * x * x))))


def _compress_kernel(a_ref, pos_ref, w1_ref, w2_ref, c_ref, sp_ref, sm_ref, o_ref, stage, *, half):
    kind = pl.program_id(2)
    stage[...] = a_ref[0].astype(F32)
    n_chunk = stage.shape[0] // B_CMP_STRIDE
    first = jnp.zeros((n_chunk, HEAD_DIM), F32)
    second = jnp.zeros((n_chunk, HEAD_DIM), F32)
    for t in range(B_CMP_STRIDE):
        a = stage[pl.ds(t, n_chunk, stride=B_CMP_STRIDE), :]
        u = B_CMP_STRIDE + t
        first = first + _dot((a + pos_ref[0, t:t + 1, :]).astype(BF16), w1_ref[0, t * HEAD_DIM:(t + 1) * HEAD_DIM, :])
        second = second + _dot((a + pos_ref[0, u:u + 1, :]).astype(BF16), w1_ref[0, u * HEAD_DIM:(u + 1) * HEAD_DIM, :])
    hmid = first + pltpu.roll(second, n_chunk - 1, 0)
    out = _dot(_gelu_tanh(hmid).astype(BF16), w2_ref[0])

    @pl.when(kind == 0)
    def _():
        o_ref[0, 0] = _apply_rope(out, c_ref[...], sp_ref[...], sm_ref[...], half).astype(o_ref.dtype)

    @pl.when(kind != 0)
    def _():
        o_ref[0, 0] = out.astype(o_ref.dtype)


def _compress(proj3, cmp_pos, cmp_w1, cmp_w2, batch, seq):
    g = B_KV_GROUPS
    n_chunk = seq // B_CMP_STRIDE
    cmp_end = jnp.arange(n_chunk) * B_CMP_STRIDE + B_CMP_LEN - 1
    tables = _rope_tables(cmp_end, HEAD_DIM)
    tab_spec = pl.BlockSpec((n_chunk, LANES), lambda b, gg, kk: (0, 0))
    return pl.pallas_call(
        functools.partial(_compress_kernel, half=HEAD_DIM // ROPE_FRACTION // 2),
        grid=(batch, g, 2),
        in_specs=[pl.BlockSpec((1, seq, HEAD_DIM), lambda b, gg, kk: (b, 0, B_HEADS + kk * g + gg)),
                  pl.BlockSpec((1, B_CMP_LEN, HEAD_DIM), lambda b, gg, kk: (kk, 0, 0)),
                  pl.BlockSpec((1, B_CMP_LEN * HEAD_DIM, HEAD_DIM), lambda b, gg, kk: (kk, 0, 0)),
                  pl.BlockSpec((1, HEAD_DIM, HEAD_DIM), lambda b, gg, kk: (kk, 0, 0)),
                  tab_spec, tab_spec, tab_spec],
        out_specs=pl.BlockSpec((1, 1, n_chunk, HEAD_DIM), lambda b, gg, kk: (kk * batch + b, gg, 0, 0)),
        out_shape=jax.ShapeDtypeStruct((2 * batch, g, n_chunk, HEAD_DIM), BF16),
        scratch_shapes=[pltpu.VMEM((seq, HEAD_DIM), F32)],
        compiler_params=_cparams(("parallel", "parallel", "arbitrary")),
        name="nsa_compress",
    )(proj3, cmp_pos, cmp_w1.astype(BF16), cmp_w2.astype(BF16), *tables)


def _split3(x):
    hi = x.astype(BF16)
    r = x - hi.astype(F32)
    mid = r.astype(BF16)
    lo = (r - mid.astype(F32)).astype(BF16)
    return hi, mid, lo


def _nsa_kernel(q_ref, kc_ref, vc_ref, ks_ref, vs_ref, kw_ref, vw_ref, gate_ref, ex_ref, o_ref,
                vct_scr, ocmp_scr, *, tq, seq, n_cmp, n_sel):
    rep = B_REP
    big = 1e30
    vct_scr[...] = vc_ref[0, 0].astype(F32).T.astype(BF16)
    kc = kc_ref[0, 0]
    lane = lax.broadcasted_iota(jnp.int32, (tq, LANES), 1)
    n_back = -(-(B_WINDOW - 1) // tq)

    for c in range(seq // tq):
        lo, hi = c * tq, (c + 1) * tq

        def q_head(r):
            return q_ref[0, lo:hi, pl.ds(pl.multiple_of(r * HEAD_DIM, HEAD_DIM), HEAD_DIM)]

        qpos = lo + lax.broadcasted_iota(jnp.int32, (1, tq), 1)
        crow = lax.broadcasted_iota(jnp.int32, (LANES, tq), 0)
        cmask = (crow * B_CMP_STRIDE + (B_CMP_LEN - 1) <= qpos) & (crow < n_cmp)

        p_sum = jnp.zeros((LANES, tq), F32)
        for r in range(rep):
            s = jnp.where(cmask, _dot_nt(kc, q_ref[0, lo:hi, r * HEAD_DIM:(r + 1) * HEAD_DIM]), NEG_FILL)
            e = jnp.exp(s - jnp.max(s, axis=0, keepdims=True))
            p = jnp.where(cmask, e * (1.0 / jnp.sum(e, axis=0, keepdims=True)), 0.0)
            ocmp_scr[r] = _dot(vct_scr[...], p.astype(BF16)).T
            p_sum = p_sum + p

        ni = lax.broadcasted_iota(jnp.int32, (LANES, LANES), 0)
        ci = lax.broadcasted_iota(jnp.int32, (LANES, LANES), 1)
        per = B_SEL_LEN // B_CMP_STRIDE
        cover = ((ci >= per * ni - (B_CMP_LEN // B_CMP_STRIDE - 1)) & (ci < per * ni + per) & (ci < n_cmp) & (ni < n_sel))
        cover = jnp.where(cover, 1.0, 0.0).astype(BF16)
        importance = sum(_dot(cover, part) for part in _split3(p_sum))[:n_sel]
        nrow = lax.broadcasted_iota(jnp.int32, (n_sel, tq), 0)
        cur = jnp.right_shift(qpos, int(math.log2(B_SEL_LEN)))
        forced = (nrow == 0) | (nrow == cur) | (nrow == cur - 1)
        score = jnp.where(forced, big, importance)
        score = jnp.where(nrow <= cur, score, -big)
        rank = jnp.zeros((n_sel, tq), jnp.int32)
        for mm in range(min(n_sel, (hi - 1) // B_SEL_LEN + 1)):
            other = score[mm:mm + 1, :]
            ahead = (other > score) | ((other == score) & (nrow > mm))
            rank = rank + jnp.where(ahead, 1, 0)
        sel_bias = jnp.where(rank < min(B_SEL_TOPN, n_sel), 0.0, -big)
        sel_bias = jnp.concatenate([sel_bias, jnp.zeros((LANES - n_sel, tq), F32)], axis=0).T.astype(BF16)

        k_aug = jnp.concatenate([ks_ref[0, :hi, :], ex_ref[:hi, :]], axis=1)
        v_sel = vs_ref[0, :hi, :]
        wlo = max(c - n_back, 0) * tq
        k_win, v_win = kw_ref[0, wlo:hi, :], vw_ref[0, wlo:hi, :]
        diff = (lo + lax.broadcasted_iota(jnp.int32, (tq, hi - wlo), 0)) - (wlo + lax.broadcasted_iota(jnp.int32, (tq, hi - wlo), 1))
        wmask = (diff >= 0) & (diff <= B_WINDOW - 1)
        gates = gate_ref[0, lo:hi, :]

        def head(r, carry):
            q = q_head(r)
            o_sel = _causal_block(jnp.concatenate([q, sel_bias], axis=1), k_aug, v_sel, c, tq)
            s = jnp.where(wmask, _dot_nt(q, k_win), -jnp.inf)
            e = jnp.exp(s - jnp.max(s, axis=-1, keepdims=True))
            o_win = _dot(e.astype(BF16), v_win) * (1.0 / jnp.sum(e, axis=-1, keepdims=True))

            def gate(j):
                return jnp.sum(jnp.where(lane == r * 3 + j, gates, 0.0), axis=-1, keepdims=True)

            o = gate(0) * ocmp_scr[r] + gate(1) * o_sel + gate(2) * o_win
            o_ref[0, lo:hi, pl.ds(pl.multiple_of(r * HEAD_DIM, HEAD_DIM), HEAD_DIM)] = o.astype(o_ref.dtype)
            return carry

        lax.fori_loop(0, rep, head, 0)


def _nsa_attn(proj3, kvc, gates, batch, seq, tq=256):
    g, rep = B_KV_GROUPS, B_REP
    n_chunk = seq // B_CMP_STRIDE
    n_cmp = n_chunk - B_CMP_LEN // B_CMP_STRIDE + 1
    n_sel = seq // B_SEL_LEN
    assert n_chunk == LANES and n_sel <= LANES and n_sel % 8 == 0
    onehot = (jnp.arange(seq)[:, None] // B_SEL_LEN == jnp.arange(LANES)[None, :]).astype(BF16)

    def kv_spec(j):
        return pl.BlockSpec((1, seq, HEAD_DIM), lambda b, gg: (b, 0, B_HEADS + j * g + gg))

    return pl.pallas_call(
        functools.partial(_nsa_kernel, tq=tq, seq=seq, n_cmp=n_cmp, n_sel=n_sel),
        grid=(batch, g),
        in_specs=[pl.BlockSpec((1, seq, rep * HEAD_DIM), lambda b, gg: (b, 0, gg)),
                  pl.BlockSpec((1, 1, n_chunk, HEAD_DIM), lambda b, gg: (b, gg, 0, 0)),
                  pl.BlockSpec((1, 1, n_chunk, HEAD_DIM), lambda b, gg: (batch + b, gg, 0, 0)),
                  kv_spec(2), kv_spec(3), kv_spec(4), kv_spec(5),
                  pl.BlockSpec((1, seq, LANES), lambda b, gg: (b, 0, gg)),
                  pl.BlockSpec((seq, LANES), lambda b, gg: (0, 0))],
        out_specs=pl.BlockSpec((1, seq, rep * HEAD_DIM), lambda b, gg: (b, 0, gg)),
        out_shape=jax.ShapeDtypeStruct((batch, seq, B_HEADS * HEAD_DIM), BF16),
        scratch_shapes=[pltpu.VMEM((HEAD_DIM, n_chunk), BF16), pltpu.VMEM((rep, tq, HEAD_DIM), F32)],
        compiler_params=_cparams(("parallel", "parallel")),
        name="nsa_attn",
    )(proj3, kvc, kvc, proj3, proj3, proj3, proj3, gates.reshape(batch, seq, g * LANES), onehot)


def _scaled_bf16(w, n_cols, scale):
    col_scale = jnp.where(jnp.arange(w.shape[1]) < n_cols, scale, 1.0).astype(F32)
    return (w * col_scale[None, :]).astype(BF16)


def _mixer_b(hb, h, w_in, cmp_pos, cmp_w1, cmp_w2, w_out, ln, batch, seq):
    g, rep = B_KV_GROUPS, B_REP
    n_main = (B_HEADS + B_N_KV * g) * HEAD_DIM
    tn = 512
    per = g * HEAD_DIM // tn
    nq = B_HEADS * HEAD_DIM // tn
    jb = jnp.arange(n_main // tn)
    flags = ((jb < nq) | ((jb >= nq + 2 * per) & (jb < nq + 3 * per))
             | ((jb >= nq + 4 * per) & (jb < nq + 5 * per))).astype(jnp.int32)
    tables = _rope_tables(jnp.arange(seq), HEAD_DIM)
    wb = _scaled_bf16(w_in, B_HEADS * HEAD_DIM, HEAD_DIM ** -0.5)
    proj3 = _proj(hb, wb[:, :n_main], flags, tables, seq, HEAD_DIM // ROPE_FRACTION // 2, tn=tn).reshape(batch, seq, n_main)
    wg = wb[:, n_main:].reshape(-1, g, rep * 3)
    wg = jnp.pad(wg, ((0, 0), (0, 0), (0, LANES - rep * 3))).reshape(-1, g * LANES)
    gates = _gates(hb, wg)
    kvc = _compress(proj3, cmp_pos, cmp_w1, cmp_w2, batch, seq)
    o = _nsa_attn(proj3, kvc, gates, batch, seq)
    return _outproj_ln(o.reshape(batch * seq, -1), w_out.astype(BF16), h, ln)


def _router_kernel(x_ref, w_ref, b_ref, o_ref):
    logits = jnp.dot(x_ref[...], w_ref[...], precision=lax.Precision.HIGHEST, preferred_element_type=F32) + b_ref[...]
    lane = lax.broadcasted_iota(jnp.int32, logits.shape, 1)
    lane_f = lane.astype(F32)

    def first_max(vals):
        top = jnp.max(vals, axis=-1, keepdims=True)
        first = jnp.min(jnp.where(vals == top, lane_f, float(4 * LANES)), axis=-1, keepdims=True)
        return top, first.astype(jnp.int32)

    gl = jnp.where(lane < MOE_GROUPS, logits, -jnp.inf)
    gmax, gidx = first_max(gl)
    g_w = 1.0 / jnp.sum(jnp.exp(gl - gmax), axis=-1, keepdims=True)
    lo = MOE_GROUPS + MOE_EXPERTS_PER_GROUP * gidx
    ev = jnp.where((lane >= lo) & (lane < lo + MOE_EXPERTS_PER_GROUP), logits, -jnp.inf)
    v1, i1 = first_max(ev)
    v2, i2 = first_max(jnp.where(lane == i1, -jnp.inf, ev))
    e2 = jnp.exp(v2 - v1)
    w1 = g_w / (1.0 + e2)
    w2 = g_w * e2 / (1.0 + e2)
    out = jnp.where(lane == 0, (i1 - MOE_GROUPS).astype(F32),
                    jnp.where(lane == 1, (i2 - MOE_GROUPS).astype(F32),
                              jnp.where(lane == 2, w1, jnp.where(lane == 3, w2, 0.0))))
    o_ref[...] = out


def _router(h, router_w, router_b, tm=512):
    m, d = h.shape
    n = router_w.shape[1]
    wp = jnp.pad(router_w, ((0, 0), (0, LANES - n)))
    bp = jnp.pad(router_b, (0, LANES - n)).reshape(1, LANES)
    return pl.pallas_call(
        _router_kernel,
        grid=(m // tm,),
        in_specs=[pl.BlockSpec((tm, d), lambda i: (i, 0)),
                  pl.BlockSpec((d, LANES), lambda i: (0, 0)),
                  pl.BlockSpec((1, LANES), lambda i: (0, 0))],
        out_specs=pl.BlockSpec((tm, LANES), lambda i: (i, 0)),
        out_shape=jax.ShapeDtypeStruct((m, LANES), F32),
        compiler_params=_cparams(("parallel",)),
        name="moe_router",
    )(h, wp, bp)


def _experts_kernel(be_ref, nu_ref, x_ref, wi_ref, wo_ref, y_ref, wi_scr, wo_scr):
    i = pl.program_id(0)

    @pl.when((i == 0) | (be_ref[i] != be_ref[jnp.maximum(i - 1, 0)]))
    def _():
        wi_scr[...] = wi_ref[0].astype(BF16)
        wo_scr[...] = wo_ref[0].astype(BF16)

    @pl.when(i < nu_ref[0])
    def _():
        gu = _dot(x_ref[...], wi_scr[...])
        gate, up = gu[:, :MOE_D_FF], gu[:, MOE_D_FF:]
        act = gate * (1.0 / (1.0 + jnp.exp(-gate))) * up
        y_ref[...] = _dot(act.astype(BF16), wo_scr[...]).astype(y_ref.dtype)

    @pl.when(i >= nu_ref[0])
    def _():
        y_ref[...] = jnp.zeros_like(y_ref)


def _experts(xb, blk_e, n_used, w_in, w_out):
    rows, d = xb.shape
    n_blk = rows // MOE_ROWS
    return pl.pallas_call(
        _experts_kernel,
        grid_spec=pltpu.PrefetchScalarGridSpec(
            num_scalar_prefetch=2,
            grid=(n_blk,),
            in_specs=[pl.BlockSpec((MOE_ROWS, d), lambda i, be, nu: (i, 0)),
                      pl.BlockSpec((1, d, 2 * MOE_D_FF), lambda i, be, nu: (be[i], 0, 0)),
                      pl.BlockSpec((1, MOE_D_FF, d), lambda i, be, nu: (be[i], 0, 0))],
            out_specs=pl.BlockSpec((MOE_ROWS, d), lambda i, be, nu: (i, 0)),
            scratch_shapes=[pltpu.VMEM((d, 2 * MOE_D_FF), BF16), pltpu.VMEM((MOE_D_FF, d), BF16)]),
        out_shape=jax.ShapeDtypeStruct((rows, d), BF16),
        compiler_params=_cparams(("arbitrary",)),
        name="moe_experts",
    )(blk_e, n_used, xb, w_in, w_out)


def _moe_ln_kernel(h_ref, y0_ref, y1_ref, r_ref, ln_ref, hf_ref, hb_ref):
    r = r_ref[...]
    f = y0_ref[...].astype(F32) * r[:, 2:3] + y1_ref[...].astype(F32) * r[:, 3:4]
    y = _layer_norm(DN_ALPHA * h_ref[...] + f, ln_ref[2:3, :], ln_ref[3:4, :])
    hf_ref[...] = y
    hb_ref[...] = y.astype(BF16)


def _moe_ln(h, y0, y1, route, ln, tm=512):
    m, d = h.shape
    row = lambda i: (i, 0)
    return pl.pallas_call(
        _moe_ln_kernel,
        grid=(m // tm,),
        in_specs=[pl.BlockSpec((tm, d), row), pl.BlockSpec((tm, d), row), pl.BlockSpec((tm, d), row),
                  pl.BlockSpec((tm, LANES), row), pl.BlockSpec((4, d), lambda i: (0, 0))],
        out_specs=[pl.BlockSpec((tm, d), row), pl.BlockSpec((tm, d), row)],
        out_shape=[jax.ShapeDtypeStruct((m, d), F32), jax.ShapeDtypeStruct((m, d), BF16)],
        compiler_params=_cparams(("parallel",)),
        name="moe_ln",
    )(h, y0, y1, route, ln)


def _moe(h, hb, router_w, router_b, w_in, w_out, ln):
    t, d = h.shape
    e, rows = MOE_EXPERTS, MOE_ROWS
    route = _router(h, router_w, router_b)
    flat_e = route[:, 0:2].astype(jnp.int32).reshape(-1)
    n_assign = flat_e.shape[0]
    n_blk = -(-n_assign // rows) + e
    onehot = (flat_e[:, None] == jnp.arange(e, dtype=jnp.int32)[None, :]).astype(jnp.int32)
    csum = jnp.cumsum(onehot, axis=0)
    rank = jnp.take_along_axis(csum, flat_e[:, None], axis=1)[:, 0] - 1
    sizes = csum[-1]
    padded = (sizes + rows - 1) // rows * rows
    pad_end = jnp.cumsum(padded)
    pad_start = pad_end - padded
    dest = pad_start[flat_e] + rank
    flat_tok = jnp.arange(n_assign, dtype=jnp.int32) // 2
    slot_tok = jnp.zeros((n_blk * rows,), jnp.int32).at[dest].set(flat_tok)
    blk_start = jnp.arange(n_blk, dtype=jnp.int32) * rows
    blk_e = jnp.minimum(jnp.sum((pad_end[None, :] <= blk_start[:, None]).astype(jnp.int32), axis=1), e - 1)
    n_used = (pad_end[-1] // rows).astype(jnp.int32).reshape(1)
    xb = hb[slot_tok]
    yb = _experts(xb, blk_e, n_used, w_in, w_out)
    dest2 = dest.reshape(t, 2)
    return _moe_ln(h, yb[dest2[:, 0]], yb[dest2[:, 1]], route, ln)


def kernel(x, l0_a_w_in, l0_a_lam, l0_a_subln, l0_a_w_out, l0_ln, l0_router_w, l0_router_b, l0_moe_w_in, l0_moe_w_out, l1_b_w_in, l1_b_cmp_pos, l1_b_cmp_w1, l1_b_cmp_w2, l1_b_w_out, l1_ln, l1_router_w, l1_router_b, l1_moe_w_in, l1_moe_w_out, l2_c_w_in, l2_c_w_out, l2_ln, l2_router_w, l2_router_b, l2_moe_w_in, l2_moe_w_out, l3_a_w_in, l3_a_lam, l3_a_subln, l3_a_w_out, l3_ln, l3_router_w, l3_router_b, l3_moe_w_in, l3_moe_w_out):
    batch, seq, d = x.shape
    h = x.reshape(batch * seq, d)
    hb = h.astype(BF16)
    h, hb = _mixer_a(hb, h, l0_a_w_in, l0_a_lam, l0_a_subln, l0_a_w_out, l0_ln, 0, batch, seq)
    h, hb = _moe(h, hb, l0_router_w, l0_router_b, l0_moe_w_in, l0_moe_w_out, l0_ln)
    h, hb = _mixer_b(hb, h, l1_b_w_in, l1_b_cmp_pos, l1_b_cmp_w1, l1_b_cmp_w2, l1_b_w_out, l1_ln, batch, seq)
    h, hb = _moe(h, hb, l1_router_w, l1_router_b, l1_moe_w_in, l1_moe_w_out, l1_ln)
    h, hb = _mixer_c(hb, h, l2_c_w_in, l2_c_w_out, l2_ln, batch, seq)
    h, hb = _moe(h, hb, l2_router_w, l2_router_b, l2_moe_w_in, l2_moe_w_out, l2_ln)
    h, hb = _mixer_a(hb, h, l3_a_w_in, l3_a_lam, l3_a_subln, l3_a_w_out, l3_ln, 3, batch, seq)
    h, hb = _moe(h, hb, l3_router_w, l3_router_b, l3_moe_w_in, l3_moe_w_out, l3_ln)
    return h.reshape(batch, seq, d)
```

```python
import functools
import math

import jax
import jax.numpy as jnp
import numpy as np
from jax import lax
from jax.experimental import pallas as pl
from jax.experimental.pallas import tpu as pltpu

D_MODEL = 2048
DEPTH = 4
HEAD_DIM = 128
ROPE_THETA = 500000.0
ROPE_FRACTION = 4
NEG_FILL = -1e30

A_HEADS = D_MODEL // HEAD_DIM
A_QK_DIM = HEAD_DIM // 2

B_HEADS = D_MODEL // HEAD_DIM
B_KV_GROUPS = 4
B_REP = B_HEADS // B_KV_GROUPS
B_CMP_LEN = 32
B_CMP_STRIDE = 16
B_SEL_LEN = 64
B_SEL_TOPN = 16
B_WINDOW = 512
B_N_KV = 6

C_PATTERNS = ((128, 1), (512, 4), (2048, 16))
C_HEADS_PER_GROUP = D_MODEL // (2 * HEAD_DIM)

MOE_GROUPS = 4
MOE_EXPERTS_PER_GROUP = 8
MOE_EXPERTS = MOE_GROUPS * MOE_EXPERTS_PER_GROUP
MOE_D_FF = D_MODEL // 4
MOE_ROWS = 256
MOE_CHUNKS = 4

DN_ALPHA = (2 * DEPTH) ** 0.25
NORM_EPS = 1e-5

LANES = 128
VMEM_LIMIT = 56 * 1024 * 1024

BF16 = jnp.bfloat16
F32 = jnp.float32


def _cparams(sem):
    return pltpu.CompilerParams(dimension_semantics=sem, vmem_limit_bytes=VMEM_LIMIT)


def _dot(a, b):
    return jnp.dot(a, b, preferred_element_type=F32)


def _dot_nt(a, b):
    return lax.dot_general(a, b, (((1,), (1,)), ((), ())), preferred_element_type=F32)


HALF_LANES = LANES // 2


def _rope_layout(dim):
    n_sub = LANES // dim
    w = HALF_LANES // n_sub
    half = dim // ROPE_FRACTION // 2
    perm, kind, freq = np.zeros(LANES, np.int32), np.zeros(LANES, np.int32), np.zeros(LANES, np.int32)
    for s in range(n_sub):
        lanes = list(range(s * w, (s + 1) * w)) + list(range(HALF_LANES + s * w, HALF_LANES + (s + 1) * w))
        x1, x2 = lanes[:half], lanes[w:w + half]
        rest = [l for l in lanes if l not in x1 and l not in x2]
        for i, l in enumerate(x1):
            perm[l], kind[l], freq[l] = s * dim + i, 1, i
        for i, l in enumerate(x2):
            perm[l], kind[l], freq[l] = s * dim + half + i, 2, i
        for i, l in enumerate(rest):
            perm[l] = s * dim + 2 * half + i
    return perm, kind, freq


def _rope_tables(pos, dim):
    rot = dim // ROPE_FRACTION
    _, kind, freq = _rope_layout(dim)
    inv_freq = ROPE_THETA ** (-jnp.arange(0, rot, 2, dtype=F32) / rot)
    ang = pos.astype(F32)[:, None] * inv_freq[None, :]
    cos, sin = jnp.cos(ang)[:, freq], jnp.sin(ang)[:, freq]
    c = jnp.where(kind[None, :] > 0, cos, 1.0)
    ss = jnp.where(kind[None, :] == 1, -sin, jnp.where(kind[None, :] == 2, sin, 0.0))
    return c, ss


def _permute_blocks(w, blocks, dim):
    perm, _, _ = _rope_layout(dim)
    idx = np.arange(w.shape[1], dtype=np.int32)
    for b in blocks:
        idx[b * LANES:(b + 1) * LANES] = b * LANES + perm
    return w[:, idx]


def _apply_rope(a, c, ss):
    return a * c + pltpu.roll(a, HALF_LANES, 1) * ss


def _proj_kernel(flags_ref, x_ref, w_ref, c_ref, ss_ref, o_ref):
    acc = _dot(x_ref[...], w_ref[...])
    c, ss = c_ref[...], ss_ref[...]
    for s in range(acc.shape[1] // LANES):
        a = acc[:, s * LANES:(s + 1) * LANES]
        o_ref[:, s * LANES:(s + 1) * LANES] = _apply_rope(a, c, ss).astype(o_ref.dtype)


def _proj(xb, w, flags, tables, seq, tm=1024, tn=512):
    m, k = xb.shape
    n = w.shape[1]
    assert m % tm == 0 and n % tn == 0 and seq % tm == 0
    pos_blocks = seq // tm
    identity = (jnp.ones_like(tables[0]), jnp.zeros_like(tables[1]))
    tables = tuple(jnp.stack([ident, tab]) for ident, tab in zip(identity, tables))
    tab_spec = pl.BlockSpec((None, tm, LANES), lambda i, j, f: (f[j], i % pos_blocks, 0))
    return pl.pallas_call(
        _proj_kernel,
        grid_spec=pltpu.PrefetchScalarGridSpec(
            num_scalar_prefetch=1,
            grid=(m // tm, n // tn),
            in_specs=[pl.BlockSpec((tm, k), lambda i, j, f: (i, 0)),
                      pl.BlockSpec((k, tn), lambda i, j, f: (0, j)),
                      tab_spec, tab_spec],
            out_specs=pl.BlockSpec((tm, tn), lambda i, j, f: (i, j))),
        out_shape=jax.ShapeDtypeStruct((m, n), BF16),
        compiler_params=_cparams(("parallel", "arbitrary")),
        name="proj",
    )(flags, xb, w, *tables)


def _layer_norm(z, g, b):
    mu = jnp.mean(z, axis=-1, keepdims=True)
    zc = z - mu
    var = jnp.mean(zc * zc, axis=-1, keepdims=True)
    return zc * lax.rsqrt(var + NORM_EPS) * g + b


def _outproj_ln_kernel(o_ref, w_ref, h_ref, ln_ref, hf_ref, hb_ref):
    m = _dot(o_ref[...], w_ref[...])
    y = _layer_norm(DN_ALPHA * h_ref[...] + m, ln_ref[0:1, :], ln_ref[1:2, :])
    hf_ref[...] = y
    hb_ref[...] = y.astype(BF16)


def _outproj_ln(o, w, h, ln, tm=256):
    m, k = o.shape
    d = w.shape[1]
    return pl.pallas_call(
        _outproj_ln_kernel,
        grid=(m // tm,),
        in_specs=[pl.BlockSpec((tm, k), lambda i: (i, 0)),
                  pl.BlockSpec((k, d), lambda i: (0, 0)),
                  pl.BlockSpec((tm, d), lambda i: (i, 0)),
                  pl.BlockSpec((4, d), lambda i: (0, 0))],
        out_specs=[pl.BlockSpec((tm, d), lambda i: (i, 0)),
                   pl.BlockSpec((tm, d), lambda i: (i, 0))],
        out_shape=[jax.ShapeDtypeStruct((m, d), F32), jax.ShapeDtypeStruct((m, d), BF16)],
        compiler_params=_cparams(("parallel",)),
        name="outproj_ln",
    )(o, w, h, ln)


def _causal_block(q, k, v, c, tq):
    s = _dot_nt(q, k)
    rows = q.shape[0]
    assert tq & (tq - 1) == 0
    pos_in_block = jnp.bitwise_and(lax.broadcasted_iota(jnp.int32, (rows, tq), 0), tq - 1)
    causal = lax.broadcasted_iota(jnp.int32, (rows, tq), 1) <= pos_in_block
    s_diag = jnp.where(causal, s[:, c * tq:], -jnp.inf)
    m = jnp.max(s_diag, axis=-1, keepdims=True)
    if c > 0:
        s_off = s[:, :c * tq]
        m = jnp.maximum(m, jnp.max(s_off, axis=-1, keepdims=True))
        e = jnp.concatenate([jnp.exp(s_off - m), jnp.exp(s_diag - m)], axis=1)
    else:
        e = jnp.exp(s_diag - m)
    inv = 1.0 / jnp.sum(e, axis=-1, keepdims=True)
    return _dot(e.astype(BF16), v) * inv


def _attn_a_kernel(lam_ref, q_ref, k_ref, v_ref, g_ref, o_ref, *, tq, seq, out_scale):
    for c in range(seq // tq):
        kv = (c + 1) * tq
        q = q_ref[0, c * tq:kv, :] * jnp.asarray(A_QK_DIM ** -0.5, BF16)
        lane = lax.broadcasted_iota(jnp.int32, q.shape, 1)
        zero = jnp.zeros_like(q)
        first = jnp.bitwise_and(lane, HALF_LANES // (LANES // A_QK_DIM)) == 0
        q12 = jnp.concatenate([jnp.where(first, q, zero), jnp.where(first, zero, q)], axis=0)
        both = _causal_block(q12, k_ref[0, :kv, :], v_ref[0, :kv, :], c, tq)
        o = both[:tq] - lam_ref[0] * both[tq:]
        o = o * lax.rsqrt(jnp.mean(o * o, axis=-1, keepdims=True) + NORM_EPS) * (g_ref[...] * out_scale)
        o_ref[0, c * tq:kv, :] = o.astype(o_ref.dtype)


def _attn_a(proj, lam_full, subln, batch, seq, lam_init, tq=256):
    h = A_HEADS
    return pl.pallas_call(
        functools.partial(_attn_a_kernel, tq=tq, seq=seq, out_scale=1.0 - lam_init),
        grid_spec=pltpu.PrefetchScalarGridSpec(
            num_scalar_prefetch=1,
            grid=(batch, h),
            in_specs=[pl.BlockSpec((1, seq, HEAD_DIM), lambda b, hh, s: (b, 0, hh)),
                      pl.BlockSpec((1, seq, HEAD_DIM), lambda b, hh, s: (b, 0, h + hh)),
                      pl.BlockSpec((1, seq, HEAD_DIM), lambda b, hh, s: (b, 0, 2 * h + hh)),
                      pl.BlockSpec((1, HEAD_DIM), lambda b, hh, s: (0, 0))],
            out_specs=pl.BlockSpec((1, seq, HEAD_DIM), lambda b, hh, s: (b, 0, hh))),
        out_shape=jax.ShapeDtypeStruct((batch, seq, h * HEAD_DIM), BF16),
        compiler_params=_cparams(("parallel", "parallel")),
        name="attn_a",
    )(lam_full, proj, proj, proj, subln)


def _mixer_a(hb, h, w_in, lam, subln, w_out, ln, layer_idx, batch, seq):
    pos = jnp.arange(seq)
    tables = _rope_tables(pos, A_QK_DIM)
    n_q = A_HEADS * HEAD_DIM
    tn = 512
    flags = (jnp.arange(3 * n_q // tn) < 2 * n_q // tn).astype(jnp.int32)
    wb = _permute_blocks(w_in, range(2 * A_HEADS), A_QK_DIM).astype(BF16)
    proj = _proj(hb, wb, flags, tables, seq, tn=tn)
    lam_init = 0.8 - 0.6 * math.exp(-0.3 * layer_idx)
    lf = lam.astype(F32)
    lam_full = jnp.exp(jnp.sum(lf[0] * lf[1])) - jnp.exp(jnp.sum(lf[2] * lf[3])) + lam_init
    o = _attn_a(proj.reshape(batch, seq, -1), lam_full.reshape(1), subln.reshape(1, HEAD_DIM), batch, seq, lam_init)
    return _outproj_ln(o.reshape(batch * seq, -1), w_out.astype(BF16), h, ln)


DIL_BLOCK = 128


def _band_block(q, k, v, q0, k0, span):
    s = _dot_nt(q, k)
    diff = (q0 + lax.broadcasted_iota(jnp.int32, s.shape, 0)) - (k0 + lax.broadcasted_iota(jnp.int32, s.shape, 1))
    s = jnp.where((diff >= 0) & (diff <= span), s, -jnp.inf)
    m = jnp.max(s, axis=-1, keepdims=True)
    e = jnp.exp(s - m)
    den = jnp.sum(e, axis=-1, keepdims=True)
    return _dot((e * (1.0 / den)).astype(BF16), v), m + jnp.log(den)


def _dil_kernel(*refs, seq):
    n_grp = len(C_PATTERNS)
    qkv = refs[:3 * n_grp]
    o_ref = refs[3 * n_grp]
    stage, o_scr, l_scr = refs[3 * n_grp + 1:]
    blk = DIL_BLOCK
    for g, (window, dil) in enumerate(C_PATTERNS):
        q_ref, k_ref, v_ref = qkv[3 * g:3 * g + 3]
        length = seq // dil
        span = window // dil
        n_blk = length // blk
        if dil > 1:
            for j, ref in enumerate((q_ref, k_ref, v_ref)):
                stage[j] = ref[0].astype(F32)

        def rows(j, ref, r, start, size):
            if dil == 1:
                return ref[0, start:start + size, :]
            return stage[j, pl.ds(r + start * dil, size, stride=dil), :].astype(BF16)

        def residue(r, carry):
            for i in range(n_blk):
                k0 = max(i - 1, 0) * blk
                kl = (i + 1) * blk - k0
                o, lse = _band_block(rows(0, q_ref, r, i * blk, blk), rows(1, k_ref, r, k0, kl),
                                     rows(2, v_ref, r, k0, kl), i * blk, k0, span)
                lse = jnp.broadcast_to(lse, (blk, HEAD_DIM))
                if dil == 1:
                    o_scr[g, i * blk:(i + 1) * blk, :] = o
                    l_scr[g, i * blk:(i + 1) * blk, :] = lse
                else:
                    o_scr[g, pl.ds(r + i * blk * dil, blk, stride=dil), :] = o
                    l_scr[g, pl.ds(r + i * blk * dil, blk, stride=dil), :] = lse
            return carry

        if dil == 1:
            residue(0, 0)
        else:
            lax.fori_loop(0, dil, residue, 0, unroll=2 if n_blk == 1 else 1)

    chunk = 256
    for c in range(seq // chunk):
        sl = slice(c * chunk, (c + 1) * chunk)
        ls = [l_scr[g, sl, :] for g in range(n_grp)]
        m = functools.reduce(jnp.maximum, ls)
        es = [jnp.exp(l - m) for l in ls]
        inv = 1.0 / functools.reduce(lambda a, b: a + b, es)
        o = functools.reduce(lambda a, b: a + b, [(es[g] * inv) * o_scr[g, sl, :] for g in range(n_grp)])
        o_ref[0, sl, :] = o.astype(o_ref.dtype)


def _dil_attn(proj, batch, seq):
    hg, p = C_HEADS_PER_GROUP, len(C_PATTERNS)

    def spec(kind, g):
        return pl.BlockSpec((1, seq, HEAD_DIM), lambda b, hh: (b, 0, kind * p * hg + g * hg + hh))

    in_specs = [spec(kind, g) for g in range(p) for kind in range(3)]
    return pl.pallas_call(
        functools.partial(_dil_kernel, seq=seq),
        grid=(batch, hg),
        in_specs=in_specs,
        out_specs=pl.BlockSpec((1, seq, HEAD_DIM), lambda b, hh: (b, 0, hh)),
        out_shape=jax.ShapeDtypeStruct((batch, seq, hg * HEAD_DIM), BF16),
        scratch_shapes=[pltpu.VMEM((3, seq, HEAD_DIM), F32), pltpu.VMEM((p, seq, HEAD_DIM), F32),
                        pltpu.VMEM((p, seq, HEAD_DIM), F32)],
        compiler_params=_cparams(("parallel", "parallel")),
        name="dil_attn",
    )(*([proj] * len(in_specs)))


def _mixer_c(hb, h, w_in, w_out, ln, batch, seq):
    tables = _rope_tables(jnp.arange(seq), HEAD_DIM)
    tn = 512
    n = w_in.shape[1]
    flags = (jnp.arange(n // tn) < 2 * (n // 3) // tn).astype(jnp.int32)
    wb = _scaled_bf16(_permute_blocks(w_in, range(2 * n // 3 // LANES), HEAD_DIM), n // 3, HEAD_DIM ** -0.5)
    proj = _proj(hb, wb, flags, tables, seq, tn=tn)
    o = _dil_attn(proj.reshape(batch, seq, n), batch, seq)
    return _outproj_ln(o.reshape(batch * seq, -1), w_out.astype(BF16), h, ln)


def _gates_kernel(x_ref, w_ref, o_ref):
    z = _dot(x_ref[...], w_ref[...])
    o_ref[...] = 1.0 / (1.0 + jnp.exp(-z))


def _gates(xb, wg, tm=1024):
    m, k = xb.shape
    n = wg.shape[1]
    return pl.pallas_call(
        _gates_kernel,
        grid=(m // tm,),
        in_specs=[pl.BlockSpec((tm, k), lambda i: (i, 0)), pl.BlockSpec((k, n), lambda i: (0, 0))],
        out_specs=pl.BlockSpec((tm, n), lambda i: (i, 0)),
        out_shape=jax.ShapeDtypeStruct((m, n), F32),
        compiler_params=_cparams(("parallel",)),
        name="nsa_gates",
    )(xb, wg)


def _gelu_tanh(x):
    return 0.5 * x * (1.0 + jnp.tanh(math.sqrt(2.0 / math.pi) * (x + 0.044715 * (x * x * x))))


def _compress_kernel(a_ref, pos_ref, w1_ref, w2_ref, c_ref, ss_ref, o_ref, stage):
    kind = pl.program_id(2)
    stage[...] = a_ref[0].astype(F32)
    n_chunk = stage.shape[0] // B_CMP_STRIDE
    first = jnp.zeros((n_chunk, HEAD_DIM), F32)
    second = jnp.zeros((n_chunk, HEAD_DIM), F32)
    for t in range(B_CMP_STRIDE):
        a = stage[pl.ds(t, n_chunk, stride=B_CMP_STRIDE), :]
        u = B_CMP_STRIDE + t
        first = first + _dot((a + pos_ref[0, t:t + 1, :]).astype(BF16), w1_ref[0, t * HEAD_DIM:(t + 1) * HEAD_DIM, :])
        second = second + _dot((a + pos_ref[0, u:u + 1, :]).astype(BF16), w1_ref[0, u * HEAD_DIM:(u + 1) * HEAD_DIM, :])
    hmid = first + pltpu.roll(second, n_chunk - 1, 0)
    out = _dot(_gelu_tanh(hmid).astype(BF16), w2_ref[0])

    @pl.when(kind == 0)
    def _():
        o_ref[0, 0] = _apply_rope(out, c_ref[...], ss_ref[...]).astype(o_ref.dtype)

    @pl.when(kind != 0)
    def _():
        o_ref[0, 0] = out.astype(o_ref.dtype)


def _compress(proj3, cmp_pos, cmp_w1, cmp_w2, batch, seq):
    g = B_KV_GROUPS
    n_chunk = seq // B_CMP_STRIDE
    cmp_end = jnp.arange(n_chunk) * B_CMP_STRIDE + B_CMP_LEN - 1
    tables = _rope_tables(cmp_end, HEAD_DIM)
    tab_spec = pl.BlockSpec((n_chunk, LANES), lambda b, gg, kk: (0, 0))
    perm, _, _ = _rope_layout(HEAD_DIM)
    cmp_w2 = jnp.stack([cmp_w2[0][:, perm], cmp_w2[1]])
    return pl.pallas_call(
        _compress_kernel,
        grid=(batch, g, 2),
        in_specs=[pl.BlockSpec((1, seq, HEAD_DIM), lambda b, gg, kk: (b, 0, B_HEADS + kk * g + gg)),
                  pl.BlockSpec((1, B_CMP_LEN, HEAD_DIM), lambda b, gg, kk: (kk, 0, 0)),
                  pl.BlockSpec((1, B_CMP_LEN * HEAD_DIM, HEAD_DIM), lambda b, gg, kk: (kk, 0, 0)),
                  pl.BlockSpec((1, HEAD_DIM, HEAD_DIM), lambda b, gg, kk: (kk, 0, 0)),
                  tab_spec, tab_spec],
        out_specs=pl.BlockSpec((1, 1, n_chunk, HEAD_DIM), lambda b, gg, kk: (kk * batch + b, gg, 0, 0)),
        out_shape=jax.ShapeDtypeStruct((2 * batch, g, n_chunk, HEAD_DIM), BF16),
        scratch_shapes=[pltpu.VMEM((seq, HEAD_DIM), F32)],
        compiler_params=_cparams(("parallel", "parallel", "arbitrary")),
        name="nsa_compress",
    )(proj3, cmp_pos, cmp_w1.astype(BF16), cmp_w2.astype(BF16), *tables)


def _split3(x):
    hi = x.astype(BF16)
    r = x - hi.astype(F32)
    mid = r.astype(BF16)
    lo = (r - mid.astype(F32)).astype(BF16)
    return hi, mid, lo


def _nsa_kernel(q_ref, kc_ref, vc_ref, ks_ref, vs_ref, kw_ref, vw_ref, gate_ref, ex_ref, o_ref,
                vct_scr, ocmp_scr, *, tq, seq, n_cmp, n_sel):
    rep = B_REP
    big = 1e30
    vct_scr[...] = vc_ref[0, 0].astype(F32).T.astype(BF16)
    kc = kc_ref[0, 0]
    lane = lax.broadcasted_iota(jnp.int32, (tq, LANES), 1)
    n_back = -(-(B_WINDOW - 1) // tq)

    for c in range(seq // tq):
        lo, hi = c * tq, (c + 1) * tq

        def q_head(r):
            return q_ref[0, lo:hi, pl.ds(pl.multiple_of(r * HEAD_DIM, HEAD_DIM), HEAD_DIM)]

        qpos = lo + lax.broadcasted_iota(jnp.int32, (1, tq), 1)
        crow = lax.broadcasted_iota(jnp.int32, (LANES, tq), 0)
        cmask = (crow * B_CMP_STRIDE + (B_CMP_LEN - 1) <= qpos) & (crow < n_cmp)

        p_sum = jnp.zeros((LANES, tq), F32)
        for r in range(rep):
            s = jnp.where(cmask, _dot_nt(kc, q_ref[0, lo:hi, r * HEAD_DIM:(r + 1) * HEAD_DIM]), NEG_FILL)
            e = jnp.exp(s - jnp.max(s, axis=0, keepdims=True))
            p = jnp.where(cmask, e * (1.0 / jnp.sum(e, axis=0, keepdims=True)), 0.0)
            ocmp_scr[r] = _dot(vct_scr[...], p.astype(BF16)).T
            p_sum = p_sum + p

        ni = lax.broadcasted_iota(jnp.int32, (LANES, LANES), 0)
        ci = lax.broadcasted_iota(jnp.int32, (LANES, LANES), 1)
        per = B_SEL_LEN // B_CMP_STRIDE
        cover = ((ci >= per * ni - (B_CMP_LEN // B_CMP_STRIDE - 1)) & (ci < per * ni + per) & (ci < n_cmp) & (ni < n_sel))
        cover = jnp.where(cover, 1.0, 0.0).astype(BF16)
        importance = sum(_dot(cover, part) for part in _split3(p_sum))[:n_sel]
        nrow = lax.broadcasted_iota(jnp.int32, (n_sel, tq), 0)
        cur = jnp.right_shift(qpos, int(math.log2(B_SEL_LEN)))
        forced = (nrow == 0) | (nrow == cur) | (nrow == cur - 1)
        score = jnp.where(forced, big, importance)
        score = jnp.where(nrow <= cur, score, -big)
        rank = jnp.zeros((n_sel, tq), jnp.int32)
        for mm in range(min(n_sel, (hi - 1) // B_SEL_LEN + 1)):
            other = score[mm:mm + 1, :]
            ahead = (other > score) | ((other == score) & (nrow > mm))
            rank = rank + jnp.where(ahead, 1, 0)
        sel_bias = jnp.where(rank < min(B_SEL_TOPN, n_sel), 0.0, -big)
        sel_bias = jnp.concatenate([sel_bias, jnp.zeros((LANES - n_sel, tq), F32)], axis=0).T.astype(BF16)

        k_aug = jnp.concatenate([ks_ref[0, :hi, :], ex_ref[:hi, :]], axis=1)
        v_sel = vs_ref[0, :hi, :]
        wlo = max(c - n_back, 0) * tq
        k_win, v_win = kw_ref[0, wlo:hi, :], vw_ref[0, wlo:hi, :]
        diff = (lo + lax.broadcasted_iota(jnp.int32, (tq, hi - wlo), 0)) - (wlo + lax.broadcasted_iota(jnp.int32, (tq, hi - wlo), 1))
        wmask = (diff >= 0) & (diff <= B_WINDOW - 1)
        gates = gate_ref[0, lo:hi, :]

        def head(r, carry):
            q = q_head(r)
            o_sel = _causal_block(jnp.concatenate([q, sel_bias], axis=1), k_aug, v_sel, c, tq)
            s = jnp.where(wmask, _dot_nt(q, k_win), -jnp.inf)
            e = jnp.exp(s - jnp.max(s, axis=-1, keepdims=True))
            o_win = _dot(e.astype(BF16), v_win) * (1.0 / jnp.sum(e, axis=-1, keepdims=True))

            def gate(j):
                return jnp.sum(jnp.where(lane == r * 3 + j, gates, 0.0), axis=-1, keepdims=True)

            o = gate(0) * ocmp_scr[r] + gate(1) * o_sel + gate(2) * o_win
            o_ref[0, lo:hi, pl.ds(pl.multiple_of(r * HEAD_DIM, HEAD_DIM), HEAD_DIM)] = o.astype(o_ref.dtype)
            return carry

        lax.fori_loop(0, rep, head, 0)


def _nsa_attn(proj3, kvc, gates, batch, seq, tq=256):
    g, rep = B_KV_GROUPS, B_REP
    n_chunk = seq // B_CMP_STRIDE
    n_cmp = n_chunk - B_CMP_LEN // B_CMP_STRIDE + 1
    n_sel = seq // B_SEL_LEN
    assert n_chunk == LANES and n_sel <= LANES and n_sel % 8 == 0
    onehot = (jnp.arange(seq)[:, None] // B_SEL_LEN == jnp.arange(LANES)[None, :]).astype(BF16)

    def kv_spec(j):
        return pl.BlockSpec((1, seq, HEAD_DIM), lambda b, gg: (b, 0, B_HEADS + j * g + gg))

    return pl.pallas_call(
        functools.partial(_nsa_kernel, tq=tq, seq=seq, n_cmp=n_cmp, n_sel=n_sel),
        grid=(batch, g),
        in_specs=[pl.BlockSpec((1, seq, rep * HEAD_DIM), lambda b, gg: (b, 0, gg)),
                  pl.BlockSpec((1, 1, n_chunk, HEAD_DIM), lambda b, gg: (b, gg, 0, 0)),
                  pl.BlockSpec((1, 1, n_chunk, HEAD_DIM), lambda b, gg: (batch + b, gg, 0, 0)),
                  kv_spec(2), kv_spec(3), kv_spec(4), kv_spec(5),
                  pl.BlockSpec((1, seq, LANES), lambda b, gg: (b, 0, gg)),
                  pl.BlockSpec((seq, LANES), lambda b, gg: (0, 0))],
        out_specs=pl.BlockSpec((1, seq, rep * HEAD_DIM), lambda b, gg: (b, 0, gg)),
        out_shape=jax.ShapeDtypeStruct((batch, seq, B_HEADS * HEAD_DIM), BF16),
        scratch_shapes=[pltpu.VMEM((HEAD_DIM, n_chunk), BF16), pltpu.VMEM((rep, tq, HEAD_DIM), F32)],
        compiler_params=_cparams(("parallel", "parallel")),
        name="nsa_attn",
    )(proj3, kvc, kvc, proj3, proj3, proj3, proj3, gates.reshape(batch, seq, g * LANES), onehot)


def _scaled_bf16(w, n_cols, scale):
    col_scale = jnp.where(jnp.arange(w.shape[1]) < n_cols, scale, 1.0).astype(F32)
    return (w * col_scale[None, :]).astype(BF16)


def _mixer_b(hb, h, w_in, cmp_pos, cmp_w1, cmp_w2, w_out, ln, batch, seq):
    g, rep = B_KV_GROUPS, B_REP
    n_main = (B_HEADS + B_N_KV * g) * HEAD_DIM
    tn = 512
    per = g * HEAD_DIM // tn
    nq = B_HEADS * HEAD_DIM // tn
    jb = jnp.arange(n_main // tn)
    flags = ((jb < nq) | ((jb >= nq + 2 * per) & (jb < nq + 3 * per))
             | ((jb >= nq + 4 * per) & (jb < nq + 5 * per))).astype(jnp.int32)
    tables = _rope_tables(jnp.arange(seq), HEAD_DIM)
    roped = (list(range(B_HEADS)) + list(range(B_HEADS + 2 * g, B_HEADS + 3 * g))
             + list(range(B_HEADS + 4 * g, B_HEADS + 5 * g)))
    wb = _scaled_bf16(_permute_blocks(w_in, roped, HEAD_DIM), B_HEADS * HEAD_DIM, HEAD_DIM ** -0.5)
    proj3 = _proj(hb, wb[:, :n_main], flags, tables, seq, tn=tn).reshape(batch, seq, n_main)
    wg = wb[:, n_main:].reshape(-1, g, rep * 3)
    wg = jnp.pad(wg, ((0, 0), (0, 0), (0, LANES - rep * 3))).reshape(-1, g * LANES)
    gates = _gates(hb, wg)
    kvc = _compress(proj3, cmp_pos, cmp_w1, cmp_w2, batch, seq)
    o = _nsa_attn(proj3, kvc, gates, batch, seq)
    return _outproj_ln(o.reshape(batch * seq, -1), w_out.astype(BF16), h, ln)


def _router_kernel(x_ref, w_ref, b_ref, o_ref):
    x = x_ref[...]
    xh = x.astype(BF16)
    xl = (x - xh.astype(F32)).astype(BF16)
    hi = _dot(xh, w_ref[...])
    logits = hi[:, :LANES] + (hi[:, LANES:] + _dot(xl, w_ref[:, :LANES])) + b_ref[...]
    lane = lax.broadcasted_iota(jnp.int32, logits.shape, 1)
    lane_f = lane.astype(F32)

    def first_max(vals):
        top = jnp.max(vals, axis=-1, keepdims=True)
        first = jnp.min(jnp.where(vals == top, lane_f, float(4 * LANES)), axis=-1, keepdims=True)
        return top, first.astype(jnp.int32)

    gl = jnp.where(lane < MOE_GROUPS, logits, -jnp.inf)
    gmax, gidx = first_max(gl)
    g_w = 1.0 / jnp.sum(jnp.exp(gl - gmax), axis=-1, keepdims=True)
    lo = MOE_GROUPS + MOE_EXPERTS_PER_GROUP * gidx
    ev = jnp.where((lane >= lo) & (lane < lo + MOE_EXPERTS_PER_GROUP), logits, -jnp.inf)
    v1, i1 = first_max(ev)
    v2, i2 = first_max(jnp.where(lane == i1, -jnp.inf, ev))
    e2 = jnp.exp(v2 - v1)
    w1 = g_w / (1.0 + e2)
    w2 = g_w * e2 / (1.0 + e2)
    out = jnp.where(lane == 0, (i1 - MOE_GROUPS).astype(F32),
                    jnp.where(lane == 1, (i2 - MOE_GROUPS).astype(F32),
                              jnp.where(lane == 2, w1, jnp.where(lane == 3, w2, 0.0))))
    o_ref[...] = out


def _router(h, router_w, router_b, tm=1024):
    m, d = h.shape
    n = router_w.shape[1]
    wp = jnp.pad(router_w, ((0, 0), (0, LANES - n)))
    wh = wp.astype(BF16)
    wp = jnp.concatenate([wh, (wp - wh.astype(F32)).astype(BF16)], axis=1)
    bp = jnp.pad(router_b, (0, LANES - n)).reshape(1, LANES)
    return pl.pallas_call(
        _router_kernel,
        grid=(m // tm,),
        in_specs=[pl.BlockSpec((tm, d), lambda i: (i, 0)),
                  pl.BlockSpec((d, 2 * LANES), lambda i: (0, 0)),
                  pl.BlockSpec((1, LANES), lambda i: (0, 0))],
        out_specs=pl.BlockSpec((tm, LANES), lambda i: (i, 0)),
        out_shape=jax.ShapeDtypeStruct((m, LANES), F32),
        compiler_params=_cparams(("parallel",)),
        name="moe_router",
    )(h, wp, bp)


def _experts_kernel(be_ref, nu_ref, x_ref, wi_ref, wo_ref, *rest, blk0):
    y_ref, wi_scr, wo_scr = rest[-3:]
    i = pl.program_id(0)
    blk = i + blk0

    @pl.when((i == 0) | (be_ref[blk] != be_ref[jnp.maximum(blk - 1, 0)]))
    def _():
        wi_scr[...] = wi_ref[0].astype(BF16)
        wo_scr[...] = wo_ref[0].astype(BF16)

    @pl.when(blk < nu_ref[0])
    def _():
        gu = _dot(x_ref[...], wi_scr[...])
        gate, up = gu[:, :MOE_D_FF], gu[:, MOE_D_FF:]
        act = gate * (1.0 / (1.0 + jnp.exp(-gate))) * up
        y_ref[...] = _dot(act.astype(BF16), wo_scr[...]).astype(y_ref.dtype)

    @pl.when(blk >= nu_ref[0])
    def _():
        y_ref[...] = jnp.zeros_like(y_ref)


def _experts(xb, blk_e, n_used, w_in, w_out, ybuf, blk0):
    rows, d = xb.shape
    n_total = blk_e.shape[0]
    in_specs = [pl.BlockSpec((MOE_ROWS, d), lambda i, be, nu: (i, 0)),
                pl.BlockSpec((1, d, 2 * MOE_D_FF), lambda i, be, nu: (be[i + blk0], 0, 0)),
                pl.BlockSpec((1, MOE_D_FF, d), lambda i, be, nu: (be[i + blk0], 0, 0))]
    args = [blk_e, n_used, xb, w_in, w_out]
    aliases = {}
    if ybuf is not None:
        in_specs.append(pl.BlockSpec(memory_space=pl.ANY))
        args.append(ybuf)
        aliases = {len(args) - 1: 0}
    return pl.pallas_call(
        functools.partial(_experts_kernel, blk0=blk0),
        grid_spec=pltpu.PrefetchScalarGridSpec(
            num_scalar_prefetch=2,
            grid=(rows // MOE_ROWS,),
            in_specs=in_specs,
            out_specs=pl.BlockSpec((MOE_ROWS, d), lambda i, be, nu: (i + blk0, 0)),
            scratch_shapes=[pltpu.VMEM((d, 2 * MOE_D_FF), BF16), pltpu.VMEM((MOE_D_FF, d), BF16)]),
        out_shape=jax.ShapeDtypeStruct((n_total * MOE_ROWS, d), BF16),
        input_output_aliases=aliases,
        compiler_params=_cparams(("arbitrary",)),
        name="moe_experts",
    )(*args)


def _moe_ln_kernel(h_ref, y0_ref, y1_ref, r_ref, ln_ref, hf_ref, hb_ref):
    r = r_ref[...]
    f = y0_ref[...].astype(F32) * r[:, 2:3] + y1_ref[...].astype(F32) * r[:, 3:4]
    y = _layer_norm(DN_ALPHA * h_ref[...] + f, ln_ref[2:3, :], ln_ref[3:4, :])
    hf_ref[...] = y
    hb_ref[...] = y.astype(BF16)


def _moe_ln(h, y0, y1, route, ln, tm=512):
    m, d = h.shape
    row = lambda i: (i, 0)
    return pl.pallas_call(
        _moe_ln_kernel,
        grid=(m // tm,),
        in_specs=[pl.BlockSpec((tm, d), row), pl.BlockSpec((tm, d), row), pl.BlockSpec((tm, d), row),
                  pl.BlockSpec((tm, LANES), row), pl.BlockSpec((4, d), lambda i: (0, 0))],
        out_specs=[pl.BlockSpec((tm, d), row), pl.BlockSpec((tm, d), row)],
        out_shape=[jax.ShapeDtypeStruct((m, d), F32), jax.ShapeDtypeStruct((m, d), BF16)],
        compiler_params=_cparams(("parallel",)),
        name="moe_ln",
    )(h, y0, y1, route, ln)


def _moe(h, hb, router_w, router_b, w_in, w_out, ln):
    t, d = h.shape
    e, rows = MOE_EXPERTS, MOE_ROWS
    route = _router(h, router_w, router_b)
    flat_e = route[:, 0:2].astype(jnp.int32).reshape(-1)
    n_assign = flat_e.shape[0]
    n_blk = -(-n_assign // rows) + e
    onehot = (flat_e[:, None] == jnp.arange(e, dtype=jnp.int32)[None, :]).astype(jnp.int32)
    csum = jnp.cumsum(onehot, axis=0)
    rank = jnp.take_along_axis(csum, flat_e[:, None], axis=1)[:, 0] - 1
    sizes = csum[-1]
    padded = (sizes + rows - 1) // rows * rows
    pad_end = jnp.cumsum(padded)
    pad_start = pad_end - padded
    dest = pad_start[flat_e] + rank
    flat_tok = jnp.arange(n_assign, dtype=jnp.int32) // 2
    slot_tok = jnp.zeros((n_blk * rows,), jnp.int32).at[dest].set(flat_tok)
    blk_start = jnp.arange(n_blk, dtype=jnp.int32) * rows
    blk_e = jnp.minimum(jnp.sum((pad_end[None, :] <= blk_start[:, None]).astype(jnp.int32), axis=1), e - 1)
    n_used = (pad_end[-1] // rows).astype(jnp.int32).reshape(1)
    assert n_blk % MOE_CHUNKS == 0
    per = n_blk // MOE_CHUNKS
    yb = None
    for c in range(MOE_CHUNKS):
        xb = hb[slot_tok[c * per * rows:(c + 1) * per * rows]]
        yb = _experts(xb, blk_e, n_used, w_in, w_out, yb, c * per)
    dest2 = dest.reshape(t, 2)
    return _moe_ln(h, yb[dest2[:, 0]], yb[dest2[:, 1]], route, ln)


def kernel(x, l0_a_w_in, l0_a_lam, l0_a_subln, l0_a_w_out, l0_ln, l0_router_w, l0_router_b, l0_moe_w_in, l0_moe_w_out, l1_b_w_in, l1_b_cmp_pos, l1_b_cmp_w1, l1_b_cmp_w2, l1_b_w_out, l1_ln, l1_router_w, l1_router_b, l1_moe_w_in, l1_moe_w_out, l2_c_w_in, l2_c_w_out, l2_ln, l2_router_w, l2_router_b, l2_moe_w_in, l2_moe_w_out, l3_a_w_in, l3_a_lam, l3_a_subln, l3_a_w_out, l3_ln, l3_router_w, l3_router_b, l3_moe_w_in, l3_moe_w_out):
    batch, seq, d = x.shape
    h = x.reshape(batch * seq, d)
    hb = h.astype(BF16)
    h, hb = _mixer_a(hb, h, l0_a_w_in, l0_a_lam, l0_a_subln, l0_a_w_out, l0_ln, 0, batch, seq)
    h, hb = _moe(h, hb, l0_router_w, l0_router_b, l0_moe_w_in, l0_moe_w_out, l0_ln)
    h, hb = _mixer_b(hb, h, l1_b_w_in, l1_b_cmp_pos, l1_b_cmp_w1, l1_b_cmp_w2, l1_b_w_out, l1_ln, batch, seq)
    h, hb = _moe(h, hb, l1_router_w, l1_router_b, l1_moe_w_in, l1_moe_w_out, l1_ln)
    h, hb = _mixer_c(hb, h, l2_c_w_in, l2_c_w_out, l2_ln, batch, seq)
    h, hb = _moe(h, hb, l2_router_w, l2_router_b, l2_moe_w_in, l2_moe_w_out, l2_ln)
    h, hb = _mixer_a(hb, h, l3_a_w_in, l3_a_lam, l3_a_subln, l3_a_w_out, l3_ln, 3, batch, seq)
    h, hb = _moe(h, hb, l3_router_w, l3_router_b, l3_moe_w_in, l3_moe_w_out, l3_ln)
    return h.reshape(batch, seq, d)
```

```python
import functools
import math

import jax
import jax.numpy as jnp
import numpy as np
from jax import lax
from jax.experimental import pallas as pl
from jax.experimental.pallas import tpu as pltpu

D_MODEL = 2048
DEPTH = 4
HEAD_DIM = 128
ROPE_THETA = 500000.0
ROPE_FRACTION = 4
NEG_FILL = -1e30

A_HEADS = D_MODEL // HEAD_DIM
A_QK_DIM = HEAD_DIM // 2

B_HEADS = D_MODEL // HEAD_DIM
B_KV_GROUPS = 4
B_REP = B_HEADS // B_KV_GROUPS
B_CMP_LEN = 32
B_CMP_STRIDE = 16
B_SEL_LEN = 64
B_SEL_TOPN = 16
B_WINDOW = 512
B_N_KV = 6

C_PATTERNS = ((128, 1), (512, 4), (2048, 16))
C_HEADS_PER_GROUP = D_MODEL // (2 * HEAD_DIM)

MOE_GROUPS = 4
MOE_EXPERTS_PER_GROUP = 8
MOE_EXPERTS = MOE_GROUPS * MOE_EXPERTS_PER_GROUP
MOE_D_FF = D_MODEL // 4
MOE_ROWS = 256
MOE_CHUNKS = 4

DN_ALPHA = (2 * DEPTH) ** 0.25
NORM_EPS = 1e-5

LANES = 128
VMEM_LIMIT = 56 * 1024 * 1024

BF16 = jnp.bfloat16
F32 = jnp.float32


def _cparams(sem):
    return pltpu.CompilerParams(dimension_semantics=sem, vmem_limit_bytes=VMEM_LIMIT)


def _dot(a, b):
    return jnp.dot(a, b, preferred_element_type=F32)


def _dot_nt(a, b):
    return lax.dot_general(a, b, (((1,), (1,)), ((), ())), preferred_element_type=F32)


HALF_LANES = LANES // 2


def _rope_layout(dim):
    n_sub = LANES // dim
    w = HALF_LANES // n_sub
    half = dim // ROPE_FRACTION // 2
    perm, kind, freq = np.zeros(LANES, np.int32), np.zeros(LANES, np.int32), np.zeros(LANES, np.int32)
    for s in range(n_sub):
        lanes = list(range(s * w, (s + 1) * w)) + list(range(HALF_LANES + s * w, HALF_LANES + (s + 1) * w))
        x1, x2 = lanes[:half], lanes[w:w + half]
        rest = [l for l in lanes if l not in x1 and l not in x2]
        for i, l in enumerate(x1):
            perm[l], kind[l], freq[l] = s * dim + i, 1, i
        for i, l in enumerate(x2):
            perm[l], kind[l], freq[l] = s * dim + half + i, 2, i
        for i, l in enumerate(rest):
            perm[l] = s * dim + 2 * half + i
    return perm, kind, freq


def _rope_tables(pos, dim):
    rot = dim // ROPE_FRACTION
    _, kind, freq = _rope_layout(dim)
    inv_freq = ROPE_THETA ** (-jnp.arange(0, rot, 2, dtype=F32) / rot)
    ang = pos.astype(F32)[:, None] * inv_freq[None, :]
    cos, sin = jnp.cos(ang)[:, freq], jnp.sin(ang)[:, freq]
    c = jnp.where(kind[None, :] > 0, cos, 1.0)
    ss = jnp.where(kind[None, :] == 1, -sin, jnp.where(kind[None, :] == 2, sin, 0.0))
    return c, ss


def _permute_blocks(w, blocks, dim):
    perm, _, _ = _rope_layout(dim)
    idx = np.arange(w.shape[1], dtype=np.int32)
    for b in blocks:
        idx[b * LANES:(b + 1) * LANES] = b * LANES + perm
    return w[:, idx]


def _apply_rope(a, c, ss):
    return a * c + pltpu.roll(a, HALF_LANES, 1) * ss


def _proj_kernel(flags_ref, x_ref, w_ref, c_ref, ss_ref, o_ref):
    acc = _dot(x_ref[...], w_ref[...])
    c, ss = c_ref[...], ss_ref[...]
    for s in range(acc.shape[1] // LANES):
        a = acc[:, s * LANES:(s + 1) * LANES]
        o_ref[:, s * LANES:(s + 1) * LANES] = _apply_rope(a, c, ss).astype(o_ref.dtype)


def _proj(xb, w, flags, tables, seq, tm=1024, tn=512):
    m, k = xb.shape
    n = w.shape[1]
    assert m % tm == 0 and n % tn == 0 and seq % tm == 0
    pos_blocks = seq // tm
    identity = (jnp.ones_like(tables[0]), jnp.zeros_like(tables[1]))
    tables = tuple(jnp.stack([ident, tab]) for ident, tab in zip(identity, tables))
    tab_spec = pl.BlockSpec((None, tm, LANES), lambda i, j, f: (f[j], i % pos_blocks, 0))
    return pl.pallas_call(
        _proj_kernel,
        grid_spec=pltpu.PrefetchScalarGridSpec(
            num_scalar_prefetch=1,
            grid=(m // tm, n // tn),
            in_specs=[pl.BlockSpec((tm, k), lambda i, j, f: (i, 0)),
                      pl.BlockSpec((k, tn), lambda i, j, f: (0, j)),
                      tab_spec, tab_spec],
            out_specs=pl.BlockSpec((tm, tn), lambda i, j, f: (i, j))),
        out_shape=jax.ShapeDtypeStruct((m, n), BF16),
        compiler_params=_cparams(("parallel", "arbitrary")),
        name="proj",
    )(flags, xb, w, *tables)


def _layer_norm(z, g, b):
    mu = jnp.mean(z, axis=-1, keepdims=True)
    zc = z - mu
    var = jnp.mean(zc * zc, axis=-1, keepdims=True)
    return zc * lax.rsqrt(var + NORM_EPS) * g + b


def _outproj_ln_kernel(o_ref, w_ref, h_ref, ln_ref, hf_ref, hb_ref):
    half = o_ref.shape[0] // 2
    for s in range(2):
        rows = slice(s * half, (s + 1) * half)
        m = _dot(o_ref[rows, :], w_ref[...])
        y = _layer_norm(DN_ALPHA * h_ref[rows, :] + m, ln_ref[0:1, :], ln_ref[1:2, :])
        hf_ref[rows, :] = y
        hb_ref[rows, :] = y.astype(BF16)


def _outproj_ln(o, w, h, ln, tm=512):
    m, k = o.shape
    d = w.shape[1]
    return pl.pallas_call(
        _outproj_ln_kernel,
        grid=(m // tm,),
        in_specs=[pl.BlockSpec((tm, k), lambda i: (i, 0)),
                  pl.BlockSpec((k, d), lambda i: (0, 0)),
                  pl.BlockSpec((tm, d), lambda i: (i, 0)),
                  pl.BlockSpec((4, d), lambda i: (0, 0))],
        out_specs=[pl.BlockSpec((tm, d), lambda i: (i, 0)),
                   pl.BlockSpec((tm, d), lambda i: (i, 0))],
        out_shape=[jax.ShapeDtypeStruct((m, d), F32), jax.ShapeDtypeStruct((m, d), BF16)],
        compiler_params=_cparams(("parallel",)),
        name="outproj_ln",
    )(o, w, h, ln)


def _causal_block(q, k, v, c, tq):
    s = _dot_nt(q, k)
    rows = q.shape[0]
    assert tq & (tq - 1) == 0
    pos_in_block = jnp.bitwise_and(lax.broadcasted_iota(jnp.int32, (rows, tq), 0), tq - 1)
    causal = lax.broadcasted_iota(jnp.int32, (rows, tq), 1) <= pos_in_block
    s_diag = jnp.where(causal, s[:, c * tq:], -jnp.inf)
    m = jnp.max(s_diag, axis=-1, keepdims=True)
    if c > 0:
        s_off = s[:, :c * tq]
        m = jnp.maximum(m, jnp.max(s_off, axis=-1, keepdims=True))
        e = jnp.concatenate([jnp.exp(s_off - m), jnp.exp(s_diag - m)], axis=1)
    else:
        e = jnp.exp(s_diag - m)
    inv = 1.0 / jnp.sum(e, axis=-1, keepdims=True)
    return _dot(e.astype(BF16), v) * inv


def _attn_a_kernel(lam_ref, q_ref, k_ref, v_ref, g_ref, o_ref, *, tq, seq, out_scale):
    for c in range(seq // tq):
        kv = (c + 1) * tq
        q = q_ref[0, c * tq:kv, :] * jnp.asarray(A_QK_DIM ** -0.5, BF16)
        lane = lax.broadcasted_iota(jnp.int32, q.shape, 1)
        zero = jnp.zeros_like(q)
        first = jnp.bitwise_and(lane, HALF_LANES // (LANES // A_QK_DIM)) == 0
        q12 = jnp.concatenate([jnp.where(first, q, zero), jnp.where(first, zero, q)], axis=0)
        both = _causal_block(q12, k_ref[0, :kv, :], v_ref[0, :kv, :], c, tq)
        o = both[:tq] - lam_ref[0] * both[tq:]
        o = o * lax.rsqrt(jnp.mean(o * o, axis=-1, keepdims=True) + NORM_EPS) * (g_ref[...] * out_scale)
        o_ref[0, c * tq:kv, :] = o.astype(o_ref.dtype)


def _attn_a(proj, lam_full, subln, batch, seq, lam_init, tq=256):
    h = A_HEADS
    return pl.pallas_call(
        functools.partial(_attn_a_kernel, tq=tq, seq=seq, out_scale=1.0 - lam_init),
        grid_spec=pltpu.PrefetchScalarGridSpec(
            num_scalar_prefetch=1,
            grid=(batch, h),
            in_specs=[pl.BlockSpec((1, seq, HEAD_DIM), lambda b, hh, s: (b, 0, hh)),
                      pl.BlockSpec((1, seq, HEAD_DIM), lambda b, hh, s: (b, 0, h + hh)),
                      pl.BlockSpec((1, seq, HEAD_DIM), lambda b, hh, s: (b, 0, 2 * h + hh)),
                      pl.BlockSpec((1, HEAD_DIM), lambda b, hh, s: (0, 0))],
            out_specs=pl.BlockSpec((1, seq, HEAD_DIM), lambda b, hh, s: (b, 0, hh))),
        out_shape=jax.ShapeDtypeStruct((batch, seq, h * HEAD_DIM), BF16),
        compiler_params=_cparams(("parallel", "parallel")),
        name="attn_a",
    )(lam_full, proj, proj, proj, subln)


def _mixer_a(hb, h, w_in, lam, subln, w_out, ln, layer_idx, batch, seq):
    pos = jnp.arange(seq)
    tables = _rope_tables(pos, A_QK_DIM)
    n_q = A_HEADS * HEAD_DIM
    tn = 512
    flags = (jnp.arange(3 * n_q // tn) < 2 * n_q // tn).astype(jnp.int32)
    wb = _permute_blocks(w_in, range(2 * A_HEADS), A_QK_DIM).astype(BF16)
    proj = _proj(hb, wb, flags, tables, seq, tn=tn)
    lam_init = 0.8 - 0.6 * math.exp(-0.3 * layer_idx)
    lf = lam.astype(F32)
    lam_full = jnp.exp(jnp.sum(lf[0] * lf[1])) - jnp.exp(jnp.sum(lf[2] * lf[3])) + lam_init
    o = _attn_a(proj.reshape(batch, seq, -1), lam_full.reshape(1), subln.reshape(1, HEAD_DIM), batch, seq, lam_init)
    return _outproj_ln(o.reshape(batch * seq, -1), w_out.astype(BF16), h, ln)


DIL_BLOCK = 128


def _band_block(q, k, v, q0, k0, span):
    s = _dot_nt(q, k)
    diff = (q0 + lax.broadcasted_iota(jnp.int32, s.shape, 0)) - (k0 + lax.broadcasted_iota(jnp.int32, s.shape, 1))
    s = jnp.where((diff >= 0) & (diff <= span), s, -jnp.inf)
    m = jnp.max(s, axis=-1, keepdims=True)
    e = jnp.exp(s - m)
    den = jnp.sum(e, axis=-1, keepdims=True)
    return _dot((e * (1.0 / den)).astype(BF16), v), m + jnp.log(den)


def _dil_kernel(*refs, seq):
    n_grp = len(C_PATTERNS)
    qkv = refs[:3 * n_grp]
    o_ref = refs[3 * n_grp]
    stage, o_scr, l_scr = refs[3 * n_grp + 1:]
    blk = DIL_BLOCK
    for g, (window, dil) in enumerate(C_PATTERNS):
        q_ref, k_ref, v_ref = qkv[3 * g:3 * g + 3]
        length = seq // dil
        span = window // dil
        n_blk = length // blk
        if dil > 1:
            for j, ref in enumerate((q_ref, k_ref, v_ref)):
                stage[j] = ref[0].astype(F32)

        def rows(j, ref, r, start, size):
            if dil == 1:
                return ref[0, start:start + size, :]
            return stage[j, pl.ds(r + start * dil, size, stride=dil), :].astype(BF16)

        for r in range(dil):
            for i in range(n_blk):
                k0 = max(i - 1, 0) * blk
                kl = (i + 1) * blk - k0
                o, lse = _band_block(rows(0, q_ref, r, i * blk, blk), rows(1, k_ref, r, k0, kl),
                                     rows(2, v_ref, r, k0, kl), i * blk, k0, span)
                lse = jnp.broadcast_to(lse, (blk, HEAD_DIM))
                if dil == 1:
                    o_scr[g, i * blk:(i + 1) * blk, :] = o
                    l_scr[g, i * blk:(i + 1) * blk, :] = lse
                else:
                    o_scr[g, pl.ds(r + i * blk * dil, blk, stride=dil), :] = o
                    l_scr[g, pl.ds(r + i * blk * dil, blk, stride=dil), :] = lse

    chunk = 256
    for c in range(seq // chunk):
        sl = slice(c * chunk, (c + 1) * chunk)
        ls = [l_scr[g, sl, :] for g in range(n_grp)]
        m = functools.reduce(jnp.maximum, ls)
        es = [jnp.exp(l - m) for l in ls]
        inv = 1.0 / functools.reduce(lambda a, b: a + b, es)
        o = functools.reduce(lambda a, b: a + b, [(es[g] * inv) * o_scr[g, sl, :] for g in range(n_grp)])
        o_ref[0, sl, :] = o.astype(o_ref.dtype)


def _dil_attn(proj, batch, seq):
    hg, p = C_HEADS_PER_GROUP, len(C_PATTERNS)

    def spec(kind, g):
        return pl.BlockSpec((1, seq, HEAD_DIM), lambda b, hh: (b, 0, kind * p * hg + g * hg + hh))

    in_specs = [spec(kind, g) for g in range(p) for kind in range(3)]
    return pl.pallas_call(
        functools.partial(_dil_kernel, seq=seq),
        grid=(batch, hg),
        in_specs=in_specs,
        out_specs=pl.BlockSpec((1, seq, HEAD_DIM), lambda b, hh: (b, 0, hh)),
        out_shape=jax.ShapeDtypeStruct((batch, seq, hg * HEAD_DIM), BF16),
        scratch_shapes=[pltpu.VMEM((3, seq, HEAD_DIM), F32), pltpu.VMEM((p, seq, HEAD_DIM), F32),
                        pltpu.VMEM((p, seq, HEAD_DIM), F32)],
        compiler_params=_cparams(("parallel", "parallel")),
        name="dil_attn",
    )(*([proj] * len(in_specs)))


def _mixer_c(hb, h, w_in, w_out, ln, batch, seq):
    tables = _rope_tables(jnp.arange(seq), HEAD_DIM)
    tn = 512
    n = w_in.shape[1]
    flags = (jnp.arange(n // tn) < 2 * (n // 3) // tn).astype(jnp.int32)
    wb = _scaled_bf16(_permute_blocks(w_in, range(2 * n // 3 // LANES), HEAD_DIM), n // 3, HEAD_DIM ** -0.5)
    proj = _proj(hb, wb, flags, tables, seq, tn=tn)
    o = _dil_attn(proj.reshape(batch, seq, n), batch, seq)
    return _outproj_ln(o.reshape(batch * seq, -1), w_out.astype(BF16), h, ln)


def _gates_kernel(x_ref, w_ref, o_ref):
    z = _dot(x_ref[...], w_ref[...])
    o_ref[...] = 1.0 / (1.0 + jnp.exp(-z))


def _gates(xb, wg, tm=1024):
    m, k = xb.shape
    n = wg.shape[1]
    return pl.pallas_call(
        _gates_kernel,
        grid=(m // tm,),
        in_specs=[pl.BlockSpec((tm, k), lambda i: (i, 0)), pl.BlockSpec((k, n), lambda i: (0, 0))],
        out_specs=pl.BlockSpec((tm, n), lambda i: (i, 0)),
        out_shape=jax.ShapeDtypeStruct((m, n), F32),
        compiler_params=_cparams(("parallel",)),
        name="nsa_gates",
    )(xb, wg)


def _gelu_tanh(x):
    return 0.5 * x * (1.0 + jnp.tanh(math.sqrt(2.0 / math.pi) * (x + 0.044715 * (x * x * x))))


def _compress_kernel(a_ref, pos_ref, w1_ref, w2_ref, c_ref, ss_ref, o_ref, stage):
    kind = pl.program_id(2)
    stage[...] = a_ref[0].astype(F32)
    n_chunk = stage.shape[0] // B_CMP_STRIDE
    first = jnp.zeros((n_chunk, HEAD_DIM), F32)
    second = jnp.zeros((n_chunk, HEAD_DIM), F32)
    for t in range(B_CMP_STRIDE):
        a = stage[pl.ds(t, n_chunk, stride=B_CMP_STRIDE), :]
        u = B_CMP_STRIDE + t
        first = first + _dot((a + pos_ref[0, t:t + 1, :]).astype(BF16), w1_ref[0, t * HEAD_DIM:(t + 1) * HEAD_DIM, :])
        second = second + _dot((a + pos_ref[0, u:u + 1, :]).astype(BF16), w1_ref[0, u * HEAD_DIM:(u + 1) * HEAD_DIM, :])
    hmid = first + pltpu.roll(second, n_chunk - 1, 0)
    out = _dot(_gelu_tanh(hmid).astype(BF16), w2_ref[0])

    @pl.when(kind == 0)
    def _():
        o_ref[0, 0] = _apply_rope(out, c_ref[...], ss_ref[...]).astype(o_ref.dtype)

    @pl.when(kind != 0)
    def _():
        o_ref[0, 0] = out.astype(o_ref.dtype)


def _compress(proj3, cmp_pos, cmp_w1, cmp_w2, batch, seq):
    g = B_KV_GROUPS
    n_chunk = seq // B_CMP_STRIDE
    cmp_end = jnp.arange(n_chunk) * B_CMP_STRIDE + B_CMP_LEN - 1
    tables = _rope_tables(cmp_end, HEAD_DIM)
    tab_spec = pl.BlockSpec((n_chunk, LANES), lambda b, gg, kk: (0, 0))
    perm, _, _ = _rope_layout(HEAD_DIM)
    cmp_w2 = jnp.stack([cmp_w2[0][:, perm], cmp_w2[1]])
    return pl.pallas_call(
        _compress_kernel,
        grid=(batch, g, 2),
        in_specs=[pl.BlockSpec((1, seq, HEAD_DIM), lambda b, gg, kk: (b, 0, B_HEADS + kk * g + gg)),
                  pl.BlockSpec((1, B_CMP_LEN, HEAD_DIM), lambda b, gg, kk: (kk, 0, 0)),
                  pl.BlockSpec((1, B_CMP_LEN * HEAD_DIM, HEAD_DIM), lambda b, gg, kk: (kk, 0, 0)),
                  pl.BlockSpec((1, HEAD_DIM, HEAD_DIM), lambda b, gg, kk: (kk, 0, 0)),
                  tab_spec, tab_spec],
        out_specs=pl.BlockSpec((1, 1, n_chunk, HEAD_DIM), lambda b, gg, kk: (kk * batch + b, gg, 0, 0)),
        out_shape=jax.ShapeDtypeStruct((2 * batch, g, n_chunk, HEAD_DIM), BF16),
        scratch_shapes=[pltpu.VMEM((seq, HEAD_DIM), F32)],
        compiler_params=_cparams(("parallel", "parallel", "arbitrary")),
        name="nsa_compress",
    )(proj3, cmp_pos, cmp_w1.astype(BF16), cmp_w2.astype(BF16), *tables)


def _split3(x):
    hi = x.astype(BF16)
    r = x - hi.astype(F32)
    mid = r.astype(BF16)
    lo = (r - mid.astype(F32)).astype(BF16)
    return hi, mid, lo


def _nsa_kernel(q_ref, kc_ref, vc_ref, ks_ref, vs_ref, kw_ref, vw_ref, gate_ref, ex_ref, o_ref,
                vct_scr, ocmp_scr, *, tq, seq, n_cmp, n_sel):
    rep = B_REP
    big = 1e30
    vct_scr[...] = vc_ref[0, 0].astype(F32).T.astype(BF16)
    kc = kc_ref[0, 0]
    lane = lax.broadcasted_iota(jnp.int32, (tq, LANES), 1)
    n_back = -(-(B_WINDOW - 1) // tq)

    for c in range(seq // tq):
        lo, hi = c * tq, (c + 1) * tq

        qpos = lo + lax.broadcasted_iota(jnp.int32, (1, tq), 1)
        crow = lax.broadcasted_iota(jnp.int32, (LANES, tq), 0)
        cmask = (crow * B_CMP_STRIDE + (B_CMP_LEN - 1) <= qpos) & (crow < n_cmp)

        p_sum = jnp.zeros((LANES, tq), F32)
        for r in range(rep):
            s = jnp.where(cmask, _dot_nt(kc, q_ref[0, lo:hi, r * HEAD_DIM:(r + 1) * HEAD_DIM]), NEG_FILL)
            e = jnp.exp(s - jnp.max(s, axis=0, keepdims=True))
            p = jnp.where(cmask, e * (1.0 / jnp.sum(e, axis=0, keepdims=True)), 0.0)
            ocmp_scr[r] = _dot(vct_scr[...], p.astype(BF16)).T
            p_sum = p_sum + p

        ni = lax.broadcasted_iota(jnp.int32, (LANES, LANES), 0)
        ci = lax.broadcasted_iota(jnp.int32, (LANES, LANES), 1)
        per = B_SEL_LEN // B_CMP_STRIDE
        cover = ((ci >= per * ni - (B_CMP_LEN // B_CMP_STRIDE - 1)) & (ci < per * ni + per) & (ci < n_cmp) & (ni < n_sel))
        cover = jnp.where(cover, 1.0, 0.0).astype(BF16)
        importance = sum(_dot(cover, part) for part in _split3(p_sum))[:n_sel]
        nrow = lax.broadcasted_iota(jnp.int32, (n_sel, tq), 0)
        cur = jnp.right_shift(qpos, int(math.log2(B_SEL_LEN)))
        forced = (nrow == 0) | (nrow == cur) | (nrow == cur - 1)
        score = jnp.where(forced, big, importance)
        score = jnp.where(nrow <= cur, score, -big)
        rank = jnp.zeros((n_sel, tq), jnp.int32)
        for mm in range(min(n_sel, (hi - 1) // B_SEL_LEN + 1)):
            other = score[mm:mm + 1, :]
            ahead = (other > score) | ((other == score) & (nrow > mm))
            rank = rank + jnp.where(ahead, 1, 0)
        sel_bias = jnp.where(rank < min(B_SEL_TOPN, n_sel), 0.0, -big)
        sel_bias = jnp.concatenate([sel_bias, jnp.zeros((LANES - n_sel, tq), F32)], axis=0).T.astype(BF16)

        k_aug = jnp.concatenate([ks_ref[0, :hi, :], ex_ref[:hi, :]], axis=1)
        v_sel = vs_ref[0, :hi, :]
        wlo = max(c - n_back, 0) * tq
        k_win, v_win = kw_ref[0, wlo:hi, :], vw_ref[0, wlo:hi, :]
        pair_rows = 2 * tq
        qrow = lo + jnp.bitwise_and(lax.broadcasted_iota(jnp.int32, (pair_rows, hi - wlo), 0), tq - 1)
        diff = qrow - (wlo + lax.broadcasted_iota(jnp.int32, (pair_rows, hi - wlo), 1))
        wmask = (diff >= 0) & (diff <= B_WINDOW - 1)
        gates = gate_ref[0, lo:hi, :]
        bias2 = jnp.concatenate([sel_bias, sel_bias], axis=0)

        def head_pair(pr, carry):
            q2 = q_ref[0, lo:hi, pl.ds(pl.multiple_of(pr * 2 * HEAD_DIM, 2 * HEAD_DIM), 2 * HEAD_DIM)]
            q = jnp.concatenate([q2[:, :HEAD_DIM], q2[:, HEAD_DIM:]], axis=0)
            o_sel = _causal_block(jnp.concatenate([q, bias2], axis=1), k_aug, v_sel, c, tq)
            s = jnp.where(wmask, _dot_nt(q, k_win), -jnp.inf)
            e = jnp.exp(s - jnp.max(s, axis=-1, keepdims=True))
            o_win = _dot(e.astype(BF16), v_win) * (1.0 / jnp.sum(e, axis=-1, keepdims=True))
            outs = []
            for j in range(2):
                r = pr * 2 + j

                def gate(b):
                    return jnp.sum(jnp.where(lane == r * 3 + b, gates, 0.0), axis=-1, keepdims=True)

                rows = slice(j * tq, (j + 1) * tq)
                outs.append(gate(0) * ocmp_scr[r] + gate(1) * o_sel[rows] + gate(2) * o_win[rows])
            o_ref[0, lo:hi, pl.ds(pl.multiple_of(pr * 2 * HEAD_DIM, 2 * HEAD_DIM), 2 * HEAD_DIM)] = (
                jnp.concatenate(outs, axis=1).astype(o_ref.dtype))
            return carry

        lax.fori_loop(0, rep // 2, head_pair, 0)


def _nsa_attn(proj3, kvc, gates, batch, seq, tq=256):
    g, rep = B_KV_GROUPS, B_REP
    n_chunk = seq // B_CMP_STRIDE
    n_cmp = n_chunk - B_CMP_LEN // B_CMP_STRIDE + 1
    n_sel = seq // B_SEL_LEN
    assert n_chunk == LANES and n_sel <= LANES and n_sel % 8 == 0
    onehot = (jnp.arange(seq)[:, None] // B_SEL_LEN == jnp.arange(LANES)[None, :]).astype(BF16)

    def kv_spec(j):
        return pl.BlockSpec((1, seq, HEAD_DIM), lambda b, gg: (b, 0, B_HEADS + j * g + gg))

    return pl.pallas_call(
        functools.partial(_nsa_kernel, tq=tq, seq=seq, n_cmp=n_cmp, n_sel=n_sel),
        grid=(batch, g),
        in_specs=[pl.BlockSpec((1, seq, rep * HEAD_DIM), lambda b, gg: (b, 0, gg)),
                  pl.BlockSpec((1, 1, n_chunk, HEAD_DIM), lambda b, gg: (b, gg, 0, 0)),
                  pl.BlockSpec((1, 1, n_chunk, HEAD_DIM), lambda b, gg: (batch + b, gg, 0, 0)),
                  kv_spec(2), kv_spec(3), kv_spec(4), kv_spec(5),
                  pl.BlockSpec((1, seq, LANES), lambda b, gg: (b, 0, gg)),
                  pl.BlockSpec((seq, LANES), lambda b, gg: (0, 0))],
        out_specs=pl.BlockSpec((1, seq, rep * HEAD_DIM), lambda b, gg: (b, 0, gg)),
        out_shape=jax.ShapeDtypeStruct((batch, seq, B_HEADS * HEAD_DIM), BF16),
        scratch_shapes=[pltpu.VMEM((HEAD_DIM, n_chunk), BF16), pltpu.VMEM((rep, tq, HEAD_DIM), F32)],
        compiler_params=_cparams(("parallel", "parallel")),
        name="nsa_attn",
    )(proj3, kvc, kvc, proj3, proj3, proj3, proj3, gates.reshape(batch, seq, g * LANES), onehot)


def _scaled_bf16(w, n_cols, scale):
    col_scale = jnp.where(jnp.arange(w.shape[1]) < n_cols, scale, 1.0).astype(F32)
    return (w * col_scale[None, :]).astype(BF16)


def _mixer_b(hb, h, w_in, cmp_pos, cmp_w1, cmp_w2, w_out, ln, batch, seq):
    g, rep = B_KV_GROUPS, B_REP
    n_main = (B_HEADS + B_N_KV * g) * HEAD_DIM
    tn = 512
    per = g * HEAD_DIM // tn
    nq = B_HEADS * HEAD_DIM // tn
    jb = jnp.arange(n_main // tn)
    flags = ((jb < nq) | ((jb >= nq + 2 * per) & (jb < nq + 3 * per))
             | ((jb >= nq + 4 * per) & (jb < nq + 5 * per))).astype(jnp.int32)
    tables = _rope_tables(jnp.arange(seq), HEAD_DIM)
    roped = (list(range(B_HEADS)) + list(range(B_HEADS + 2 * g, B_HEADS + 3 * g))
             + list(range(B_HEADS + 4 * g, B_HEADS + 5 * g)))
    wb = _scaled_bf16(_permute_blocks(w_in, roped, HEAD_DIM), B_HEADS * HEAD_DIM, HEAD_DIM ** -0.5)
    proj3 = _proj(hb, wb[:, :n_main], flags, tables, seq, tn=tn).reshape(batch, seq, n_main)
    wg = wb[:, n_main:].reshape(-1, g, rep * 3)
    wg = jnp.pad(wg, ((0, 0), (0, 0), (0, LANES - rep * 3))).reshape(-1, g * LANES)
    gates = _gates(hb, wg)
    kvc = _compress(proj3, cmp_pos, cmp_w1, cmp_w2, batch, seq)
    o = _nsa_attn(proj3, kvc, gates, batch, seq)
    return _outproj_ln(o.reshape(batch * seq, -1), w_out.astype(BF16), h, ln)


def _router_kernel(x_ref, w_ref, b_ref, o_ref):
    x = x_ref[...]
    xh = x.astype(BF16)
    xl = (x - xh.astype(F32)).astype(BF16)
    hi = _dot(xh, w_ref[...])
    logits = hi[:, :LANES] + (hi[:, LANES:] + _dot(xl, w_ref[:, :LANES])) + b_ref[...]
    lane = lax.broadcasted_iota(jnp.int32, logits.shape, 1)
    lane_f = lane.astype(F32)

    def first_max(vals):
        top = jnp.max(vals, axis=-1, keepdims=True)
        first = jnp.min(jnp.where(vals == top, lane_f, float(4 * LANES)), axis=-1, keepdims=True)
        return top, first.astype(jnp.int32)

    gl = jnp.where(lane < MOE_GROUPS, logits, -jnp.inf)
    gmax, gidx = first_max(gl)
    g_w = 1.0 / jnp.sum(jnp.exp(gl - gmax), axis=-1, keepdims=True)
    lo = MOE_GROUPS + MOE_EXPERTS_PER_GROUP * gidx
    ev = jnp.where((lane >= lo) & (lane < lo + MOE_EXPERTS_PER_GROUP), logits, -jnp.inf)
    v1, i1 = first_max(ev)
    v2, i2 = first_max(jnp.where(lane == i1, -jnp.inf, ev))
    e2 = jnp.exp(v2 - v1)
    w1 = g_w / (1.0 + e2)
    w2 = g_w * e2 / (1.0 + e2)
    out = jnp.where(lane == 0, (i1 - MOE_GROUPS).astype(F32),
                    jnp.where(lane == 1, (i2 - MOE_GROUPS).astype(F32),
                              jnp.where(lane == 2, w1, jnp.where(lane == 3, w2, 0.0))))
    o_ref[...] = out


def _router(h, router_w, router_b, tm=1024):
    m, d = h.shape
    n = router_w.shape[1]
    wp = jnp.pad(router_w, ((0, 0), (0, LANES - n)))
    wh = wp.astype(BF16)
    wp = jnp.concatenate([wh, (wp - wh.astype(F32)).astype(BF16)], axis=1)
    bp = jnp.pad(router_b, (0, LANES - n)).reshape(1, LANES)
    return pl.pallas_call(
        _router_kernel,
        grid=(m // tm,),
        in_specs=[pl.BlockSpec((tm, d), lambda i: (i, 0)),
                  pl.BlockSpec((d, 2 * LANES), lambda i: (0, 0)),
                  pl.BlockSpec((1, LANES), lambda i: (0, 0))],
        out_specs=pl.BlockSpec((tm, LANES), lambda i: (i, 0)),
        out_shape=jax.ShapeDtypeStruct((m, LANES), F32),
        compiler_params=_cparams(("parallel",)),
        name="moe_router",
    )(h, wp, bp)


def _experts_kernel(be_ref, nu_ref, x_ref, wi_ref, wo_ref, *rest, blk0):
    y_ref, wi_scr, wo_scr = rest[-3:]
    i = pl.program_id(0)
    blk = i + blk0

    @pl.when((i == 0) | (be_ref[blk] != be_ref[jnp.maximum(blk - 1, 0)]))
    def _():
        wi_scr[...] = wi_ref[0].astype(BF16)
        wo_scr[...] = wo_ref[0].astype(BF16)

    @pl.when(blk < nu_ref[0])
    def _():
        gu = _dot(x_ref[...], wi_scr[...])
        gate, up = gu[:, :MOE_D_FF], gu[:, MOE_D_FF:]
        act = gate * (1.0 / (1.0 + jnp.exp(-gate))) * up
        y_ref[...] = _dot(act.astype(BF16), wo_scr[...]).astype(y_ref.dtype)

    @pl.when(blk >= nu_ref[0])
    def _():
        y_ref[...] = jnp.zeros_like(y_ref)


def _experts(xb, blk_e, n_used, w_in, w_out, ybuf, blk0):
    rows, d = xb.shape
    n_total = blk_e.shape[0]
    in_specs = [pl.BlockSpec((MOE_ROWS, d), lambda i, be, nu: (i, 0)),
                pl.BlockSpec((1, d, 2 * MOE_D_FF), lambda i, be, nu: (be[i + blk0], 0, 0)),
                pl.BlockSpec((1, MOE_D_FF, d), lambda i, be, nu: (be[i + blk0], 0, 0))]
    args = [blk_e, n_used, xb, w_in, w_out]
    aliases = {}
    if ybuf is not None:
        in_specs.append(pl.BlockSpec(memory_space=pl.ANY))
        args.append(ybuf)
        aliases = {len(args) - 1: 0}
    return pl.pallas_call(
        functools.partial(_experts_kernel, blk0=blk0),
        grid_spec=pltpu.PrefetchScalarGridSpec(
            num_scalar_prefetch=2,
            grid=(rows // MOE_ROWS,),
            in_specs=in_specs,
            out_specs=pl.BlockSpec((MOE_ROWS, d), lambda i, be, nu: (i + blk0, 0)),
            scratch_shapes=[pltpu.VMEM((d, 2 * MOE_D_FF), BF16), pltpu.VMEM((MOE_D_FF, d), BF16)]),
        out_shape=jax.ShapeDtypeStruct((n_total * MOE_ROWS, d), BF16),
        input_output_aliases=aliases,
        compiler_params=_cparams(("arbitrary",)),
        name="moe_experts",
    )(*args)


def _moe_ln_kernel(h_ref, y0_ref, y1_ref, r_ref, ln_ref, hf_ref, hb_ref):
    r = r_ref[...]
    f = y0_ref[...].astype(F32) * r[:, 2:3] + y1_ref[...].astype(F32) * r[:, 3:4]
    y = _layer_norm(DN_ALPHA * h_ref[...] + f, ln_ref[2:3, :], ln_ref[3:4, :])
    hf_ref[...] = y
    hb_ref[...] = y.astype(BF16)


def _moe_ln(h, y0, y1, route, ln, tm=512):
    m, d = h.shape
    row = lambda i: (i, 0)
    return pl.pallas_call(
        _moe_ln_kernel,
        grid=(m // tm,),
        in_specs=[pl.BlockSpec((tm, d), row), pl.BlockSpec((tm, d), row), pl.BlockSpec((tm, d), row),
                  pl.BlockSpec((tm, LANES), row), pl.BlockSpec((4, d), lambda i: (0, 0))],
        out_specs=[pl.BlockSpec((tm, d), row), pl.BlockSpec((tm, d), row)],
        out_shape=[jax.ShapeDtypeStruct((m, d), F32), jax.ShapeDtypeStruct((m, d), BF16)],
        compiler_params=_cparams(("parallel",)),
        name="moe_ln",
    )(h, y0, y1, route, ln)


def _moe(h, hb, router_w, router_b, w_in, w_out, ln):
    t, d = h.shape
    e, rows = MOE_EXPERTS, MOE_ROWS
    route = _router(h, router_w, router_b)
    flat_e = route[:, 0:2].astype(jnp.int32).reshape(-1)
    n_assign = flat_e.shape[0]
    n_blk = -(-n_assign // rows) + e
    onehot = (flat_e[:, None] == jnp.arange(e, dtype=jnp.int32)[None, :]).astype(jnp.int32)
    csum = jnp.cumsum(onehot, axis=0)
    rank = jnp.take_along_axis(csum, flat_e[:, None], axis=1)[:, 0] - 1
    sizes = csum[-1]
    padded = (sizes + rows - 1) // rows * rows
    pad_end = jnp.cumsum(padded)
    pad_start = pad_end - padded
    dest = pad_start[flat_e] + rank
    flat_tok = jnp.arange(n_assign, dtype=jnp.int32) // 2
    slot_tok = jnp.zeros((n_blk * rows,), jnp.int32).at[dest].set(flat_tok)
    blk_start = jnp.arange(n_blk, dtype=jnp.int32) * rows
    blk_e = jnp.minimum(jnp.sum((pad_end[None, :] <= blk_start[:, None]).astype(jnp.int32), axis=1), e - 1)
    n_used = (pad_end[-1] // rows).astype(jnp.int32).reshape(1)
    assert n_blk % MOE_CHUNKS == 0
    per = n_blk // MOE_CHUNKS
    yb = None
    for c in range(MOE_CHUNKS):
        xb = hb[slot_tok[c * per * rows:(c + 1) * per * rows]]
        yb = _experts(xb, blk_e, n_used, w_in, w_out, yb, c * per)
    dest2 = dest.reshape(t, 2)
    return _moe_ln(h, yb[dest2[:, 0]], yb[dest2[:, 1]], route, ln)


def kernel(x, l0_a_w_in, l0_a_lam, l0_a_subln, l0_a_w_out, l0_ln, l0_router_w, l0_router_b, l0_moe_w_in, l0_moe_w_out, l1_b_w_in, l1_b_cmp_pos, l1_b_cmp_w1, l1_b_cmp_w2, l1_b_w_out, l1_ln, l1_router_w, l1_router_b, l1_moe_w_in, l1_moe_w_out, l2_c_w_in, l2_c_w_out, l2_ln, l2_router_w, l2_router_b, l2_moe_w_in, l2_moe_w_out, l3_a_w_in, l3_a_lam, l3_a_subln, l3_a_w_out, l3_ln, l3_router_w, l3_router_b, l3_moe_w_in, l3_moe_w_out):
    batch, seq, d = x.shape
    h = x.reshape(batch * seq, d)
    hb = h.astype(BF16)
    h, hb = _mixer_a(hb, h, l0_a_w_in, l0_a_lam, l0_a_subln, l0_a_w_out, l0_ln, 0, batch, seq)
    h, hb = _moe(h, hb, l0_router_w, l0_router_b, l0_moe_w_in, l0_moe_w_out, l0_ln)
    h, hb = _mixer_b(hb, h, l1_b_w_in, l1_b_cmp_pos, l1_b_cmp_w1, l1_b_cmp_w2, l1_b_w_out, l1_ln, batch, seq)
    h, hb = _moe(h, hb, l1_router_w, l1_router_b, l1_moe_w_in, l1_moe_w_out, l1_ln)
    h, hb = _mixer_c(hb, h, l2_c_w_in, l2_c_w_out, l2_ln, batch, seq)
    h, hb = _moe(h, hb, l2_router_w, l2_router_b, l2_moe_w_in, l2_moe_w_out, l2_ln)
    h, hb = _mixer_a(hb, h, l3_a_w_in, l3_a_lam, l3_a_subln, l3_a_w_out, l3_ln, 3, batch, seq)
    h, hb = _moe(h, hb, l3_router_w, l3_router_b, l3_moe_w_in, l3_moe_w_out, l3_ln)
    return h.reshape(batch, seq, d)
```

```python
import functools
import math

import jax
import jax.numpy as jnp
import numpy as np
from jax import lax
from jax.experimental import pallas as pl
from jax.experimental.pallas import tpu as pltpu

D_MODEL = 2048
DEPTH = 4
HEAD_DIM = 128
ROPE_THETA = 500000.0
ROPE_FRACTION = 4
NEG_FILL = -1e30

A_HEADS = D_MODEL // HEAD_DIM
A_QK_DIM = HEAD_DIM // 2

B_HEADS = D_MODEL // HEAD_DIM
B_KV_GROUPS = 4
B_REP = B_HEADS // B_KV_GROUPS
B_CMP_LEN = 32
B_CMP_STRIDE = 16
B_SEL_LEN = 64
B_SEL_TOPN = 16
B_WINDOW = 512
B_N_KV = 6

C_PATTERNS = ((128, 1), (512, 4), (2048, 16))
C_HEADS_PER_GROUP = D_MODEL // (2 * HEAD_DIM)

MOE_GROUPS = 4
MOE_EXPERTS_PER_GROUP = 8
MOE_EXPERTS = MOE_GROUPS * MOE_EXPERTS_PER_GROUP
MOE_D_FF = D_MODEL // 4
MOE_ROWS = 256
MOE_CHUNKS = 2
BATCH_STREAMS = 2

DN_ALPHA = (2 * DEPTH) ** 0.25
NORM_EPS = 1e-5

LANES = 128
VMEM_LIMIT = 56 * 1024 * 1024

BF16 = jnp.bfloat16
F32 = jnp.float32


def _cparams(sem):
    return pltpu.CompilerParams(dimension_semantics=sem, vmem_limit_bytes=VMEM_LIMIT)


def _dot(a, b):
    return jnp.dot(a, b, preferred_element_type=F32)


def _dot_nt(a, b):
    return lax.dot_general(a, b, (((1,), (1,)), ((), ())), preferred_element_type=F32)


HALF_LANES = LANES // 2


def _rope_layout(dim):
    n_sub = LANES // dim
    w = HALF_LANES // n_sub
    half = dim // ROPE_FRACTION // 2
    perm, kind, freq = np.zeros(LANES, np.int32), np.zeros(LANES, np.int32), np.zeros(LANES, np.int32)
    for s in range(n_sub):
        lanes = list(range(s * w, (s + 1) * w)) + list(range(HALF_LANES + s * w, HALF_LANES + (s + 1) * w))
        x1, x2 = lanes[:half], lanes[w:w + half]
        rest = [l for l in lanes if l not in x1 and l not in x2]
        for i, l in enumerate(x1):
            perm[l], kind[l], freq[l] = s * dim + i, 1, i
        for i, l in enumerate(x2):
            perm[l], kind[l], freq[l] = s * dim + half + i, 2, i
        for i, l in enumerate(rest):
            perm[l] = s * dim + 2 * half + i
    return perm, kind, freq


def _rope_tables(pos, dim):
    rot = dim // ROPE_FRACTION
    _, kind, freq = _rope_layout(dim)
    inv_freq = ROPE_THETA ** (-jnp.arange(0, rot, 2, dtype=F32) / rot)
    ang = pos.astype(F32)[:, None] * inv_freq[None, :]
    cos, sin = jnp.cos(ang)[:, freq], jnp.sin(ang)[:, freq]
    c = jnp.where(kind[None, :] > 0, cos, 1.0)
    ss = jnp.where(kind[None, :] == 1, -sin, jnp.where(kind[None, :] == 2, sin, 0.0))
    return c, ss


def _permute_blocks(w, blocks, dim):
    perm, _, _ = _rope_layout(dim)
    idx = np.arange(w.shape[1], dtype=np.int32)
    for b in blocks:
        idx[b * LANES:(b + 1) * LANES] = b * LANES + perm
    return w[:, idx]


def _apply_rope(a, c, ss):
    return a * c + pltpu.roll(a, HALF_LANES, 1) * ss


def _proj_kernel(flags_ref, x_ref, w_ref, c_ref, ss_ref, o_ref):
    acc = _dot(x_ref[...], w_ref[...])
    c, ss = c_ref[...], ss_ref[...]
    for s in range(acc.shape[1] // LANES):
        a = acc[:, s * LANES:(s + 1) * LANES]
        o_ref[:, s * LANES:(s + 1) * LANES] = _apply_rope(a, c, ss).astype(o_ref.dtype)


def _proj(xb, w, flags, tables, seq, tm=1024, tn=512):
    m, k = xb.shape
    n = w.shape[1]
    assert m % tm == 0 and n % tn == 0 and seq % tm == 0
    pos_blocks = seq // tm
    identity = (jnp.ones_like(tables[0]), jnp.zeros_like(tables[1]))
    tables = tuple(jnp.stack([ident, tab]) for ident, tab in zip(identity, tables))
    tab_spec = pl.BlockSpec((None, tm, LANES), lambda i, j, f: (f[j], i % pos_blocks, 0))
    return pl.pallas_call(
        _proj_kernel,
        grid_spec=pltpu.PrefetchScalarGridSpec(
            num_scalar_prefetch=1,
            grid=(m // tm, n // tn),
            in_specs=[pl.BlockSpec((tm, k), lambda i, j, f: (i, 0)),
                      pl.BlockSpec((k, tn), lambda i, j, f: (0, j)),
                      tab_spec, tab_spec],
            out_specs=pl.BlockSpec((tm, tn), lambda i, j, f: (i, j))),
        out_shape=jax.ShapeDtypeStruct((m, n), BF16),
        compiler_params=_cparams(("parallel", "arbitrary")),
        name="proj",
    )(flags, xb, w, *tables)


def _layer_norm(z, g, b):
    mu = jnp.mean(z, axis=-1, keepdims=True)
    zc = z - mu
    var = jnp.mean(zc * zc, axis=-1, keepdims=True)
    return zc * lax.rsqrt(var + NORM_EPS) * g + b


def _outproj_ln_kernel(o_ref, w_ref, h_ref, ln_ref, hf_ref, hb_ref):
    half = o_ref.shape[0] // 2
    for s in range(2):
        rows = slice(s * half, (s + 1) * half)
        m = _dot(o_ref[rows, :], w_ref[...])
        y = _layer_norm(DN_ALPHA * h_ref[rows, :] + m, ln_ref[0:1, :], ln_ref[1:2, :])
        hf_ref[rows, :] = y
        hb_ref[rows, :] = y.astype(BF16)


def _outproj_ln(o, w, h, ln, tm=512):
    m, k = o.shape
    d = w.shape[1]
    return pl.pallas_call(
        _outproj_ln_kernel,
        grid=(m // tm,),
        in_specs=[pl.BlockSpec((tm, k), lambda i: (i, 0)),
                  pl.BlockSpec((k, d), lambda i: (0, 0)),
                  pl.BlockSpec((tm, d), lambda i: (i, 0)),
                  pl.BlockSpec((4, d), lambda i: (0, 0))],
        out_specs=[pl.BlockSpec((tm, d), lambda i: (i, 0)),
                   pl.BlockSpec((tm, d), lambda i: (i, 0))],
        out_shape=[jax.ShapeDtypeStruct((m, d), F32), jax.ShapeDtypeStruct((m, d), BF16)],
        compiler_params=_cparams(("parallel",)),
        name="outproj_ln",
    )(o, w, h, ln)


def _causal_block(q, k, v, c, tq):
    s = _dot_nt(q, k)
    rows = q.shape[0]
    assert tq & (tq - 1) == 0
    pos_in_block = jnp.bitwise_and(lax.broadcasted_iota(jnp.int32, (rows, tq), 0), tq - 1)
    causal = lax.broadcasted_iota(jnp.int32, (rows, tq), 1) <= pos_in_block
    s_diag = jnp.where(causal, s[:, c * tq:], -jnp.inf)
    m = jnp.max(s_diag, axis=-1, keepdims=True)
    if c > 0:
        s_off = s[:, :c * tq]
        m = jnp.maximum(m, jnp.max(s_off, axis=-1, keepdims=True))
        e = jnp.concatenate([jnp.exp(s_off - m), jnp.exp(s_diag - m)], axis=1)
    else:
        e = jnp.exp(s_diag - m)
    inv = 1.0 / jnp.sum(e, axis=-1, keepdims=True)
    return _dot(e.astype(BF16), v) * inv


def _attn_a_kernel(lam_ref, q_ref, k_ref, v_ref, g_ref, o_ref, *, tq, seq, out_scale):
    for c in range(seq // tq):
        kv = (c + 1) * tq
        q = q_ref[0, c * tq:kv, :] * jnp.asarray(A_QK_DIM ** -0.5, BF16)
        lane = lax.broadcasted_iota(jnp.int32, q.shape, 1)
        zero = jnp.zeros_like(q)
        first = jnp.bitwise_and(lane, HALF_LANES // (LANES // A_QK_DIM)) == 0
        q12 = jnp.concatenate([jnp.where(first, q, zero), jnp.where(first, zero, q)], axis=0)
        both = _causal_block(q12, k_ref[0, :kv, :], v_ref[0, :kv, :], c, tq)
        o = both[:tq] - lam_ref[0] * both[tq:]
        o = o * lax.rsqrt(jnp.mean(o * o, axis=-1, keepdims=True) + NORM_EPS) * (g_ref[...] * out_scale)
        o_ref[0, c * tq:kv, :] = o.astype(o_ref.dtype)


def _attn_a(proj, lam_full, subln, batch, seq, lam_init, tq=256):
    h = A_HEADS
    return pl.pallas_call(
        functools.partial(_attn_a_kernel, tq=tq, seq=seq, out_scale=1.0 - lam_init),
        grid_spec=pltpu.PrefetchScalarGridSpec(
            num_scalar_prefetch=1,
            grid=(batch, h),
            in_specs=[pl.BlockSpec((1, seq, HEAD_DIM), lambda b, hh, s: (b, 0, hh)),
                      pl.BlockSpec((1, seq, HEAD_DIM), lambda b, hh, s: (b, 0, h + hh)),
                      pl.BlockSpec((1, seq, HEAD_DIM), lambda b, hh, s: (b, 0, 2 * h + hh)),
                      pl.BlockSpec((1, HEAD_DIM), lambda b, hh, s: (0, 0))],
            out_specs=pl.BlockSpec((1, seq, HEAD_DIM), lambda b, hh, s: (b, 0, hh))),
        out_shape=jax.ShapeDtypeStruct((batch, seq, h * HEAD_DIM), BF16),
        compiler_params=_cparams(("parallel", "parallel")),
        name="attn_a",
    )(lam_full, proj, proj, proj, subln)


def _mixer_a(hb, h, w_in, lam, subln, w_out, ln, layer_idx, batch, seq):
    pos = jnp.arange(seq)
    tables = _rope_tables(pos, A_QK_DIM)
    n_q = A_HEADS * HEAD_DIM
    tn = 512
    flags = (jnp.arange(3 * n_q // tn) < 2 * n_q // tn).astype(jnp.int32)
    wb = _permute_blocks(w_in, range(2 * A_HEADS), A_QK_DIM).astype(BF16)
    proj = _proj(hb, wb, flags, tables, seq, tn=tn)
    lam_init = 0.8 - 0.6 * math.exp(-0.3 * layer_idx)
    lf = lam.astype(F32)
    lam_full = jnp.exp(jnp.sum(lf[0] * lf[1])) - jnp.exp(jnp.sum(lf[2] * lf[3])) + lam_init
    o = _attn_a(proj.reshape(batch, seq, -1), lam_full.reshape(1), subln.reshape(1, HEAD_DIM), batch, seq, lam_init)
    return _outproj_ln(o.reshape(batch * seq, -1), w_out.astype(BF16), h, ln)


DIL_BLOCK = 128


def _band_block(q, k, v, q0, k0, span):
    s = _dot_nt(q, k)
    diff = (q0 + lax.broadcasted_iota(jnp.int32, s.shape, 0)) - (k0 + lax.broadcasted_iota(jnp.int32, s.shape, 1))
    s = jnp.where((diff >= 0) & (diff <= span), s, -jnp.inf)
    m = jnp.max(s, axis=-1, keepdims=True)
    e = jnp.exp(s - m)
    den = jnp.sum(e, axis=-1, keepdims=True)
    return _dot((e * (1.0 / den)).astype(BF16), v), m + jnp.log(den)


def _dil_kernel(*refs, seq):
    n_grp = len(C_PATTERNS)
    qkv = refs[:3 * n_grp]
    o_ref = refs[3 * n_grp]
    stage, o_scr, l_scr = refs[3 * n_grp + 1:]
    blk = DIL_BLOCK
    for g, (window, dil) in enumerate(C_PATTERNS):
        q_ref, k_ref, v_ref = qkv[3 * g:3 * g + 3]
        length = seq // dil
        span = window // dil
        n_blk = length // blk
        if dil > 1:
            for j, ref in enumerate((q_ref, k_ref, v_ref)):
                stage[j] = ref[0].astype(F32)

        def rows(j, ref, r, start, size):
            if dil == 1:
                return ref[0, start:start + size, :]
            return stage[j, pl.ds(r + start * dil, size, stride=dil), :].astype(BF16)

        for r in range(dil):
            for i in range(n_blk):
                k0 = max(i - 1, 0) * blk
                kl = (i + 1) * blk - k0
                o, lse = _band_block(rows(0, q_ref, r, i * blk, blk), rows(1, k_ref, r, k0, kl),
                                     rows(2, v_ref, r, k0, kl), i * blk, k0, span)
                lse = jnp.broadcast_to(lse, (blk, HEAD_DIM))
                if dil == 1:
                    o_scr[g, i * blk:(i + 1) * blk, :] = o
                    l_scr[g, i * blk:(i + 1) * blk, :] = lse
                else:
                    o_scr[g, pl.ds(r + i * blk * dil, blk, stride=dil), :] = o
                    l_scr[g, pl.ds(r + i * blk * dil, blk, stride=dil), :] = lse

    chunk = 256
    for c in range(seq // chunk):
        sl = slice(c * chunk, (c + 1) * chunk)
        ls = [l_scr[g, sl, :] for g in range(n_grp)]
        m = functools.reduce(jnp.maximum, ls)
        es = [jnp.exp(l - m) for l in ls]
        inv = 1.0 / functools.reduce(lambda a, b: a + b, es)
        o = functools.reduce(lambda a, b: a + b, [(es[g] * inv) * o_scr[g, sl, :] for g in range(n_grp)])
        o_ref[0, sl, :] = o.astype(o_ref.dtype)


def _dil_attn(proj, batch, seq):
    hg, p = C_HEADS_PER_GROUP, len(C_PATTERNS)

    def spec(kind, g):
        return pl.BlockSpec((1, seq, HEAD_DIM), lambda b, hh: (b, 0, kind * p * hg + g * hg + hh))

    in_specs = [spec(kind, g) for g in range(p) for kind in range(3)]
    return pl.pallas_call(
        functools.partial(_dil_kernel, seq=seq),
        grid=(batch, hg),
        in_specs=in_specs,
        out_specs=pl.BlockSpec((1, seq, HEAD_DIM), lambda b, hh: (b, 0, hh)),
        out_shape=jax.ShapeDtypeStruct((batch, seq, hg * HEAD_DIM), BF16),
        scratch_shapes=[pltpu.VMEM((3, seq, HEAD_DIM), F32), pltpu.VMEM((p, seq, HEAD_DIM), F32),
                        pltpu.VMEM((p, seq, HEAD_DIM), F32)],
        compiler_params=_cparams(("parallel", "parallel")),
        name="dil_attn",
    )(*([proj] * len(in_specs)))


def _mixer_c(hb, h, w_in, w_out, ln, batch, seq):
    tables = _rope_tables(jnp.arange(seq), HEAD_DIM)
    tn = 512
    n = w_in.shape[1]
    flags = (jnp.arange(n // tn) < 2 * (n // 3) // tn).astype(jnp.int32)
    wb = _scaled_bf16(_permute_blocks(w_in, range(2 * n // 3 // LANES), HEAD_DIM), n // 3, HEAD_DIM ** -0.5)
    proj = _proj(hb, wb, flags, tables, seq, tn=tn)
    o = _dil_attn(proj.reshape(batch, seq, n), batch, seq)
    return _outproj_ln(o.reshape(batch * seq, -1), w_out.astype(BF16), h, ln)


def _gates_kernel(x_ref, w_ref, o_ref):
    z = _dot(x_ref[...], w_ref[...])
    o_ref[...] = 1.0 / (1.0 + jnp.exp(-z))


def _gates(xb, wg, tm=1024):
    m, k = xb.shape
    n = wg.shape[1]
    return pl.pallas_call(
        _gates_kernel,
        grid=(m // tm,),
        in_specs=[pl.BlockSpec((tm, k), lambda i: (i, 0)), pl.BlockSpec((k, n), lambda i: (0, 0))],
        out_specs=pl.BlockSpec((tm, n), lambda i: (i, 0)),
        out_shape=jax.ShapeDtypeStruct((m, n), F32),
        compiler_params=_cparams(("parallel",)),
        name="nsa_gates",
    )(xb, wg)


def _gelu_tanh(x):
    return 0.5 * x * (1.0 + jnp.tanh(math.sqrt(2.0 / math.pi) * (x + 0.044715 * (x * x * x))))


def _compress_kernel(a_ref, pos_ref, w1_ref, w2_ref, c_ref, ss_ref, o_ref, stage):
    kind = pl.program_id(2)
    stage[...] = a_ref[0].astype(F32)
    n_chunk = stage.shape[0] // B_CMP_STRIDE
    first = jnp.zeros((n_chunk, HEAD_DIM), F32)
    second = jnp.zeros((n_chunk, HEAD_DIM), F32)
    for t in range(B_CMP_STRIDE):
        a = stage[pl.ds(t, n_chunk, stride=B_CMP_STRIDE), :]
        u = B_CMP_STRIDE + t
        first = first + _dot((a + pos_ref[0, t:t + 1, :]).astype(BF16), w1_ref[0, t * HEAD_DIM:(t + 1) * HEAD_DIM, :])
        second = second + _dot((a + pos_ref[0, u:u + 1, :]).astype(BF16), w1_ref[0, u * HEAD_DIM:(u + 1) * HEAD_DIM, :])
    hmid = first + pltpu.roll(second, n_chunk - 1, 0)
    out = _dot(_gelu_tanh(hmid).astype(BF16), w2_ref[0])

    @pl.when(kind == 0)
    def _():
        o_ref[0, 0] = _apply_rope(out, c_ref[...], ss_ref[...]).astype(o_ref.dtype)

    @pl.when(kind != 0)
    def _():
        o_ref[0, 0] = out.astype(o_ref.dtype)


def _compress(proj3, cmp_pos, cmp_w1, cmp_w2, batch, seq):
    g = B_KV_GROUPS
    n_chunk = seq // B_CMP_STRIDE
    cmp_end = jnp.arange(n_chunk) * B_CMP_STRIDE + B_CMP_LEN - 1
    tables = _rope_tables(cmp_end, HEAD_DIM)
    tab_spec = pl.BlockSpec((n_chunk, LANES), lambda b, gg, kk: (0, 0))
    perm, _, _ = _rope_layout(HEAD_DIM)
    cmp_w2 = jnp.stack([cmp_w2[0][:, perm], cmp_w2[1]])
    return pl.pallas_call(
        _compress_kernel,
        grid=(batch, g, 2),
        in_specs=[pl.BlockSpec((1, seq, HEAD_DIM), lambda b, gg, kk: (b, 0, B_HEADS + kk * g + gg)),
                  pl.BlockSpec((1, B_CMP_LEN, HEAD_DIM), lambda b, gg, kk: (kk, 0, 0)),
                  pl.BlockSpec((1, B_CMP_LEN * HEAD_DIM, HEAD_DIM), lambda b, gg, kk: (kk, 0, 0)),
                  pl.BlockSpec((1, HEAD_DIM, HEAD_DIM), lambda b, gg, kk: (kk, 0, 0)),
                  tab_spec, tab_spec],
        out_specs=pl.BlockSpec((1, 1, n_chunk, HEAD_DIM), lambda b, gg, kk: (kk * batch + b, gg, 0, 0)),
        out_shape=jax.ShapeDtypeStruct((2 * batch, g, n_chunk, HEAD_DIM), BF16),
        scratch_shapes=[pltpu.VMEM((seq, HEAD_DIM), F32)],
        compiler_params=_cparams(("parallel", "parallel", "arbitrary")),
        name="nsa_compress",
    )(proj3, cmp_pos, cmp_w1.astype(BF16), cmp_w2.astype(BF16), *tables)


def _split3(x):
    hi = x.astype(BF16)
    r = x - hi.astype(F32)
    mid = r.astype(BF16)
    lo = (r - mid.astype(F32)).astype(BF16)
    return hi, mid, lo


def _nsa_kernel(q_ref, kc_ref, vc_ref, ks_ref, vs_ref, kw_ref, vw_ref, gate_ref, ex_ref, o_ref,
                vct_scr, ocmp_scr, *, tq, seq, n_cmp, n_sel):
    rep = B_REP
    big = 1e30
    vct_scr[...] = vc_ref[0, 0].astype(F32).T.astype(BF16)
    kc = kc_ref[0, 0]
    lane = lax.broadcasted_iota(jnp.int32, (tq, LANES), 1)
    n_back = -(-(B_WINDOW - 1) // tq)

    for c in range(seq // tq):
        lo, hi = c * tq, (c + 1) * tq

        qpos = lo + lax.broadcasted_iota(jnp.int32, (1, tq), 1)
        crow = lax.broadcasted_iota(jnp.int32, (LANES, tq), 0)
        cmask = (crow * B_CMP_STRIDE + (B_CMP_LEN - 1) <= qpos) & (crow < n_cmp)

        p_sum = jnp.zeros((LANES, tq), F32)
        for r in range(rep):
            s = jnp.where(cmask, _dot_nt(kc, q_ref[0, lo:hi, r * HEAD_DIM:(r + 1) * HEAD_DIM]), NEG_FILL)
            e = jnp.exp(s - jnp.max(s, axis=0, keepdims=True))
            p = jnp.where(cmask, e * (1.0 / jnp.sum(e, axis=0, keepdims=True)), 0.0)
            ocmp_scr[r] = _dot(vct_scr[...], p.astype(BF16)).T
            p_sum = p_sum + p

        ni = lax.broadcasted_iota(jnp.int32, (LANES, LANES), 0)
        ci = lax.broadcasted_iota(jnp.int32, (LANES, LANES), 1)
        per = B_SEL_LEN // B_CMP_STRIDE
        cover = ((ci >= per * ni - (B_CMP_LEN // B_CMP_STRIDE - 1)) & (ci < per * ni + per) & (ci < n_cmp) & (ni < n_sel))
        cover = jnp.where(cover, 1.0, 0.0).astype(BF16)
        importance = sum(_dot(cover, part) for part in _split3(p_sum))[:n_sel]
        nrow = lax.broadcasted_iota(jnp.int32, (n_sel, tq), 0)
        cur = jnp.right_shift(qpos, int(math.log2(B_SEL_LEN)))
        forced = (nrow == 0) | (nrow == cur) | (nrow == cur - 1)
        score = jnp.where(forced, big, importance)
        score = jnp.where(nrow <= cur, score, -big)
        rank = jnp.zeros((n_sel, tq), jnp.int32)
        for mm in range(min(n_sel, (hi - 1) // B_SEL_LEN + 1)):
            other = score[mm:mm + 1, :]
            ahead = (other > score) | ((other == score) & (nrow > mm))
            rank = rank + jnp.where(ahead, 1, 0)
        sel_bias = jnp.where(rank < min(B_SEL_TOPN, n_sel), 0.0, -big)
        sel_bias = jnp.concatenate([sel_bias, jnp.zeros((LANES - n_sel, tq), F32)], axis=0).T.astype(BF16)

        k_aug = jnp.concatenate([ks_ref[0, :hi, :], ex_ref[:hi, :]], axis=1)
        v_sel = vs_ref[0, :hi, :]
        wlo = max(c - n_back, 0) * tq
        k_win, v_win = kw_ref[0, wlo:hi, :], vw_ref[0, wlo:hi, :]
        pair_rows = 2 * tq
        qrow = lo + jnp.bitwise_and(lax.broadcasted_iota(jnp.int32, (pair_rows, hi - wlo), 0), tq - 1)
        diff = qrow - (wlo + lax.broadcasted_iota(jnp.int32, (pair_rows, hi - wlo), 1))
        wmask = (diff >= 0) & (diff <= B_WINDOW - 1)
        gates = gate_ref[0, lo:hi, :]
        bias2 = jnp.concatenate([sel_bias, sel_bias], axis=0)

        def head_pair(pr, carry):
            q2 = q_ref[0, lo:hi, pl.ds(pl.multiple_of(pr * 2 * HEAD_DIM, 2 * HEAD_DIM), 2 * HEAD_DIM)]
            q = jnp.concatenate([q2[:, :HEAD_DIM], q2[:, HEAD_DIM:]], axis=0)
            o_sel = _causal_block(jnp.concatenate([q, bias2], axis=1), k_aug, v_sel, c, tq)
            s = jnp.where(wmask, _dot_nt(q, k_win), -jnp.inf)
            e = jnp.exp(s - jnp.max(s, axis=-1, keepdims=True))
            o_win = _dot(e.astype(BF16), v_win) * (1.0 / jnp.sum(e, axis=-1, keepdims=True))
            outs = []
            for j in range(2):
                r = pr * 2 + j

                def gate(b):
                    return jnp.sum(jnp.where(lane == r * 3 + b, gates, 0.0), axis=-1, keepdims=True)

                rows = slice(j * tq, (j + 1) * tq)
                outs.append(gate(0) * ocmp_scr[r] + gate(1) * o_sel[rows] + gate(2) * o_win[rows])
            o_ref[0, lo:hi, pl.ds(pl.multiple_of(pr * 2 * HEAD_DIM, 2 * HEAD_DIM), 2 * HEAD_DIM)] = (
                jnp.concatenate(outs, axis=1).astype(o_ref.dtype))
            return carry

        lax.fori_loop(0, rep // 2, head_pair, 0)


def _nsa_attn(proj3, kvc, gates, batch, seq, tq=256):
    g, rep = B_KV_GROUPS, B_REP
    n_chunk = seq // B_CMP_STRIDE
    n_cmp = n_chunk - B_CMP_LEN // B_CMP_STRIDE + 1
    n_sel = seq // B_SEL_LEN
    assert n_chunk == LANES and n_sel <= LANES and n_sel % 8 == 0
    onehot = (jnp.arange(seq)[:, None] // B_SEL_LEN == jnp.arange(LANES)[None, :]).astype(BF16)

    def kv_spec(j):
        return pl.BlockSpec((1, seq, HEAD_DIM), lambda b, gg: (b, 0, B_HEADS + j * g + gg))

    return pl.pallas_call(
        functools.partial(_nsa_kernel, tq=tq, seq=seq, n_cmp=n_cmp, n_sel=n_sel),
        grid=(batch, g),
        in_specs=[pl.BlockSpec((1, seq, rep * HEAD_DIM), lambda b, gg: (b, 0, gg)),
                  pl.BlockSpec((1, 1, n_chunk, HEAD_DIM), lambda b, gg: (b, gg, 0, 0)),
                  pl.BlockSpec((1, 1, n_chunk, HEAD_DIM), lambda b, gg: (batch + b, gg, 0, 0)),
                  kv_spec(2), kv_spec(3), kv_spec(4), kv_spec(5),
                  pl.BlockSpec((1, seq, LANES), lambda b, gg: (b, 0, gg)),
                  pl.BlockSpec((seq, LANES), lambda b, gg: (0, 0))],
        out_specs=pl.BlockSpec((1, seq, rep * HEAD_DIM), lambda b, gg: (b, 0, gg)),
        out_shape=jax.ShapeDtypeStruct((batch, seq, B_HEADS * HEAD_DIM), BF16),
        scratch_shapes=[pltpu.VMEM((HEAD_DIM, n_chunk), BF16), pltpu.VMEM((rep, tq, HEAD_DIM), F32)],
        compiler_params=_cparams(("parallel", "parallel")),
        name="nsa_attn",
    )(proj3, kvc, kvc, proj3, proj3, proj3, proj3, gates.reshape(batch, seq, g * LANES), onehot)


def _scaled_bf16(w, n_cols, scale):
    col_scale = jnp.where(jnp.arange(w.shape[1]) < n_cols, scale, 1.0).astype(F32)
    return (w * col_scale[None, :]).astype(BF16)


def _mixer_b(hb, h, w_in, cmp_pos, cmp_w1, cmp_w2, w_out, ln, batch, seq):
    g, rep = B_KV_GROUPS, B_REP
    n_main = (B_HEADS + B_N_KV * g) * HEAD_DIM
    tn = 512
    per = g * HEAD_DIM // tn
    nq = B_HEADS * HEAD_DIM // tn
    jb = jnp.arange(n_main // tn)
    flags = ((jb < nq) | ((jb >= nq + 2 * per) & (jb < nq + 3 * per))
             | ((jb >= nq + 4 * per) & (jb < nq + 5 * per))).astype(jnp.int32)
    tables = _rope_tables(jnp.arange(seq), HEAD_DIM)
    roped = (list(range(B_HEADS)) + list(range(B_HEADS + 2 * g, B_HEADS + 3 * g))
             + list(range(B_HEADS + 4 * g, B_HEADS + 5 * g)))
    wb = _scaled_bf16(_permute_blocks(w_in, roped, HEAD_DIM), B_HEADS * HEAD_DIM, HEAD_DIM ** -0.5)
    proj3 = _proj(hb, wb[:, :n_main], flags, tables, seq, tn=tn).reshape(batch, seq, n_main)
    wg = wb[:, n_main:].reshape(-1, g, rep * 3)
    wg = jnp.pad(wg, ((0, 0), (0, 0), (0, LANES - rep * 3))).reshape(-1, g * LANES)
    gates = _gates(hb, wg)
    kvc = _compress(proj3, cmp_pos, cmp_w1, cmp_w2, batch, seq)
    o = _nsa_attn(proj3, kvc, gates, batch, seq)
    return _outproj_ln(o.reshape(batch * seq, -1), w_out.astype(BF16), h, ln)


def _router_kernel(x_ref, w_ref, b_ref, o_ref):
    x = x_ref[...]
    xh = x.astype(BF16)
    xl = (x - xh.astype(F32)).astype(BF16)
    hi = _dot(xh, w_ref[...])
    logits = hi[:, :LANES] + (hi[:, LANES:] + _dot(xl, w_ref[:, :LANES])) + b_ref[...]
    lane = lax.broadcasted_iota(jnp.int32, logits.shape, 1)
    lane_f = lane.astype(F32)

    def first_max(vals):
        top = jnp.max(vals, axis=-1, keepdims=True)
        first = jnp.min(jnp.where(vals == top, lane_f, float(4 * LANES)), axis=-1, keepdims=True)
        return top, first.astype(jnp.int32)

    gl = jnp.where(lane < MOE_GROUPS, logits, -jnp.inf)
    gmax, gidx = first_max(gl)
    g_w = 1.0 / jnp.sum(jnp.exp(gl - gmax), axis=-1, keepdims=True)
    lo = MOE_GROUPS + MOE_EXPERTS_PER_GROUP * gidx
    ev = jnp.where((lane >= lo) & (lane < lo + MOE_EXPERTS_PER_GROUP), logits, -jnp.inf)
    v1, i1 = first_max(ev)
    v2, i2 = first_max(jnp.where(lane == i1, -jnp.inf, ev))
    e2 = jnp.exp(v2 - v1)
    w1 = g_w / (1.0 + e2)
    w2 = g_w * e2 / (1.0 + e2)
    out = jnp.where(lane == 0, (i1 - MOE_GROUPS).astype(F32),
                    jnp.where(lane == 1, (i2 - MOE_GROUPS).astype(F32),
                              jnp.where(lane == 2, w1, jnp.where(lane == 3, w2, 0.0))))
    o_ref[...] = out


def _router(h, router_w, router_b, tm=1024):
    m, d = h.shape
    n = router_w.shape[1]
    wp = jnp.pad(router_w, ((0, 0), (0, LANES - n)))
    wh = wp.astype(BF16)
    wp = jnp.concatenate([wh, (wp - wh.astype(F32)).astype(BF16)], axis=1)
    bp = jnp.pad(router_b, (0, LANES - n)).reshape(1, LANES)
    return pl.pallas_call(
        _router_kernel,
        grid=(m // tm,),
        in_specs=[pl.BlockSpec((tm, d), lambda i: (i, 0)),
                  pl.BlockSpec((d, 2 * LANES), lambda i: (0, 0)),
                  pl.BlockSpec((1, LANES), lambda i: (0, 0))],
        out_specs=pl.BlockSpec((tm, LANES), lambda i: (i, 0)),
        out_shape=jax.ShapeDtypeStruct((m, LANES), F32),
        compiler_params=_cparams(("parallel",)),
        name="moe_router",
    )(h, wp, bp)


def _experts_kernel(be_ref, nu_ref, x_ref, wi_ref, wo_ref, *rest, blk0):
    y_ref, wi_scr, wo_scr = rest[-3:]
    i = pl.program_id(0)
    blk = i + blk0

    @pl.when((i == 0) | (be_ref[blk] != be_ref[jnp.maximum(blk - 1, 0)]))
    def _():
        wi_scr[...] = wi_ref[0].astype(BF16)
        wo_scr[...] = wo_ref[0].astype(BF16)

    @pl.when(blk < nu_ref[0])
    def _():
        gu = _dot(x_ref[...], wi_scr[...])
        gate, up = gu[:, :MOE_D_FF], gu[:, MOE_D_FF:]
        act = gate * (1.0 / (1.0 + jnp.exp(-gate))) * up
        y_ref[...] = _dot(act.astype(BF16), wo_scr[...]).astype(y_ref.dtype)

    @pl.when(blk >= nu_ref[0])
    def _():
        y_ref[...] = jnp.zeros_like(y_ref)


def _experts(xb, blk_e, n_used, w_in, w_out, ybuf, blk0):
    rows, d = xb.shape
    n_total = blk_e.shape[0]
    in_specs = [pl.BlockSpec((MOE_ROWS, d), lambda i, be, nu: (i, 0)),
                pl.BlockSpec((1, d, 2 * MOE_D_FF), lambda i, be, nu: (be[i + blk0], 0, 0)),
                pl.BlockSpec((1, MOE_D_FF, d), lambda i, be, nu: (be[i + blk0], 0, 0))]
    args = [blk_e, n_used, xb, w_in, w_out]
    aliases = {}
    if ybuf is not None:
        in_specs.append(pl.BlockSpec(memory_space=pl.ANY))
        args.append(ybuf)
        aliases = {len(args) - 1: 0}
    return pl.pallas_call(
        functools.partial(_experts_kernel, blk0=blk0),
        grid_spec=pltpu.PrefetchScalarGridSpec(
            num_scalar_prefetch=2,
            grid=(rows // MOE_ROWS,),
            in_specs=in_specs,
            out_specs=pl.BlockSpec((MOE_ROWS, d), lambda i, be, nu: (i + blk0, 0)),
            scratch_shapes=[pltpu.VMEM((d, 2 * MOE_D_FF), BF16), pltpu.VMEM((MOE_D_FF, d), BF16)]),
        out_shape=jax.ShapeDtypeStruct((n_total * MOE_ROWS, d), BF16),
        input_output_aliases=aliases,
        compiler_params=_cparams(("arbitrary",)),
        name="moe_experts",
    )(*args)


def _moe_ln_kernel(h_ref, y0_ref, y1_ref, r_ref, ln_ref, hf_ref, hb_ref):
    r = r_ref[...]
    f = y0_ref[...].astype(F32) * r[:, 2:3] + y1_ref[...].astype(F32) * r[:, 3:4]
    y = _layer_norm(DN_ALPHA * h_ref[...] + f, ln_ref[2:3, :], ln_ref[3:4, :])
    hf_ref[...] = y
    hb_ref[...] = y.astype(BF16)


def _moe_ln(h, y0, y1, route, ln, tm=512):
    m, d = h.shape
    row = lambda i: (i, 0)
    return pl.pallas_call(
        _moe_ln_kernel,
        grid=(m // tm,),
        in_specs=[pl.BlockSpec((tm, d), row), pl.BlockSpec((tm, d), row), pl.BlockSpec((tm, d), row),
                  pl.BlockSpec((tm, LANES), row), pl.BlockSpec((4, d), lambda i: (0, 0))],
        out_specs=[pl.BlockSpec((tm, d), row), pl.BlockSpec((tm, d), row)],
        out_shape=[jax.ShapeDtypeStruct((m, d), F32), jax.ShapeDtypeStruct((m, d), BF16)],
        compiler_params=_cparams(("parallel",)),
        name="moe_ln",
    )(h, y0, y1, route, ln)


def _moe(h, hb, router_w, router_b, w_in, w_out, ln):
    t, d = h.shape
    e, rows = MOE_EXPERTS, MOE_ROWS
    route = _router(h, router_w, router_b)
    flat_e = route[:, 0:2].astype(jnp.int32).reshape(-1)
    n_assign = flat_e.shape[0]
    n_blk = -(-n_assign // rows) + e
    onehot = (flat_e[:, None] == jnp.arange(e, dtype=jnp.int32)[None, :]).astype(jnp.int32)
    csum = jnp.cumsum(onehot, axis=0)
    rank = jnp.take_along_axis(csum, flat_e[:, None], axis=1)[:, 0] - 1
    sizes = csum[-1]
    padded = (sizes + rows - 1) // rows * rows
    pad_end = jnp.cumsum(padded)
    pad_start = pad_end - padded
    dest = pad_start[flat_e] + rank
    flat_tok = jnp.arange(n_assign, dtype=jnp.int32) // 2
    slot_tok = jnp.zeros((n_blk * rows,), jnp.int32).at[dest].set(flat_tok)
    blk_start = jnp.arange(n_blk, dtype=jnp.int32) * rows
    blk_e = jnp.minimum(jnp.sum((pad_end[None, :] <= blk_start[:, None]).astype(jnp.int32), axis=1), e - 1)
    n_used = (pad_end[-1] // rows).astype(jnp.int32).reshape(1)
    assert n_blk % MOE_CHUNKS == 0
    per = n_blk // MOE_CHUNKS
    yb = None
    for c in range(MOE_CHUNKS):
        xb = hb[slot_tok[c * per * rows:(c + 1) * per * rows]]
        yb = _experts(xb, blk_e, n_used, w_in, w_out, yb, c * per)
    dest2 = dest.reshape(t, 2)
    return _moe_ln(h, yb[dest2[:, 0]], yb[dest2[:, 1]], route, ln)


def kernel(x, l0_a_w_in, l0_a_lam, l0_a_subln, l0_a_w_out, l0_ln, l0_router_w, l0_router_b, l0_moe_w_in, l0_moe_w_out, l1_b_w_in, l1_b_cmp_pos, l1_b_cmp_w1, l1_b_cmp_w2, l1_b_w_out, l1_ln, l1_router_w, l1_router_b, l1_moe_w_in, l1_moe_w_out, l2_c_w_in, l2_c_w_out, l2_ln, l2_router_w, l2_router_b, l2_moe_w_in, l2_moe_w_out, l3_a_w_in, l3_a_lam, l3_a_subln, l3_a_w_out, l3_ln, l3_router_w, l3_router_b, l3_moe_w_in, l3_moe_w_out):
    batch, seq, d = x.shape
    assert batch % BATCH_STREAMS == 0
    sub = batch // BATCH_STREAMS
    mixers = [
        lambda hb, h: _mixer_a(hb, h, l0_a_w_in, l0_a_lam, l0_a_subln, l0_a_w_out, l0_ln, 0, sub, seq),
        lambda hb, h: _mixer_b(hb, h, l1_b_w_in, l1_b_cmp_pos, l1_b_cmp_w1, l1_b_cmp_w2, l1_b_w_out, l1_ln, sub, seq),
        lambda hb, h: _mixer_c(hb, h, l2_c_w_in, l2_c_w_out, l2_ln, sub, seq),
        lambda hb, h: _mixer_a(hb, h, l3_a_w_in, l3_a_lam, l3_a_subln, l3_a_w_out, l3_ln, 3, sub, seq),
    ]
    moes = [
        lambda h, hb: _moe(h, hb, l0_router_w, l0_router_b, l0_moe_w_in, l0_moe_w_out, l0_ln),
        lambda h, hb: _moe(h, hb, l1_router_w, l1_router_b, l1_moe_w_in, l1_moe_w_out, l1_ln),
        lambda h, hb: _moe(h, hb, l2_router_w, l2_router_b, l2_moe_w_in, l2_moe_w_out, l2_ln),
        lambda h, hb: _moe(h, hb, l3_router_w, l3_router_b, l3_moe_w_in, l3_moe_w_out, l3_ln),
    ]
    streams = []
    for s in range(BATCH_STREAMS):
        h = x[s * sub:(s + 1) * sub].reshape(sub * seq, d)
        streams.append((h, h.astype(BF16)))
    steps = [(layer, kind) for layer in range(DEPTH) for kind in (0, 1)]
    for tick in range(len(steps) + BATCH_STREAMS - 1):
        for s in range(BATCH_STREAMS):
            if 0 <= tick - s < len(steps):
                layer, kind = steps[tick - s]
                h, hb = streams[s]
                streams[s] = mixers[layer](hb, h) if kind == 0 else moes[layer](h, hb)
    return jnp.concatenate([h for h, _ in streams], axis=0).reshape(batch, seq, d)
```

```python
import functools
import math

import jax
import jax.numpy as jnp
import numpy as np
from jax import lax
from jax.experimental import pallas as pl
from jax.experimental.pallas import tpu as pltpu

D_MODEL = 2048
DEPTH = 4
HEAD_DIM = 128
ROPE_THETA = 500000.0
ROPE_FRACTION = 4
NEG_FILL = -1e30

A_HEADS = D_MODEL // HEAD_DIM
A_QK_DIM = HEAD_DIM // 2

B_HEADS = D_MODEL // HEAD_DIM
B_KV_GROUPS = 4
B_REP = B_HEADS // B_KV_GROUPS
B_CMP_LEN = 32
B_CMP_STRIDE = 16
B_SEL_LEN = 64
B_SEL_TOPN = 16
B_WINDOW = 512
B_N_KV = 6

C_PATTERNS = ((128, 1), (512, 4), (2048, 16))
C_HEADS_PER_GROUP = D_MODEL // (2 * HEAD_DIM)

MOE_GROUPS = 4
MOE_EXPERTS_PER_GROUP = 8
MOE_EXPERTS = MOE_GROUPS * MOE_EXPERTS_PER_GROUP
MOE_D_FF = D_MODEL // 4
MOE_ROWS = 256
MOE_CHUNKS = 2

DN_ALPHA = (2 * DEPTH) ** 0.25
NORM_EPS = 1e-5

LANES = 128
VMEM_LIMIT = 56 * 1024 * 1024

BF16 = jnp.bfloat16
F32 = jnp.float32


def _cparams(sem):
    return pltpu.CompilerParams(dimension_semantics=sem, vmem_limit_bytes=VMEM_LIMIT)


def _dot(a, b):
    return jnp.dot(a, b, preferred_element_type=F32)


def _dot_nt(a, b):
    return lax.dot_general(a, b, (((1,), (1,)), ((), ())), preferred_element_type=F32)


HALF_LANES = LANES // 2


def _rope_layout(dim):
    n_sub = LANES // dim
    w = HALF_LANES // n_sub
    half = dim // ROPE_FRACTION // 2
    perm, kind, freq = np.zeros(LANES, np.int32), np.zeros(LANES, np.int32), np.zeros(LANES, np.int32)
    for s in range(n_sub):
        lanes = list(range(s * w, (s + 1) * w)) + list(range(HALF_LANES + s * w, HALF_LANES + (s + 1) * w))
        x1, x2 = lanes[:half], lanes[w:w + half]
        rest = [l for l in lanes if l not in x1 and l not in x2]
        for i, l in enumerate(x1):
            perm[l], kind[l], freq[l] = s * dim + i, 1, i
        for i, l in enumerate(x2):
            perm[l], kind[l], freq[l] = s * dim + half + i, 2, i
        for i, l in enumerate(rest):
            perm[l] = s * dim + 2 * half + i
    return perm, kind, freq


def _rope_tables(pos, dim):
    rot = dim // ROPE_FRACTION
    _, kind, freq = _rope_layout(dim)
    inv_freq = ROPE_THETA ** (-jnp.arange(0, rot, 2, dtype=F32) / rot)
    ang = pos.astype(F32)[:, None] * inv_freq[None, :]
    cos, sin = jnp.cos(ang)[:, freq], jnp.sin(ang)[:, freq]
    c = jnp.where(kind[None, :] > 0, cos, 1.0)
    ss = jnp.where(kind[None, :] == 1, -sin, jnp.where(kind[None, :] == 2, sin, 0.0))
    return c, ss


def _permute_blocks(w, blocks, dim):
    perm, _, _ = _rope_layout(dim)
    idx = np.arange(w.shape[1], dtype=np.int32)
    for b in blocks:
        idx[b * LANES:(b + 1) * LANES] = b * LANES + perm
    return w[:, idx]


def _apply_rope(a, c, ss):
    return a * c + pltpu.roll(a, HALF_LANES, 1) * ss


def _proj_kernel(flags_ref, x_ref, w_ref, c_ref, ss_ref, o_ref):
    acc = _dot(x_ref[...], w_ref[...])
    c, ss = c_ref[...], ss_ref[...]
    for s in range(acc.shape[1] // LANES):
        a = acc[:, s * LANES:(s + 1) * LANES]
        o_ref[:, s * LANES:(s + 1) * LANES] = _apply_rope(a, c, ss).astype(o_ref.dtype)


def _proj(xb, w, flags, tables, seq, tm=1024, tn=512):
    m, k = xb.shape
    n = w.shape[1]
    assert m % tm == 0 and n % tn == 0 and seq % tm == 0
    pos_blocks = seq // tm
    identity = (jnp.ones_like(tables[0]), jnp.zeros_like(tables[1]))
    tables = tuple(jnp.stack([ident, tab]) for ident, tab in zip(identity, tables))
    tab_spec = pl.BlockSpec((None, tm, LANES), lambda i, j, f: (f[j], i % pos_blocks, 0))
    return pl.pallas_call(
        _proj_kernel,
        grid_spec=pltpu.PrefetchScalarGridSpec(
            num_scalar_prefetch=1,
            grid=(m // tm, n // tn),
            in_specs=[pl.BlockSpec((tm, k), lambda i, j, f: (i, 0)),
                      pl.BlockSpec((k, tn), lambda i, j, f: (0, j)),
                      tab_spec, tab_spec],
            out_specs=pl.BlockSpec((tm, tn), lambda i, j, f: (i, j))),
        out_shape=jax.ShapeDtypeStruct((m, n), BF16),
        compiler_params=_cparams(("parallel", "arbitrary")),
        name="proj",
    )(flags, xb, w, *tables)


def _layer_norm(z, g, b):
    mu = jnp.mean(z, axis=-1, keepdims=True)
    zc = z - mu
    var = jnp.mean(zc * zc, axis=-1, keepdims=True)
    return zc * lax.rsqrt(var + NORM_EPS) * g + b


def _outproj_ln_kernel(o_ref, w_ref, h_ref, ln_ref, hf_ref, hb_ref):
    half = o_ref.shape[0] // 2
    for s in range(2):
        rows = slice(s * half, (s + 1) * half)
        m = _dot(o_ref[rows, :], w_ref[...])
        y = _layer_norm(DN_ALPHA * h_ref[rows, :] + m, ln_ref[0:1, :], ln_ref[1:2, :])
        hf_ref[rows, :] = y
        hb_ref[rows, :] = y.astype(BF16)


def _outproj_ln(o, w, h, ln, tm=512):
    m, k = o.shape
    d = w.shape[1]
    return pl.pallas_call(
        _outproj_ln_kernel,
        grid=(m // tm,),
        in_specs=[pl.BlockSpec((tm, k), lambda i: (i, 0)),
                  pl.BlockSpec((k, d), lambda i: (0, 0)),
                  pl.BlockSpec((tm, d), lambda i: (i, 0)),
                  pl.BlockSpec((4, d), lambda i: (0, 0))],
        out_specs=[pl.BlockSpec((tm, d), lambda i: (i, 0)),
                   pl.BlockSpec((tm, d), lambda i: (i, 0))],
        out_shape=[jax.ShapeDtypeStruct((m, d), F32), jax.ShapeDtypeStruct((m, d), BF16)],
        compiler_params=_cparams(("parallel",)),
        name="outproj_ln",
    )(o, w, h, ln)


def _causal_block(q, k, v, c, tq):
    s = _dot_nt(q, k)
    rows = q.shape[0]
    assert tq & (tq - 1) == 0
    pos_in_block = jnp.bitwise_and(lax.broadcasted_iota(jnp.int32, (rows, tq), 0), tq - 1)
    causal = lax.broadcasted_iota(jnp.int32, (rows, tq), 1) <= pos_in_block
    s_diag = jnp.where(causal, s[:, c * tq:], -jnp.inf)
    m = jnp.max(s_diag, axis=-1, keepdims=True)
    if c > 0:
        s_off = s[:, :c * tq]
        m = jnp.maximum(m, jnp.max(s_off, axis=-1, keepdims=True))
        e = jnp.concatenate([jnp.exp(s_off - m), jnp.exp(s_diag - m)], axis=1)
    else:
        e = jnp.exp(s_diag - m)
    inv = 1.0 / jnp.sum(e, axis=-1, keepdims=True)
    return _dot(e.astype(BF16), v) * inv


def _attn_a_kernel(lam_ref, q_ref, k_ref, v_ref, g_ref, o_ref, *, tq, seq, out_scale):
    for c in range(seq // tq):
        kv = (c + 1) * tq
        q = q_ref[0, c * tq:kv, :] * jnp.asarray(A_QK_DIM ** -0.5, BF16)
        lane = lax.broadcasted_iota(jnp.int32, q.shape, 1)
        zero = jnp.zeros_like(q)
        first = jnp.bitwise_and(lane, HALF_LANES // (LANES // A_QK_DIM)) == 0
        q12 = jnp.concatenate([jnp.where(first, q, zero), jnp.where(first, zero, q)], axis=0)
        both = _causal_block(q12, k_ref[0, :kv, :], v_ref[0, :kv, :], c, tq)
        o = both[:tq] - lam_ref[0] * both[tq:]
        o = o * lax.rsqrt(jnp.mean(o * o, axis=-1, keepdims=True) + NORM_EPS) * (g_ref[...] * out_scale)
        o_ref[0, c * tq:kv, :] = o.astype(o_ref.dtype)


def _attn_a(proj, lam_full, subln, batch, seq, lam_init, tq=256):
    h = A_HEADS
    return pl.pallas_call(
        functools.partial(_attn_a_kernel, tq=tq, seq=seq, out_scale=1.0 - lam_init),
        grid_spec=pltpu.PrefetchScalarGridSpec(
            num_scalar_prefetch=1,
            grid=(batch, h),
            in_specs=[pl.BlockSpec((1, seq, HEAD_DIM), lambda b, hh, s: (b, 0, hh)),
                      pl.BlockSpec((1, seq, HEAD_DIM), lambda b, hh, s: (b, 0, h + hh)),
                      pl.BlockSpec((1, seq, HEAD_DIM), lambda b, hh, s: (b, 0, 2 * h + hh)),
                      pl.BlockSpec((1, HEAD_DIM), lambda b, hh, s: (0, 0))],
            out_specs=pl.BlockSpec((1, seq, HEAD_DIM), lambda b, hh, s: (b, 0, hh))),
        out_shape=jax.ShapeDtypeStruct((batch, seq, h * HEAD_DIM), BF16),
        compiler_params=_cparams(("parallel", "parallel")),
        name="attn_a",
    )(lam_full, proj, proj, proj, subln)


def _mixer_a(hb, h, w_in, lam, subln, w_out, ln, layer_idx, batch, seq):
    pos = jnp.arange(seq)
    tables = _rope_tables(pos, A_QK_DIM)
    n_q = A_HEADS * HEAD_DIM
    tn = 512
    flags = (jnp.arange(3 * n_q // tn) < 2 * n_q // tn).astype(jnp.int32)
    wb = _permute_blocks(w_in, range(2 * A_HEADS), A_QK_DIM).astype(BF16)
    proj = _proj(hb, wb, flags, tables, seq, tn=tn)
    lam_init = 0.8 - 0.6 * math.exp(-0.3 * layer_idx)
    lf = lam.astype(F32)
    lam_full = jnp.exp(jnp.sum(lf[0] * lf[1])) - jnp.exp(jnp.sum(lf[2] * lf[3])) + lam_init
    o = _attn_a(proj.reshape(batch, seq, -1), lam_full.reshape(1), subln.reshape(1, HEAD_DIM), batch, seq, lam_init)
    return _outproj_ln(o.reshape(batch * seq, -1), w_out.astype(BF16), h, ln)


DIL_BLOCK = 128


def _band_block(q, k, v, q0, k0, span):
    s = _dot_nt(q, k)
    diff = (q0 + lax.broadcasted_iota(jnp.int32, s.shape, 0)) - (k0 + lax.broadcasted_iota(jnp.int32, s.shape, 1))
    s = jnp.where((diff >= 0) & (diff <= span), s, -jnp.inf)
    m = jnp.max(s, axis=-1, keepdims=True)
    e = jnp.exp(s - m)
    den = jnp.sum(e, axis=-1, keepdims=True)
    return _dot((e * (1.0 / den)).astype(BF16), v), m + jnp.log(den)


def _dil_kernel(*refs, seq):
    n_grp = len(C_PATTERNS)
    qkv = refs[:3 * n_grp]
    o_ref = refs[3 * n_grp]
    stage, o_scr, l_scr = refs[3 * n_grp + 1:]
    blk = DIL_BLOCK
    for g, (window, dil) in enumerate(C_PATTERNS):
        q_ref, k_ref, v_ref = qkv[3 * g:3 * g + 3]
        length = seq // dil
        span = window // dil
        n_blk = length // blk
        if dil > 1:
            for j, ref in enumerate((q_ref, k_ref, v_ref)):
                stage[j] = ref[0].astype(F32)

        def rows(j, ref, r, start, size):
            if dil == 1:
                return ref[0, start:start + size, :]
            return stage[j, pl.ds(r + start * dil, size, stride=dil), :].astype(BF16)

        for r in range(dil):
            for i in range(n_blk):
                k0 = max(i - 1, 0) * blk
                kl = (i + 1) * blk - k0
                o, lse = _band_block(rows(0, q_ref, r, i * blk, blk), rows(1, k_ref, r, k0, kl),
                                     rows(2, v_ref, r, k0, kl), i * blk, k0, span)
                lse = jnp.broadcast_to(lse, (blk, HEAD_DIM))
                if dil == 1:
                    o_scr[g, i * blk:(i + 1) * blk, :] = o
                    l_scr[g, i * blk:(i + 1) * blk, :] = lse
                else:
                    o_scr[g, pl.ds(r + i * blk * dil, blk, stride=dil), :] = o
                    l_scr[g, pl.ds(r + i * blk * dil, blk, stride=dil), :] = lse

    chunk = 256
    for c in range(seq // chunk):
        sl = slice(c * chunk, (c + 1) * chunk)
        ls = [l_scr[g, sl, :] for g in range(n_grp)]
        m = functools.reduce(jnp.maximum, ls)
        es = [jnp.exp(l - m) for l in ls]
        inv = 1.0 / functools.reduce(lambda a, b: a + b, es)
        o = functools.reduce(lambda a, b: a + b, [(es[g] * inv) * o_scr[g, sl, :] for g in range(n_grp)])
        o_ref[0, sl, :] = o.astype(o_ref.dtype)


def _dil_attn(proj, batch, seq):
    hg, p = C_HEADS_PER_GROUP, len(C_PATTERNS)

    def spec(kind, g):
        return pl.BlockSpec((1, seq, HEAD_DIM), lambda b, hh: (b, 0, kind * p * hg + g * hg + hh))

    in_specs = [spec(kind, g) for g in range(p) for kind in range(3)]
    return pl.pallas_call(
        functools.partial(_dil_kernel, seq=seq),
        grid=(batch, hg),
        in_specs=in_specs,
        out_specs=pl.BlockSpec((1, seq, HEAD_DIM), lambda b, hh: (b, 0, hh)),
        out_shape=jax.ShapeDtypeStruct((batch, seq, hg * HEAD_DIM), BF16),
        scratch_shapes=[pltpu.VMEM((3, seq, HEAD_DIM), F32), pltpu.VMEM((p, seq, HEAD_DIM), F32),
                        pltpu.VMEM((p, seq, HEAD_DIM), F32)],
        compiler_params=_cparams(("parallel", "parallel")),
        name="dil_attn",
    )(*([proj] * len(in_specs)))


def _mixer_c(hb, h, w_in, w_out, ln, batch, seq):
    tables = _rope_tables(jnp.arange(seq), HEAD_DIM)
    tn = 512
    n = w_in.shape[1]
    flags = (jnp.arange(n // tn) < 2 * (n // 3) // tn).astype(jnp.int32)
    wb = _scaled_bf16(_permute_blocks(w_in, range(2 * n // 3 // LANES), HEAD_DIM), n // 3, HEAD_DIM ** -0.5)
    proj = _proj(hb, wb, flags, tables, seq, tn=tn)
    o = _dil_attn(proj.reshape(batch, seq, n), batch, seq)
    return _outproj_ln(o.reshape(batch * seq, -1), w_out.astype(BF16), h, ln)


def _gates_kernel(x_ref, w_ref, o_ref):
    z = _dot(x_ref[...], w_ref[...])
    o_ref[...] = 1.0 / (1.0 + jnp.exp(-z))


def _gates(xb, wg, tm=1024):
    m, k = xb.shape
    n = wg.shape[1]
    return pl.pallas_call(
        _gates_kernel,
        grid=(m // tm,),
        in_specs=[pl.BlockSpec((tm, k), lambda i: (i, 0)), pl.BlockSpec((k, n), lambda i: (0, 0))],
        out_specs=pl.BlockSpec((tm, n), lambda i: (i, 0)),
        out_shape=jax.ShapeDtypeStruct((m, n), F32),
        compiler_params=_cparams(("parallel",)),
        name="nsa_gates",
    )(xb, wg)


def _gelu_tanh(x):
    return 0.5 * x * (1.0 + jnp.tanh(math.sqrt(2.0 / math.pi) * (x + 0.044715 * (x * x * x))))


def _compress_kernel(a_ref, pos_ref, w1_ref, w2_ref, c_ref, ss_ref, o_ref, stage):
    kind = pl.program_id(2)
    stage[...] = a_ref[0].astype(F32)
    n_chunk = stage.shape[0] // B_CMP_STRIDE
    first = jnp.zeros((n_chunk, HEAD_DIM), F32)
    second = jnp.zeros((n_chunk, HEAD_DIM), F32)
    for t in range(B_CMP_STRIDE):
        a = stage[pl.ds(t, n_chunk, stride=B_CMP_STRIDE), :]
        u = B_CMP_STRIDE + t
        first = first + _dot((a + pos_ref[0, t:t + 1, :]).astype(BF16), w1_ref[0, t * HEAD_DIM:(t + 1) * HEAD_DIM, :])
        second = second + _dot((a + pos_ref[0, u:u + 1, :]).astype(BF16), w1_ref[0, u * HEAD_DIM:(u + 1) * HEAD_DIM, :])
    hmid = first + pltpu.roll(second, n_chunk - 1, 0)
    out = _dot(_gelu_tanh(hmid).astype(BF16), w2_ref[0])

    @pl.when(kind == 0)
    def _():
        o_ref[0, 0] = _apply_rope(out, c_ref[...], ss_ref[...]).astype(o_ref.dtype)

    @pl.when(kind != 0)
    def _():
        o_ref[0, 0] = out.astype(o_ref.dtype)


def _compress(proj3, cmp_pos, cmp_w1, cmp_w2, batch, seq):
    g = B_KV_GROUPS
    n_chunk = seq // B_CMP_STRIDE
    cmp_end = jnp.arange(n_chunk) * B_CMP_STRIDE + B_CMP_LEN - 1
    tables = _rope_tables(cmp_end, HEAD_DIM)
    tab_spec = pl.BlockSpec((n_chunk, LANES), lambda b, gg, kk: (0, 0))
    perm, _, _ = _rope_layout(HEAD_DIM)
    cmp_w2 = jnp.stack([cmp_w2[0][:, perm], cmp_w2[1]])
    return pl.pallas_call(
        _compress_kernel,
        grid=(batch, g, 2),
        in_specs=[pl.BlockSpec((1, seq, HEAD_DIM), lambda b, gg, kk: (b, 0, B_HEADS + kk * g + gg)),
                  pl.BlockSpec((1, B_CMP_LEN, HEAD_DIM), lambda b, gg, kk: (kk, 0, 0)),
                  pl.BlockSpec((1, B_CMP_LEN * HEAD_DIM, HEAD_DIM), lambda b, gg, kk: (kk, 0, 0)),
                  pl.BlockSpec((1, HEAD_DIM, HEAD_DIM), lambda b, gg, kk: (kk, 0, 0)),
                  tab_spec, tab_spec],
        out_specs=pl.BlockSpec((1, 1, n_chunk, HEAD_DIM), lambda b, gg, kk: (kk * batch + b, gg, 0, 0)),
        out_shape=jax.ShapeDtypeStruct((2 * batch, g, n_chunk, HEAD_DIM), BF16),
        scratch_shapes=[pltpu.VMEM((seq, HEAD_DIM), F32)],
        compiler_params=_cparams(("parallel", "parallel", "arbitrary")),
        name="nsa_compress",
    )(proj3, cmp_pos, cmp_w1.astype(BF16), cmp_w2.astype(BF16), *tables)


def _split3(x):
    hi = x.astype(BF16)
    r = x - hi.astype(F32)
    mid = r.astype(BF16)
    lo = (r - mid.astype(F32)).astype(BF16)
    return hi, mid, lo


def _nsa_kernel(q_ref, kc_ref, vc_ref, ks_ref, vs_ref, kw_ref, vw_ref, gate_ref, ex_ref, o_ref,
                vct_scr, ocmp_scr, *, tq, seq, n_cmp, n_sel):
    rep = B_REP
    big = 1e30
    vct_scr[...] = vc_ref[0, 0].astype(F32).T.astype(BF16)
    kc = kc_ref[0, 0]
    lane = lax.broadcasted_iota(jnp.int32, (tq, LANES), 1)
    n_back = -(-(B_WINDOW - 1) // tq)
    pair_rows = 2 * tq
    wrow = jnp.bitwise_and(lax.broadcasted_iota(jnp.int32, (pair_rows, tq), 0), tq - 1)
    wcol = lax.broadcasted_iota(jnp.int32, (pair_rows, tq), 1)
    win_bias = []
    for back in range(n_back + 1):
        if back * tq - (tq - 1) >= 0 and back * tq + (tq - 1) <= B_WINDOW - 1:
            win_bias.append(None)
        else:
            diff = back * tq + wrow - wcol
            win_bias.append(jnp.where((diff >= 0) & (diff <= B_WINDOW - 1), 0.0, -jnp.inf))

    for c in range(seq // tq):
        lo, hi = c * tq, (c + 1) * tq

        qpos = lo + lax.broadcasted_iota(jnp.int32, (1, tq), 1)
        crow = lax.broadcasted_iota(jnp.int32, (LANES, tq), 0)
        cmask = (crow * B_CMP_STRIDE + (B_CMP_LEN - 1) <= qpos) & (crow < n_cmp)

        p_sum = jnp.zeros((LANES, tq), F32)
        for r in range(rep):
            s = jnp.where(cmask, _dot_nt(kc, q_ref[0, lo:hi, r * HEAD_DIM:(r + 1) * HEAD_DIM]), NEG_FILL)
            e = jnp.exp(s - jnp.max(s, axis=0, keepdims=True))
            p = jnp.where(cmask, e * (1.0 / jnp.sum(e, axis=0, keepdims=True)), 0.0)
            ocmp_scr[r] = _dot(vct_scr[...], p.astype(BF16)).T
            p_sum = p_sum + p

        ni = lax.broadcasted_iota(jnp.int32, (LANES, LANES), 0)
        ci = lax.broadcasted_iota(jnp.int32, (LANES, LANES), 1)
        per = B_SEL_LEN // B_CMP_STRIDE
        cover = ((ci >= per * ni - (B_CMP_LEN // B_CMP_STRIDE - 1)) & (ci < per * ni + per) & (ci < n_cmp) & (ni < n_sel))
        cover = jnp.where(cover, 1.0, 0.0).astype(BF16)
        importance = sum(_dot(cover, part) for part in _split3(p_sum))[:n_sel]
        nrow = lax.broadcasted_iota(jnp.int32, (n_sel, tq), 0)
        cur = jnp.right_shift(qpos, int(math.log2(B_SEL_LEN)))
        forced = (nrow == 0) | (nrow == cur) | (nrow == cur - 1)
        score = jnp.where(forced, big, importance)
        score = jnp.where(nrow <= cur, score, -big)
        rank = jnp.zeros((n_sel, tq), jnp.int32)
        for mm in range(min(n_sel, (hi - 1) // B_SEL_LEN + 1)):
            other = score[mm:mm + 1, :]
            ahead = (other > score) | ((other == score) & (nrow > mm))
            rank = rank + jnp.where(ahead, 1, 0)
        sel_bias = jnp.where(rank < min(B_SEL_TOPN, n_sel), 0.0, -big)
        sel_bias = jnp.concatenate([sel_bias, jnp.zeros((LANES - n_sel, tq), F32)], axis=0).T.astype(BF16)

        k_aug = jnp.concatenate([ks_ref[0, :hi, :], ex_ref[:hi, :]], axis=1)
        v_sel = vs_ref[0, :hi, :]
        wlo = max(c - n_back, 0) * tq
        k_win, v_win = kw_ref[0, wlo:hi, :], vw_ref[0, wlo:hi, :]
        n_wchunks = (hi - wlo) // tq
        gates = gate_ref[0, lo:hi, :]
        bias2 = jnp.concatenate([sel_bias, sel_bias], axis=0)

        def head_pair(pr, carry):
            q2 = q_ref[0, lo:hi, pl.ds(pl.multiple_of(pr * 2 * HEAD_DIM, 2 * HEAD_DIM), 2 * HEAD_DIM)]
            q = jnp.concatenate([q2[:, :HEAD_DIM], q2[:, HEAD_DIM:]], axis=0)
            o_sel = _causal_block(jnp.concatenate([q, bias2], axis=1), k_aug, v_sel, c, tq)
            s = _dot_nt(q, k_win)
            pieces = []
            for j in range(n_wchunks):
                piece, bias = s[:, j * tq:(j + 1) * tq], win_bias[n_wchunks - 1 - j]
                pieces.append(piece if bias is None else piece + bias)
            s = jnp.concatenate(pieces, axis=1)
            e = jnp.exp(s - jnp.max(s, axis=-1, keepdims=True))
            o_win = _dot(e.astype(BF16), v_win) * (1.0 / jnp.sum(e, axis=-1, keepdims=True))
            outs = []
            for j in range(2):
                r = pr * 2 + j

                def gate(b):
                    return jnp.sum(jnp.where(lane == r * 3 + b, gates, 0.0), axis=-1, keepdims=True)

                rows = slice(j * tq, (j + 1) * tq)
                outs.append(gate(0) * ocmp_scr[r] + gate(1) * o_sel[rows] + gate(2) * o_win[rows])
            o_ref[0, lo:hi, pl.ds(pl.multiple_of(pr * 2 * HEAD_DIM, 2 * HEAD_DIM), 2 * HEAD_DIM)] = (
                jnp.concatenate(outs, axis=1).astype(o_ref.dtype))
            return carry

        lax.fori_loop(0, rep // 2, head_pair, 0)


def _nsa_attn(proj3, kvc, gates, batch, seq, tq=256):
    g, rep = B_KV_GROUPS, B_REP
    n_chunk = seq // B_CMP_STRIDE
    n_cmp = n_chunk - B_CMP_LEN // B_CMP_STRIDE + 1
    n_sel = seq // B_SEL_LEN
    assert n_chunk == LANES and n_sel <= LANES and n_sel % 8 == 0
    onehot = (jnp.arange(seq)[:, None] // B_SEL_LEN == jnp.arange(LANES)[None, :]).astype(BF16)

    def kv_spec(j):
        return pl.BlockSpec((1, seq, HEAD_DIM), lambda b, gg: (b, 0, B_HEADS + j * g + gg))

    return pl.pallas_call(
        functools.partial(_nsa_kernel, tq=tq, seq=seq, n_cmp=n_cmp, n_sel=n_sel),
        grid=(batch, g),
        in_specs=[pl.BlockSpec((1, seq, rep * HEAD_DIM), lambda b, gg: (b, 0, gg)),
                  pl.BlockSpec((1, 1, n_chunk, HEAD_DIM), lambda b, gg: (b, gg, 0, 0)),
                  pl.BlockSpec((1, 1, n_chunk, HEAD_DIM), lambda b, gg: (batch + b, gg, 0, 0)),
                  kv_spec(2), kv_spec(3), kv_spec(4), kv_spec(5),
                  pl.BlockSpec((1, seq, LANES), lambda b, gg: (b, 0, gg)),
                  pl.BlockSpec((seq, LANES), lambda b, gg: (0, 0))],
        out_specs=pl.BlockSpec((1, seq, rep * HEAD_DIM), lambda b, gg: (b, 0, gg)),
        out_shape=jax.ShapeDtypeStruct((batch, seq, B_HEADS * HEAD_DIM), BF16),
        scratch_shapes=[pltpu.VMEM((HEAD_DIM, n_chunk), BF16), pltpu.VMEM((rep, tq, HEAD_DIM), F32)],
        compiler_params=_cparams(("parallel", "parallel")),
        name="nsa_attn",
    )(proj3, kvc, kvc, proj3, proj3, proj3, proj3, gates.reshape(batch, seq, g * LANES), onehot)


def _scaled_bf16(w, n_cols, scale):
    col_scale = jnp.where(jnp.arange(w.shape[1]) < n_cols, scale, 1.0).astype(F32)
    return (w * col_scale[None, :]).astype(BF16)


def _mixer_b(hb, h, w_in, cmp_pos, cmp_w1, cmp_w2, w_out, ln, batch, seq):
    g, rep = B_KV_GROUPS, B_REP
    n_main = (B_HEADS + B_N_KV * g) * HEAD_DIM
    tn = 512
    per = g * HEAD_DIM // tn
    nq = B_HEADS * HEAD_DIM // tn
    jb = jnp.arange(n_main // tn)
    flags = ((jb < nq) | ((jb >= nq + 2 * per) & (jb < nq + 3 * per))
             | ((jb >= nq + 4 * per) & (jb < nq + 5 * per))).astype(jnp.int32)
    tables = _rope_tables(jnp.arange(seq), HEAD_DIM)
    roped = (list(range(B_HEADS)) + list(range(B_HEADS + 2 * g, B_HEADS + 3 * g))
             + list(range(B_HEADS + 4 * g, B_HEADS + 5 * g)))
    wb = _scaled_bf16(_permute_blocks(w_in, roped, HEAD_DIM), B_HEADS * HEAD_DIM, HEAD_DIM ** -0.5)
    proj3 = _proj(hb, wb[:, :n_main], flags, tables, seq, tn=tn).reshape(batch, seq, n_main)
    wg = wb[:, n_main:].reshape(-1, g, rep * 3)
    wg = jnp.pad(wg, ((0, 0), (0, 0), (0, LANES - rep * 3))).reshape(-1, g * LANES)
    gates = _gates(hb, wg)
    kvc = _compress(proj3, cmp_pos, cmp_w1, cmp_w2, batch, seq)
    o = _nsa_attn(proj3, kvc, gates, batch, seq)
    return _outproj_ln(o.reshape(batch * seq, -1), w_out.astype(BF16), h, ln)


def _router_kernel(x_ref, w_ref, b_ref, o_ref):
    x = x_ref[...]
    xh = x.astype(BF16)
    xl = (x - xh.astype(F32)).astype(BF16)
    hi = _dot(xh, w_ref[...])
    logits = hi[:, :LANES] + (hi[:, LANES:] + _dot(xl, w_ref[:, :LANES])) + b_ref[...]
    lane = lax.broadcasted_iota(jnp.int32, logits.shape, 1)
    lane_f = lane.astype(F32)

    def first_max(vals):
        top = jnp.max(vals, axis=-1, keepdims=True)
        first = jnp.min(jnp.where(vals == top, lane_f, float(4 * LANES)), axis=-1, keepdims=True)
        return top, first.astype(jnp.int32)

    gl = jnp.where(lane < MOE_GROUPS, logits, -jnp.inf)
    gmax, gidx = first_max(gl)
    g_w = 1.0 / jnp.sum(jnp.exp(gl - gmax), axis=-1, keepdims=True)
    lo = MOE_GROUPS + MOE_EXPERTS_PER_GROUP * gidx
    ev = jnp.where((lane >= lo) & (lane < lo + MOE_EXPERTS_PER_GROUP), logits, -jnp.inf)
    v1, i1 = first_max(ev)
    v2, i2 = first_max(jnp.where(lane == i1, -jnp.inf, ev))
    e2 = jnp.exp(v2 - v1)
    w1 = g_w / (1.0 + e2)
    w2 = g_w * e2 / (1.0 + e2)
    out = jnp.where(lane == 0, (i1 - MOE_GROUPS).astype(F32),
                    jnp.where(lane == 1, (i2 - MOE_GROUPS).astype(F32),
                              jnp.where(lane == 2, w1, jnp.where(lane == 3, w2, 0.0))))
    o_ref[...] = out


def _router(h, router_w, router_b, tm=1024):
    m, d = h.shape
    n = router_w.shape[1]
    wp = jnp.pad(router_w, ((0, 0), (0, LANES - n)))
    wh = wp.astype(BF16)
    wp = jnp.concatenate([wh, (wp - wh.astype(F32)).astype(BF16)], axis=1)
    bp = jnp.pad(router_b, (0, LANES - n)).reshape(1, LANES)
    return pl.pallas_call(
        _router_kernel,
        grid=(m // tm,),
        in_specs=[pl.BlockSpec((tm, d), lambda i: (i, 0)),
                  pl.BlockSpec((d, 2 * LANES), lambda i: (0, 0)),
                  pl.BlockSpec((1, LANES), lambda i: (0, 0))],
        out_specs=pl.BlockSpec((tm, LANES), lambda i: (i, 0)),
        out_shape=jax.ShapeDtypeStruct((m, LANES), F32),
        compiler_params=_cparams(("parallel",)),
        name="moe_router",
    )(h, wp, bp)


def _experts_kernel(be_ref, nu_ref, x_ref, wi_ref, wo_ref, *rest, blk0):
    y_ref, wi_scr, wo_scr = rest[-3:]
    i = pl.program_id(0)
    blk = i + blk0

    @pl.when((i == 0) | (be_ref[blk] != be_ref[jnp.maximum(blk - 1, 0)]))
    def _():
        wi_scr[...] = wi_ref[0].astype(BF16)
        wo_scr[...] = wo_ref[0].astype(BF16)

    @pl.when(blk < nu_ref[0])
    def _():
        gu = _dot(x_ref[...], wi_scr[...])
        gate, up = gu[:, :MOE_D_FF], gu[:, MOE_D_FF:]
        act = gate * (1.0 / (1.0 + jnp.exp(-gate))) * up
        y_ref[...] = _dot(act.astype(BF16), wo_scr[...]).astype(y_ref.dtype)

    @pl.when(blk >= nu_ref[0])
    def _():
        y_ref[...] = jnp.zeros_like(y_ref)


def _experts(xb, blk_e, n_used, w_in, w_out, ybuf, blk0):
    rows, d = xb.shape
    n_total = blk_e.shape[0]
    in_specs = [pl.BlockSpec((MOE_ROWS, d), lambda i, be, nu: (i, 0)),
                pl.BlockSpec((1, d, 2 * MOE_D_FF), lambda i, be, nu: (be[i + blk0], 0, 0)),
                pl.BlockSpec((1, MOE_D_FF, d), lambda i, be, nu: (be[i + blk0], 0, 0))]
    args = [blk_e, n_used, xb, w_in, w_out]
    aliases = {}
    if ybuf is not None:
        in_specs.append(pl.BlockSpec(memory_space=pl.ANY))
        args.append(ybuf)
        aliases = {len(args) - 1: 0}
    return pl.pallas_call(
        functools.partial(_experts_kernel, blk0=blk0),
        grid_spec=pltpu.PrefetchScalarGridSpec(
            num_scalar_prefetch=2,
            grid=(rows // MOE_ROWS,),
            in_specs=in_specs,
            out_specs=pl.BlockSpec((MOE_ROWS, d), lambda i, be, nu: (i + blk0, 0)),
            scratch_shapes=[pltpu.VMEM((d, 2 * MOE_D_FF), BF16), pltpu.VMEM((MOE_D_FF, d), BF16)]),
        out_shape=jax.ShapeDtypeStruct((n_total * MOE_ROWS, d), BF16),
        input_output_aliases=aliases,
        compiler_params=_cparams(("arbitrary",)),
        name="moe_experts",
    )(*args)


def _moe_ln_kernel(h_ref, y0_ref, y1_ref, r_ref, ln_ref, hf_ref, hb_ref):
    r = r_ref[...]
    f = y0_ref[...].astype(F32) * r[:, 2:3] + y1_ref[...].astype(F32) * r[:, 3:4]
    y = _layer_norm(DN_ALPHA * h_ref[...] + f, ln_ref[2:3, :], ln_ref[3:4, :])
    hf_ref[...] = y
    hb_ref[...] = y.astype(BF16)


def _moe_ln(h, y0, y1, route, ln, tm=512):
    m, d = h.shape
    row = lambda i: (i, 0)
    return pl.pallas_call(
        _moe_ln_kernel,
        grid=(m // tm,),
        in_specs=[pl.BlockSpec((tm, d), row), pl.BlockSpec((tm, d), row), pl.BlockSpec((tm, d), row),
                  pl.BlockSpec((tm, LANES), row), pl.BlockSpec((4, d), lambda i: (0, 0))],
        out_specs=[pl.BlockSpec((tm, d), row), pl.BlockSpec((tm, d), row)],
        out_shape=[jax.ShapeDtypeStruct((m, d), F32), jax.ShapeDtypeStruct((m, d), BF16)],
        compiler_params=_cparams(("parallel",)),
        name="moe_ln",
    )(h, y0, y1, route, ln)


def _moe(h, hb, router_w, router_b, w_in, w_out, ln):
    t, d = h.shape
    e, rows = MOE_EXPERTS, MOE_ROWS
    route = _router(h, router_w, router_b)
    flat_e = route[:, 0:2].astype(jnp.int32).reshape(-1)
    n_assign = flat_e.shape[0]
    n_blk = -(-n_assign // rows) + e
    onehot = (flat_e[:, None] == jnp.arange(e, dtype=jnp.int32)[None, :]).astype(jnp.int32)
    csum = jnp.cumsum(onehot, axis=0)
    rank = jnp.take_along_axis(csum, flat_e[:, None], axis=1)[:, 0] - 1
    sizes = csum[-1]
    padded = (sizes + rows - 1) // rows * rows
    pad_end = jnp.cumsum(padded)
    pad_start = pad_end - padded
    dest = pad_start[flat_e] + rank
    flat_tok = jnp.arange(n_assign, dtype=jnp.int32) // 2
    slot_tok = jnp.zeros((n_blk * rows,), jnp.int32).at[dest].set(flat_tok)
    blk_start = jnp.arange(n_blk, dtype=jnp.int32) * rows
    blk_e = jnp.minimum(jnp.sum((pad_end[None, :] <= blk_start[:, None]).astype(jnp.int32), axis=1), e - 1)
    n_used = (pad_end[-1] // rows).astype(jnp.int32).reshape(1)
    assert n_blk % MOE_CHUNKS == 0
    per = n_blk // MOE_CHUNKS
    yb = None
    for c in range(MOE_CHUNKS):
        xb = hb[slot_tok[c * per * rows:(c + 1) * per * rows]]
        yb = _experts(xb, blk_e, n_used, w_in, w_out, yb, c * per)
    dest2 = dest.reshape(t, 2)
    return _moe_ln(h, yb[dest2[:, 0]], yb[dest2[:, 1]], route, ln)


def kernel(x, l0_a_w_in, l0_a_lam, l0_a_subln, l0_a_w_out, l0_ln, l0_router_w, l0_router_b, l0_moe_w_in, l0_moe_w_out, l1_b_w_in, l1_b_cmp_pos, l1_b_cmp_w1, l1_b_cmp_w2, l1_b_w_out, l1_ln, l1_router_w, l1_router_b, l1_moe_w_in, l1_moe_w_out, l2_c_w_in, l2_c_w_out, l2_ln, l2_router_w, l2_router_b, l2_moe_w_in, l2_moe_w_out, l3_a_w_in, l3_a_lam, l3_a_subln, l3_a_w_out, l3_ln, l3_router_w, l3_router_b, l3_moe_w_in, l3_moe_w_out):
    batch, seq, d = x.shape
    h = x.reshape(batch * seq, d)
    hb = h.astype(BF16)
    h, hb = _mixer_a(hb, h, l0_a_w_in, l0_a_lam, l0_a_subln, l0_a_w_out, l0_ln, 0, batch, seq)
    h, hb = _moe(h, hb, l0_router_w, l0_router_b, l0_moe_w_in, l0_moe_w_out, l0_ln)
    h, hb = _mixer_b(hb, h, l1_b_w_in, l1_b_cmp_pos, l1_b_cmp_w1, l1_b_cmp_w2, l1_b_w_out, l1_ln, batch, seq)
    h, hb = _moe(h, hb, l1_router_w, l1_router_b, l1_moe_w_in, l1_moe_w_out, l1_ln)
    h, hb = _mixer_c(hb, h, l2_c_w_in, l2_c_w_out, l2_ln, batch, seq)
    h, hb = _moe(h, hb, l2_router_w, l2_router_b, l2_moe_w_in, l2_moe_w_out, l2_ln)
    h, hb = _mixer_a(hb, h, l3_a_w_in, l3_a_lam, l3_a_subln, l3_a_w_out, l3_ln, 3, batch, seq)
    h, hb = _moe(h, hb, l3_router_w, l3_router_b, l3_moe_w_in, l3_moe_w_out, l3_ln)
    return h.reshape(batch, seq, d)
```

```python
import functools
import math

import jax
import jax.numpy as jnp
import numpy as np
from jax import lax
from jax.experimental import pallas as pl
from jax.experimental.pallas import tpu as pltpu

D_MODEL = 2048
DEPTH = 4
HEAD_DIM = 128
ROPE_THETA = 500000.0
ROPE_FRACTION = 4
NEG_FILL = -1e30

A_HEADS = D_MODEL // HEAD_DIM
A_QK_DIM = HEAD_DIM // 2

B_HEADS = D_MODEL // HEAD_DIM
B_KV_GROUPS = 4
B_REP = B_HEADS // B_KV_GROUPS
B_CMP_LEN = 32
B_CMP_STRIDE = 16
B_SEL_LEN = 64
B_SEL_TOPN = 16
B_WINDOW = 512
B_N_KV = 6

C_PATTERNS = ((128, 1), (512, 4), (2048, 16))
C_HEADS_PER_GROUP = D_MODEL // (2 * HEAD_DIM)

MOE_GROUPS = 4
MOE_EXPERTS_PER_GROUP = 8
MOE_EXPERTS = MOE_GROUPS * MOE_EXPERTS_PER_GROUP
MOE_D_FF = D_MODEL // 4
MOE_ROWS = 256
MOE_CHUNKS = 2

DN_ALPHA = (2 * DEPTH) ** 0.25
NORM_EPS = 1e-5

LANES = 128
VMEM_LIMIT = 56 * 1024 * 1024

BF16 = jnp.bfloat16
F32 = jnp.float32


def _cparams(sem):
    return pltpu.CompilerParams(dimension_semantics=sem, vmem_limit_bytes=VMEM_LIMIT)


def _dot(a, b):
    return jnp.dot(a, b, preferred_element_type=F32)


def _dot_nt(a, b):
    return lax.dot_general(a, b, (((1,), (1,)), ((), ())), preferred_element_type=F32)


HALF_LANES = LANES // 2


def _rope_layout(dim):
    n_sub = LANES // dim
    w = HALF_LANES // n_sub
    half = dim // ROPE_FRACTION // 2
    perm, kind, freq = np.zeros(LANES, np.int32), np.zeros(LANES, np.int32), np.zeros(LANES, np.int32)
    for s in range(n_sub):
        lanes = list(range(s * w, (s + 1) * w)) + list(range(HALF_LANES + s * w, HALF_LANES + (s + 1) * w))
        x1, x2 = lanes[:half], lanes[w:w + half]
        rest = [l for l in lanes if l not in x1 and l not in x2]
        for i, l in enumerate(x1):
            perm[l], kind[l], freq[l] = s * dim + i, 1, i
        for i, l in enumerate(x2):
            perm[l], kind[l], freq[l] = s * dim + half + i, 2, i
        for i, l in enumerate(rest):
            perm[l] = s * dim + 2 * half + i
    return perm, kind, freq


def _rope_tables(pos, dim):
    rot = dim // ROPE_FRACTION
    _, kind, freq = _rope_layout(dim)
    inv_freq = ROPE_THETA ** (-jnp.arange(0, rot, 2, dtype=F32) / rot)
    ang = pos.astype(F32)[:, None] * inv_freq[None, :]
    cos, sin = jnp.cos(ang)[:, freq], jnp.sin(ang)[:, freq]
    c = jnp.where(kind[None, :] > 0, cos, 1.0)
    ss = jnp.where(kind[None, :] == 1, -sin, jnp.where(kind[None, :] == 2, sin, 0.0))
    return c, ss


def _permute_blocks(w, blocks, dim):
    perm, _, _ = _rope_layout(dim)
    idx = np.arange(w.shape[1], dtype=np.int32)
    for b in blocks:
        idx[b * LANES:(b + 1) * LANES] = b * LANES + perm
    return w[:, idx]


def _apply_rope(a, c, ss):
    return a * c + pltpu.roll(a, HALF_LANES, 1) * ss


def _roped(x, c_ref, ss_ref, rows=slice(None)):
    return _apply_rope(x.astype(F32), c_ref[rows, :], ss_ref[rows, :]).astype(BF16)


def _proj_kernel(x_ref, w_ref, o_ref):
    o_ref[...] = _dot(x_ref[...], w_ref[...]).astype(o_ref.dtype)


def _proj(xb, w, tm=1024, tn=512):
    m, k = xb.shape
    n = w.shape[1]
    assert m % tm == 0 and n % tn == 0
    return pl.pallas_call(
        _proj_kernel,
        grid=(m // tm, n // tn),
        in_specs=[pl.BlockSpec((tm, k), lambda i, j: (i, 0)),
                  pl.BlockSpec((k, tn), lambda i, j: (0, j))],
        out_specs=pl.BlockSpec((tm, tn), lambda i, j: (i, j)),
        out_shape=jax.ShapeDtypeStruct((m, n), BF16),
        compiler_params=_cparams(("parallel", "arbitrary")),
        name="proj",
    )(xb, w)


def _layer_norm(z, g, b):
    mu = jnp.mean(z, axis=-1, keepdims=True)
    zc = z - mu
    var = jnp.mean(zc * zc, axis=-1, keepdims=True)
    return zc * lax.rsqrt(var + NORM_EPS) * g + b


def _outproj_ln_kernel(o_ref, w_ref, h_ref, ln_ref, hf_ref, hb_ref):
    half = o_ref.shape[0] // 2
    for s in range(2):
        rows = slice(s * half, (s + 1) * half)
        m = _dot(o_ref[rows, :], w_ref[...])
        y = _layer_norm(DN_ALPHA * h_ref[rows, :] + m, ln_ref[0:1, :], ln_ref[1:2, :])
        hf_ref[rows, :] = y
        hb_ref[rows, :] = y.astype(BF16)


def _outproj_ln(o, w, h, ln, tm=512):
    m, k = o.shape
    d = w.shape[1]
    return pl.pallas_call(
        _outproj_ln_kernel,
        grid=(m // tm,),
        in_specs=[pl.BlockSpec((tm, k), lambda i: (i, 0)),
                  pl.BlockSpec((k, d), lambda i: (0, 0)),
                  pl.BlockSpec((tm, d), lambda i: (i, 0)),
                  pl.BlockSpec((4, d), lambda i: (0, 0))],
        out_specs=[pl.BlockSpec((tm, d), lambda i: (i, 0)),
                   pl.BlockSpec((tm, d), lambda i: (i, 0))],
        out_shape=[jax.ShapeDtypeStruct((m, d), F32), jax.ShapeDtypeStruct((m, d), BF16)],
        compiler_params=_cparams(("parallel",)),
        name="outproj_ln",
    )(o, w, h, ln)


def _causal_block(q, k, v, c, tq):
    s = _dot_nt(q, k)
    rows = q.shape[0]
    assert tq & (tq - 1) == 0
    pos_in_block = jnp.bitwise_and(lax.broadcasted_iota(jnp.int32, (rows, tq), 0), tq - 1)
    causal = lax.broadcasted_iota(jnp.int32, (rows, tq), 1) <= pos_in_block
    s_diag = jnp.where(causal, s[:, c * tq:], -jnp.inf)
    m = jnp.max(s_diag, axis=-1, keepdims=True)
    if c > 0:
        s_off = s[:, :c * tq]
        m = jnp.maximum(m, jnp.max(s_off, axis=-1, keepdims=True))
        e = jnp.concatenate([jnp.exp(s_off - m), jnp.exp(s_diag - m)], axis=1)
    else:
        e = jnp.exp(s_diag - m)
    inv = 1.0 / jnp.sum(e, axis=-1, keepdims=True)
    return _dot(e.astype(BF16), v) * inv


def _attn_a_kernel(lam_ref, q_ref, k_ref, v_ref, g_ref, c_ref, ss_ref, o_ref, k_scr, *, tq, seq, out_scale):
    k_scr[...] = _roped(k_ref[0], c_ref, ss_ref)
    for c in range(seq // tq):
        kv = (c + 1) * tq
        q = _roped(q_ref[0, c * tq:kv, :], c_ref, ss_ref, slice(c * tq, kv))
        q = q * jnp.asarray(A_QK_DIM ** -0.5, BF16)
        lane = lax.broadcasted_iota(jnp.int32, q.shape, 1)
        zero = jnp.zeros_like(q)
        first = jnp.bitwise_and(lane, HALF_LANES // (LANES // A_QK_DIM)) == 0
        q12 = jnp.concatenate([jnp.where(first, q, zero), jnp.where(first, zero, q)], axis=0)
        both = _causal_block(q12, k_scr[:kv, :], v_ref[0, :kv, :], c, tq)
        o = both[:tq] - lam_ref[0] * both[tq:]
        o = o * lax.rsqrt(jnp.mean(o * o, axis=-1, keepdims=True) + NORM_EPS) * (g_ref[...] * out_scale)
        o_ref[0, c * tq:kv, :] = o.astype(o_ref.dtype)


def _attn_a(proj, lam_full, subln, tables, batch, seq, lam_init, tq=256):
    h = A_HEADS
    tab_spec = pl.BlockSpec((seq, LANES), lambda b, hh, s: (0, 0))
    return pl.pallas_call(
        functools.partial(_attn_a_kernel, tq=tq, seq=seq, out_scale=1.0 - lam_init),
        grid_spec=pltpu.PrefetchScalarGridSpec(
            num_scalar_prefetch=1,
            grid=(batch, h),
            in_specs=[pl.BlockSpec((1, seq, HEAD_DIM), lambda b, hh, s: (b, 0, hh)),
                      pl.BlockSpec((1, seq, HEAD_DIM), lambda b, hh, s: (b, 0, h + hh)),
                      pl.BlockSpec((1, seq, HEAD_DIM), lambda b, hh, s: (b, 0, 2 * h + hh)),
                      pl.BlockSpec((1, HEAD_DIM), lambda b, hh, s: (0, 0)),
                      tab_spec, tab_spec],
            out_specs=pl.BlockSpec((1, seq, HEAD_DIM), lambda b, hh, s: (b, 0, hh)),
            scratch_shapes=[pltpu.VMEM((seq, HEAD_DIM), BF16)]),
        out_shape=jax.ShapeDtypeStruct((batch, seq, h * HEAD_DIM), BF16),
        compiler_params=_cparams(("parallel", "parallel")),
        name="attn_a",
    )(lam_full, proj, proj, proj, subln, *tables)


def _mixer_a(hb, h, w_in, lam, subln, w_out, ln, layer_idx, batch, seq):
    tables = _rope_tables(jnp.arange(seq), A_QK_DIM)
    wb = _permute_blocks(w_in, range(2 * A_HEADS), A_QK_DIM).astype(BF16)
    proj = _proj(hb, wb)
    lam_init = 0.8 - 0.6 * math.exp(-0.3 * layer_idx)
    lf = lam.astype(F32)
    lam_full = jnp.exp(jnp.sum(lf[0] * lf[1])) - jnp.exp(jnp.sum(lf[2] * lf[3])) + lam_init
    o = _attn_a(proj.reshape(batch, seq, -1), lam_full.reshape(1), subln.reshape(1, HEAD_DIM), tables, batch, seq,
                lam_init)
    return _outproj_ln(o.reshape(batch * seq, -1), w_out.astype(BF16), h, ln)


DIL_BLOCK = 128


def _band_block(q, k, v, q0, k0, span):
    s = _dot_nt(q, k)
    diff = (q0 + lax.broadcasted_iota(jnp.int32, s.shape, 0)) - (k0 + lax.broadcasted_iota(jnp.int32, s.shape, 1))
    s = jnp.where((diff >= 0) & (diff <= span), s, -jnp.inf)
    m = jnp.max(s, axis=-1, keepdims=True)
    e = jnp.exp(s - m)
    den = jnp.sum(e, axis=-1, keepdims=True)
    return _dot((e * (1.0 / den)).astype(BF16), v), m + jnp.log(den)


def _dil_kernel(*refs, seq):
    n_grp = len(C_PATTERNS)
    qkv = refs[:3 * n_grp]
    c_ref, ss_ref, o_ref = refs[3 * n_grp:3 * n_grp + 3]
    stage, o_scr, l_scr = refs[3 * n_grp + 3:]
    blk = DIL_BLOCK
    for g, (window, dil) in enumerate(C_PATTERNS):
        q_ref, k_ref, v_ref = qkv[3 * g:3 * g + 3]
        length = seq // dil
        span = window // dil
        n_blk = length // blk
        stage[0] = _apply_rope(q_ref[0].astype(F32), c_ref[...], ss_ref[...])
        stage[1] = _apply_rope(k_ref[0].astype(F32), c_ref[...], ss_ref[...])
        if dil > 1:
            stage[2] = v_ref[0].astype(F32)

        def rows(j, ref, r, start, size):
            if dil == 1:
                return ref[0, start:start + size, :] if j == 2 else stage[j, start:start + size, :].astype(BF16)
            return stage[j, pl.ds(r + start * dil, size, stride=dil), :].astype(BF16)

        for r in range(dil):
            for i in range(n_blk):
                k0 = max(i - 1, 0) * blk
                kl = (i + 1) * blk - k0
                o, lse = _band_block(rows(0, q_ref, r, i * blk, blk), rows(1, k_ref, r, k0, kl),
                                     rows(2, v_ref, r, k0, kl), i * blk, k0, span)
                lse = jnp.broadcast_to(lse, (blk, HEAD_DIM))
                if dil == 1:
                    o_scr[g, i * blk:(i + 1) * blk, :] = o
                    l_scr[g, i * blk:(i + 1) * blk, :] = lse
                else:
                    o_scr[g, pl.ds(r + i * blk * dil, blk, stride=dil), :] = o
                    l_scr[g, pl.ds(r + i * blk * dil, blk, stride=dil), :] = lse

    chunk = 256
    for c in range(seq // chunk):
        sl = slice(c * chunk, (c + 1) * chunk)
        ls = [l_scr[g, sl, :] for g in range(n_grp)]
        m = functools.reduce(jnp.maximum, ls)
        es = [jnp.exp(l - m) for l in ls]
        inv = 1.0 / functools.reduce(lambda a, b: a + b, es)
        o = functools.reduce(lambda a, b: a + b, [(es[g] * inv) * o_scr[g, sl, :] for g in range(n_grp)])
        o_ref[0, sl, :] = o.astype(o_ref.dtype)


def _dil_attn(proj, tables, batch, seq):
    hg, p = C_HEADS_PER_GROUP, len(C_PATTERNS)

    def spec(kind, g):
        return pl.BlockSpec((1, seq, HEAD_DIM), lambda b, hh: (b, 0, kind * p * hg + g * hg + hh))

    in_specs = [spec(kind, g) for g in range(p) for kind in range(3)]
    tab_spec = pl.BlockSpec((seq, LANES), lambda b, hh: (0, 0))
    return pl.pallas_call(
        functools.partial(_dil_kernel, seq=seq),
        grid=(batch, hg),
        in_specs=in_specs + [tab_spec, tab_spec],
        out_specs=pl.BlockSpec((1, seq, HEAD_DIM), lambda b, hh: (b, 0, hh)),
        out_shape=jax.ShapeDtypeStruct((batch, seq, hg * HEAD_DIM), BF16),
        scratch_shapes=[pltpu.VMEM((3, seq, HEAD_DIM), F32), pltpu.VMEM((p, seq, HEAD_DIM), F32),
                        pltpu.VMEM((p, seq, HEAD_DIM), F32)],
        compiler_params=_cparams(("parallel", "parallel")),
        name="dil_attn",
    )(*([proj] * len(in_specs)), *tables)


def _mixer_c(hb, h, w_in, w_out, ln, batch, seq):
    tables = _rope_tables(jnp.arange(seq), HEAD_DIM)
    n = w_in.shape[1]
    wb = _scaled_bf16(_permute_blocks(w_in, range(2 * n // 3 // LANES), HEAD_DIM), n // 3, HEAD_DIM ** -0.5)
    proj = _proj(hb, wb)
    o = _dil_attn(proj.reshape(batch, seq, n), tables, batch, seq)
    return _outproj_ln(o.reshape(batch * seq, -1), w_out.astype(BF16), h, ln)


def _gates_kernel(x_ref, w_ref, o_ref):
    z = _dot(x_ref[...], w_ref[...])
    o_ref[...] = 1.0 / (1.0 + jnp.exp(-z))


def _gates(xb, wg, tm=1024):
    m, k = xb.shape
    n = wg.shape[1]
    return pl.pallas_call(
        _gates_kernel,
        grid=(m // tm,),
        in_specs=[pl.BlockSpec((tm, k), lambda i: (i, 0)), pl.BlockSpec((k, n), lambda i: (0, 0))],
        out_specs=pl.BlockSpec((tm, n), lambda i: (i, 0)),
        out_shape=jax.ShapeDtypeStruct((m, n), F32),
        compiler_params=_cparams(("parallel",)),
        name="nsa_gates",
    )(xb, wg)


def _gelu_tanh(x):
    return 0.5 * x * (1.0 + jnp.tanh(math.sqrt(2.0 / math.pi) * (x + 0.044715 * (x * x * x))))


def _compress_kernel(a_ref, pos_ref, w1_ref, w2_ref, c_ref, ss_ref, o_ref, stage):
    kind = pl.program_id(2)
    stage[...] = a_ref[0].astype(F32)
    n_chunk = stage.shape[0] // B_CMP_STRIDE
    first = jnp.zeros((n_chunk, HEAD_DIM), F32)
    second = jnp.zeros((n_chunk, HEAD_DIM), F32)
    for t in range(B_CMP_STRIDE):
        a = stage[pl.ds(t, n_chunk, stride=B_CMP_STRIDE), :]
        u = B_CMP_STRIDE + t
        first = first + _dot((a + pos_ref[0, t:t + 1, :]).astype(BF16), w1_ref[0, t * HEAD_DIM:(t + 1) * HEAD_DIM, :])
        second = second + _dot((a + pos_ref[0, u:u + 1, :]).astype(BF16), w1_ref[0, u * HEAD_DIM:(u + 1) * HEAD_DIM, :])
    hmid = first + pltpu.roll(second, n_chunk - 1, 0)
    out = _dot(_gelu_tanh(hmid).astype(BF16), w2_ref[0])

    @pl.when(kind == 0)
    def _():
        o_ref[0, 0] = _apply_rope(out, c_ref[...], ss_ref[...]).astype(o_ref.dtype)

    @pl.when(kind != 0)
    def _():
        o_ref[0, 0] = out.astype(o_ref.dtype)


def _compress(proj3, cmp_pos, cmp_w1, cmp_w2, batch, seq):
    g = B_KV_GROUPS
    n_chunk = seq // B_CMP_STRIDE
    cmp_end = jnp.arange(n_chunk) * B_CMP_STRIDE + B_CMP_LEN - 1
    tables = _rope_tables(cmp_end, HEAD_DIM)
    tab_spec = pl.BlockSpec((n_chunk, LANES), lambda b, gg, kk: (0, 0))
    perm, _, _ = _rope_layout(HEAD_DIM)
    cmp_w2 = jnp.stack([cmp_w2[0][:, perm], cmp_w2[1]])
    return pl.pallas_call(
        _compress_kernel,
        grid=(batch, g, 2),
        in_specs=[pl.BlockSpec((1, seq, HEAD_DIM), lambda b, gg, kk: (b, 0, B_HEADS + kk * g + gg)),
                  pl.BlockSpec((1, B_CMP_LEN, HEAD_DIM), lambda b, gg, kk: (kk, 0, 0)),
                  pl.BlockSpec((1, B_CMP_LEN * HEAD_DIM, HEAD_DIM), lambda b, gg, kk: (kk, 0, 0)),
                  pl.BlockSpec((1, HEAD_DIM, HEAD_DIM), lambda b, gg, kk: (kk, 0, 0)),
                  tab_spec, tab_spec],
        out_specs=pl.BlockSpec((1, 1, n_chunk, HEAD_DIM), lambda b, gg, kk: (kk * batch + b, gg, 0, 0)),
        out_shape=jax.ShapeDtypeStruct((2 * batch, g, n_chunk, HEAD_DIM), BF16),
        scratch_shapes=[pltpu.VMEM((seq, HEAD_DIM), F32)],
        compiler_params=_cparams(("parallel", "parallel", "arbitrary")),
        name="nsa_compress",
    )(proj3, cmp_pos, cmp_w1.astype(BF16), cmp_w2.astype(BF16), *tables)


def _split3(x):
    hi = x.astype(BF16)
    r = x - hi.astype(F32)
    mid = r.astype(BF16)
    lo = (r - mid.astype(F32)).astype(BF16)
    return hi, mid, lo


def _nsa_kernel(q_ref, kc_ref, vc_ref, ks_ref, vs_ref, kw_ref, vw_ref, gate_ref, ex_ref, c_ref, ss_ref, o_ref,
                vct_scr, ocmp_scr, ks_scr, kw_scr, q_scr, *, tq, seq, n_cmp, n_sel):
    rep = B_REP
    big = 1e30
    vct_scr[...] = vc_ref[0, 0].astype(F32).T.astype(BF16)
    ks_scr[...] = _roped(ks_ref[0], c_ref, ss_ref)
    kw_scr[...] = _roped(kw_ref[0], c_ref, ss_ref)
    kc = kc_ref[0, 0]
    lane = lax.broadcasted_iota(jnp.int32, (tq, LANES), 1)
    n_back = -(-(B_WINDOW - 1) // tq)
    pair_rows = 2 * tq
    wrow = jnp.bitwise_and(lax.broadcasted_iota(jnp.int32, (pair_rows, tq), 0), tq - 1)
    wcol = lax.broadcasted_iota(jnp.int32, (pair_rows, tq), 1)
    win_bias = []
    for back in range(n_back + 1):
        if back * tq - (tq - 1) >= 0 and back * tq + (tq - 1) <= B_WINDOW - 1:
            win_bias.append(None)
        else:
            diff = back * tq + wrow - wcol
            win_bias.append(jnp.where((diff >= 0) & (diff <= B_WINDOW - 1), 0.0, -jnp.inf))

    for c in range(seq // tq):
        lo, hi = c * tq, (c + 1) * tq

        qpos = lo + lax.broadcasted_iota(jnp.int32, (1, tq), 1)
        crow = lax.broadcasted_iota(jnp.int32, (LANES, tq), 0)
        cmask = (crow * B_CMP_STRIDE + (B_CMP_LEN - 1) <= qpos) & (crow < n_cmp)

        p_sum = jnp.zeros((LANES, tq), F32)
        for r in range(rep):
            cols = slice(r * HEAD_DIM, (r + 1) * HEAD_DIM)
            q_scr[:, cols] = _roped(q_ref[0, lo:hi, cols], c_ref, ss_ref, slice(lo, hi))
            s = jnp.where(cmask, _dot_nt(kc, q_scr[:, cols]), NEG_FILL)
            e = jnp.exp(s - jnp.max(s, axis=0, keepdims=True))
            p = jnp.where(cmask, e * (1.0 / jnp.sum(e, axis=0, keepdims=True)), 0.0)
            ocmp_scr[r] = _dot(vct_scr[...], p.astype(BF16)).T
            p_sum = p_sum + p

        ni = lax.broadcasted_iota(jnp.int32, (LANES, LANES), 0)
        ci = lax.broadcasted_iota(jnp.int32, (LANES, LANES), 1)
        per = B_SEL_LEN // B_CMP_STRIDE
        cover = ((ci >= per * ni - (B_CMP_LEN // B_CMP_STRIDE - 1)) & (ci < per * ni + per) & (ci < n_cmp) & (ni < n_sel))
        cover = jnp.where(cover, 1.0, 0.0).astype(BF16)
        importance = sum(_dot(cover, part) for part in _split3(p_sum))[:n_sel]
        nrow = lax.broadcasted_iota(jnp.int32, (n_sel, tq), 0)
        cur = jnp.right_shift(qpos, int(math.log2(B_SEL_LEN)))
        forced = (nrow == 0) | (nrow == cur) | (nrow == cur - 1)
        score = jnp.where(forced, big, importance)
        score = jnp.where(nrow <= cur, score, -big)
        rank = jnp.zeros((n_sel, tq), jnp.int32)
        for mm in range(min(n_sel, (hi - 1) // B_SEL_LEN + 1)):
            other = score[mm:mm + 1, :]
            ahead = (other > score) | ((other == score) & (nrow > mm))
            rank = rank + jnp.where(ahead, 1, 0)
        sel_bias = jnp.where(rank < min(B_SEL_TOPN, n_sel), 0.0, -big)
        sel_bias = jnp.concatenate([sel_bias, jnp.zeros((LANES - n_sel, tq), F32)], axis=0).T.astype(BF16)

        k_aug = jnp.concatenate([ks_scr[:hi, :], ex_ref[:hi, :]], axis=1)
        v_sel = vs_ref[0, :hi, :]
        wlo = max(c - n_back, 0) * tq
        k_win, v_win = kw_scr[wlo:hi, :], vw_ref[0, wlo:hi, :]
        n_wchunks = (hi - wlo) // tq
        gates = gate_ref[0, lo:hi, :]
        bias2 = jnp.concatenate([sel_bias, sel_bias], axis=0)

        def head_pair(pr, carry):
            q2 = q_scr[:, pl.ds(pl.multiple_of(pr * 2 * HEAD_DIM, 2 * HEAD_DIM), 2 * HEAD_DIM)]
            q = jnp.concatenate([q2[:, :HEAD_DIM], q2[:, HEAD_DIM:]], axis=0)
            o_sel = _causal_block(jnp.concatenate([q, bias2], axis=1), k_aug, v_sel, c, tq)
            s = _dot_nt(q, k_win)
            pieces = []
            for j in range(n_wchunks):
                piece, bias = s[:, j * tq:(j + 1) * tq], win_bias[n_wchunks - 1 - j]
                pieces.append(piece if bias is None else piece + bias)
            s = jnp.concatenate(pieces, axis=1)
            e = jnp.exp(s - jnp.max(s, axis=-1, keepdims=True))
            o_win = _dot(e.astype(BF16), v_win) * (1.0 / jnp.sum(e, axis=-1, keepdims=True))
            outs = []
            for j in range(2):
                r = pr * 2 + j

                def gate(b):
                    return jnp.sum(jnp.where(lane == r * 3 + b, gates, 0.0), axis=-1, keepdims=True)

                rows = slice(j * tq, (j + 1) * tq)
                outs.append(gate(0) * ocmp_scr[r] + gate(1) * o_sel[rows] + gate(2) * o_win[rows])
            o_ref[0, lo:hi, pl.ds(pl.multiple_of(pr * 2 * HEAD_DIM, 2 * HEAD_DIM), 2 * HEAD_DIM)] = (
                jnp.concatenate(outs, axis=1).astype(o_ref.dtype))
            return carry

        lax.fori_loop(0, rep // 2, head_pair, 0)


def _nsa_attn(proj3, kvc, gates, tables, batch, seq, tq=256):
    g, rep = B_KV_GROUPS, B_REP
    n_chunk = seq // B_CMP_STRIDE
    n_cmp = n_chunk - B_CMP_LEN // B_CMP_STRIDE + 1
    n_sel = seq // B_SEL_LEN
    assert n_chunk == LANES and n_sel <= LANES and n_sel % 8 == 0
    onehot = (jnp.arange(seq)[:, None] // B_SEL_LEN == jnp.arange(LANES)[None, :]).astype(BF16)

    def kv_spec(j):
        return pl.BlockSpec((1, seq, HEAD_DIM), lambda b, gg: (b, 0, B_HEADS + j * g + gg))

    return pl.pallas_call(
        functools.partial(_nsa_kernel, tq=tq, seq=seq, n_cmp=n_cmp, n_sel=n_sel),
        grid=(batch, g),
        in_specs=[pl.BlockSpec((1, seq, rep * HEAD_DIM), lambda b, gg: (b, 0, gg)),
                  pl.BlockSpec((1, 1, n_chunk, HEAD_DIM), lambda b, gg: (b, gg, 0, 0)),
                  pl.BlockSpec((1, 1, n_chunk, HEAD_DIM), lambda b, gg: (batch + b, gg, 0, 0)),
                  kv_spec(2), kv_spec(3), kv_spec(4), kv_spec(5),
                  pl.BlockSpec((1, seq, LANES), lambda b, gg: (b, 0, gg)),
                  pl.BlockSpec((seq, LANES), lambda b, gg: (0, 0)),
                  pl.BlockSpec((seq, LANES), lambda b, gg: (0, 0)),
                  pl.BlockSpec((seq, LANES), lambda b, gg: (0, 0))],
        out_specs=pl.BlockSpec((1, seq, rep * HEAD_DIM), lambda b, gg: (b, 0, gg)),
        out_shape=jax.ShapeDtypeStruct((batch, seq, B_HEADS * HEAD_DIM), BF16),
        scratch_shapes=[pltpu.VMEM((HEAD_DIM, n_chunk), BF16), pltpu.VMEM((rep, tq, HEAD_DIM), F32),
                        pltpu.VMEM((seq, HEAD_DIM), BF16), pltpu.VMEM((seq, HEAD_DIM), BF16),
                        pltpu.VMEM((tq, rep * HEAD_DIM), BF16)],
        compiler_params=_cparams(("parallel", "parallel")),
        name="nsa_attn",
    )(proj3, kvc, kvc, proj3, proj3, proj3, proj3, gates.reshape(batch, seq, g * LANES), onehot, *tables)


def _scaled_bf16(w, n_cols, scale):
    col_scale = jnp.where(jnp.arange(w.shape[1]) < n_cols, scale, 1.0).astype(F32)
    return (w * col_scale[None, :]).astype(BF16)


def _mixer_b(hb, h, w_in, cmp_pos, cmp_w1, cmp_w2, w_out, ln, batch, seq):
    g, rep = B_KV_GROUPS, B_REP
    n_main = (B_HEADS + B_N_KV * g) * HEAD_DIM
    tables = _rope_tables(jnp.arange(seq), HEAD_DIM)
    roped = (list(range(B_HEADS)) + list(range(B_HEADS + 2 * g, B_HEADS + 3 * g))
             + list(range(B_HEADS + 4 * g, B_HEADS + 5 * g)))
    wb = _scaled_bf16(_permute_blocks(w_in, roped, HEAD_DIM), B_HEADS * HEAD_DIM, HEAD_DIM ** -0.5)
    proj3 = _proj(hb, wb[:, :n_main]).reshape(batch, seq, n_main)
    wg = wb[:, n_main:].reshape(-1, g, rep * 3)
    wg = jnp.pad(wg, ((0, 0), (0, 0), (0, LANES - rep * 3))).reshape(-1, g * LANES)
    gates = _gates(hb, wg)
    kvc = _compress(proj3, cmp_pos, cmp_w1, cmp_w2, batch, seq)
    o = _nsa_attn(proj3, kvc, gates, tables, batch, seq)
    return _outproj_ln(o.reshape(batch * seq, -1), w_out.astype(BF16), h, ln)


def _router_kernel(x_ref, w_ref, b_ref, o_ref):
    x = x_ref[...]
    xh = x.astype(BF16)
    xl = (x - xh.astype(F32)).astype(BF16)
    hi = _dot(xh, w_ref[...])
    logits = hi[:, :LANES] + (hi[:, LANES:] + _dot(xl, w_ref[:, :LANES])) + b_ref[...]
    lane = lax.broadcasted_iota(jnp.int32, logits.shape, 1)
    lane_f = lane.astype(F32)

    def first_max(vals):
        top = jnp.max(vals, axis=-1, keepdims=True)
        first = jnp.min(jnp.where(vals == top, lane_f, float(4 * LANES)), axis=-1, keepdims=True)
        return top, first.astype(jnp.int32)

    gl = jnp.where(lane < MOE_GROUPS, logits, -jnp.inf)
    gmax, gidx = first_max(gl)
    g_w = 1.0 / jnp.sum(jnp.exp(gl - gmax), axis=-1, keepdims=True)
    lo = MOE_GROUPS + MOE_EXPERTS_PER_GROUP * gidx
    ev = jnp.where((lane >= lo) & (lane < lo + MOE_EXPERTS_PER_GROUP), logits, -jnp.inf)
    v1, i1 = first_max(ev)
    v2, i2 = first_max(jnp.where(lane == i1, -jnp.inf, ev))
    e2 = jnp.exp(v2 - v1)
    w1 = g_w / (1.0 + e2)
    w2 = g_w * e2 / (1.0 + e2)
    out = jnp.where(lane == 0, (i1 - MOE_GROUPS).astype(F32),
                    jnp.where(lane == 1, (i2 - MOE_GROUPS).astype(F32),
                              jnp.where(lane == 2, w1, jnp.where(lane == 3, w2, 0.0))))
    o_ref[...] = out


def _router(h, router_w, router_b, tm=1024):
    m, d = h.shape
    n = router_w.shape[1]
    wp = jnp.pad(router_w, ((0, 0), (0, LANES - n)))
    wh = wp.astype(BF16)
    wp = jnp.concatenate([wh, (wp - wh.astype(F32)).astype(BF16)], axis=1)
    bp = jnp.pad(router_b, (0, LANES - n)).reshape(1, LANES)
    return pl.pallas_call(
        _router_kernel,
        grid=(m // tm,),
        in_specs=[pl.BlockSpec((tm, d), lambda i: (i, 0)),
                  pl.BlockSpec((d, 2 * LANES), lambda i: (0, 0)),
                  pl.BlockSpec((1, LANES), lambda i: (0, 0))],
        out_specs=pl.BlockSpec((tm, LANES), lambda i: (i, 0)),
        out_shape=jax.ShapeDtypeStruct((m, LANES), F32),
        compiler_params=_cparams(("parallel",)),
        name="moe_router",
    )(h, wp, bp)


def _experts_kernel(be_ref, nu_ref, x_ref, wi_ref, wo_ref, *rest, blk0):
    y_ref, wi_scr, wo_scr = rest[-3:]
    i = pl.program_id(0)
    blk = i + blk0

    @pl.when((i == 0) | (be_ref[blk] != be_ref[jnp.maximum(blk - 1, 0)]))
    def _():
        wi_scr[...] = wi_ref[0].astype(BF16)
        wo_scr[...] = wo_ref[0].astype(BF16)

    @pl.when(blk < nu_ref[0])
    def _():
        gu = _dot(x_ref[...], wi_scr[...])
        gate, up = gu[:, :MOE_D_FF], gu[:, MOE_D_FF:]
        act = gate * (1.0 / (1.0 + jnp.exp(-gate))) * up
        y_ref[...] = _dot(act.astype(BF16), wo_scr[...]).astype(y_ref.dtype)

    @pl.when(blk >= nu_ref[0])
    def _():
        y_ref[...] = jnp.zeros_like(y_ref)


def _experts(xb, blk_e, n_used, w_in, w_out, ybuf, blk0):
    rows, d = xb.shape
    n_total = blk_e.shape[0]
    in_specs = [pl.BlockSpec((MOE_ROWS, d), lambda i, be, nu: (i, 0)),
                pl.BlockSpec((1, d, 2 * MOE_D_FF), lambda i, be, nu: (be[i + blk0], 0, 0)),
                pl.BlockSpec((1, MOE_D_FF, d), lambda i, be, nu: (be[i + blk0], 0, 0))]
    args = [blk_e, n_used, xb, w_in, w_out]
    aliases = {}
    if ybuf is not None:
        in_specs.append(pl.BlockSpec(memory_space=pl.ANY))
        args.append(ybuf)
        aliases = {len(args) - 1: 0}
    return pl.pallas_call(
        functools.partial(_experts_kernel, blk0=blk0),
        grid_spec=pltpu.PrefetchScalarGridSpec(
            num_scalar_prefetch=2,
            grid=(rows // MOE_ROWS,),
            in_specs=in_specs,
            out_specs=pl.BlockSpec((MOE_ROWS, d), lambda i, be, nu: (i + blk0, 0)),
            scratch_shapes=[pltpu.VMEM((d, 2 * MOE_D_FF), BF16), pltpu.VMEM((MOE_D_FF, d), BF16)]),
        out_shape=jax.ShapeDtypeStruct((n_total * MOE_ROWS, d), BF16),
        input_output_aliases=aliases,
        compiler_params=_cparams(("arbitrary",)),
        name="moe_experts",
    )(*args)


def _moe_ln_kernel(h_ref, y0_ref, y1_ref, r_ref, ln_ref, hf_ref, hb_ref):
    r = r_ref[...]
    f = y0_ref[...].astype(F32) * r[:, 2:3] + y1_ref[...].astype(F32) * r[:, 3:4]
    y = _layer_norm(DN_ALPHA * h_ref[...] + f, ln_ref[2:3, :], ln_ref[3:4, :])
    hf_ref[...] = y
    hb_ref[...] = y.astype(BF16)


def _moe_ln(h, y0, y1, route, ln, tm=512):
    m, d = h.shape
    row = lambda i: (i, 0)
    return pl.pallas_call(
        _moe_ln_kernel,
        grid=(m // tm,),
        in_specs=[pl.BlockSpec((tm, d), row), pl.BlockSpec((tm, d), row), pl.BlockSpec((tm, d), row),
                  pl.BlockSpec((tm, LANES), row), pl.BlockSpec((4, d), lambda i: (0, 0))],
        out_specs=[pl.BlockSpec((tm, d), row), pl.BlockSpec((tm, d), row)],
        out_shape=[jax.ShapeDtypeStruct((m, d), F32), jax.ShapeDtypeStruct((m, d), BF16)],
        compiler_params=_cparams(("parallel",)),
        name="moe_ln",
    )(h, y0, y1, route, ln)


def _moe(h, hb, router_w, router_b, w_in, w_out, ln):
    t, d = h.shape
    e, rows = MOE_EXPERTS, MOE_ROWS
    route = _router(h, router_w, router_b)
    flat_e = route[:, 0:2].astype(jnp.int32).reshape(-1)
    n_assign = flat_e.shape[0]
    n_blk = -(-n_assign // rows) + e
    onehot = (flat_e[:, None] == jnp.arange(e, dtype=jnp.int32)[None, :]).astype(jnp.int32)
    csum = jnp.cumsum(onehot, axis=0)
    rank = jnp.take_along_axis(csum, flat_e[:, None], axis=1)[:, 0] - 1
    sizes = csum[-1]
    padded = (sizes + rows - 1) // rows * rows
    pad_end = jnp.cumsum(padded)
    pad_start = pad_end - padded
    dest = pad_start[flat_e] + rank
    flat_tok = jnp.arange(n_assign, dtype=jnp.int32) // 2
    slot_tok = jnp.zeros((n_blk * rows,), jnp.int32).at[dest].set(flat_tok)
    blk_start = jnp.arange(n_blk, dtype=jnp.int32) * rows
    blk_e = jnp.minimum(jnp.sum((pad_end[None, :] <= blk_start[:, None]).astype(jnp.int32), axis=1), e - 1)
    n_used = (pad_end[-1] // rows).astype(jnp.int32).reshape(1)
    assert n_blk % MOE_CHUNKS == 0
    per = n_blk // MOE_CHUNKS
    yb = None
    for c in range(MOE_CHUNKS):
        xb = hb[slot_tok[c * per * rows:(c + 1) * per * rows]]
        yb = _experts(xb, blk_e, n_used, w_in, w_out, yb, c * per)
    dest2 = dest.reshape(t, 2)
    return _moe_ln(h, yb[dest2[:, 0]], yb[dest2[:, 1]], route, ln)


def kernel(x, l0_a_w_in, l0_a_lam, l0_a_subln, l0_a_w_out, l0_ln, l0_router_w, l0_router_b, l0_moe_w_in, l0_moe_w_out, l1_b_w_in, l1_b_cmp_pos, l1_b_cmp_w1, l1_b_cmp_w2, l1_b_w_out, l1_ln, l1_router_w, l1_router_b, l1_moe_w_in, l1_moe_w_out, l2_c_w_in, l2_c_w_out, l2_ln, l2_router_w, l2_router_b, l2_moe_w_in, l2_moe_w_out, l3_a_w_in, l3_a_lam, l3_a_subln, l3_a_w_out, l3_ln, l3_router_w, l3_router_b, l3_moe_w_in, l3_moe_w_out):
    batch, seq, d = x.shape
    h = x.reshape(batch * seq, d)
    hb = h.astype(BF16)
    h, hb = _mixer_a(hb, h, l0_a_w_in, l0_a_lam, l0_a_subln, l0_a_w_out, l0_ln, 0, batch, seq)
    h, hb = _moe(h, hb, l0_router_w, l0_router_b, l0_moe_w_in, l0_moe_w_out, l0_ln)
    h, hb = _mixer_b(hb, h, l1_b_w_in, l1_b_cmp_pos, l1_b_cmp_w1, l1_b_cmp_w2, l1_b_w_out, l1_ln, batch, seq)
    h, hb = _moe(h, hb, l1_router_w, l1_router_b, l1_moe_w_in, l1_moe_w_out, l1_ln)
    h, hb = _mixer_c(hb, h, l2_c_w_in, l2_c_w_out, l2_ln, batch, seq)
    h, hb = _moe(h, hb, l2_router_w, l2_router_b, l2_moe_w_in, l2_moe_w_out, l2_ln)
    h, hb = _mixer_a(hb, h, l3_a_w_in, l3_a_lam, l3_a_subln, l3_a_w_out, l3_ln, 3, batch, seq)
    h, hb = _moe(h, hb, l3_router_w, l3_router_b, l3_moe_w_in, l3_moe_w_out, l3_ln)
    return h.reshape(batch, seq, d)
```

```python
import functools
import math

import jax
import jax.numpy as jnp
import numpy as np
from jax import lax
from jax.experimental import pallas as pl
from jax.experimental.pallas import tpu as pltpu

D_MODEL = 2048
DEPTH = 4
HEAD_DIM = 128
ROPE_THETA = 500000.0
ROPE_FRACTION = 4
NEG_FILL = -1e30

A_HEADS = D_MODEL // HEAD_DIM
A_QK_DIM = HEAD_DIM // 2

B_HEADS = D_MODEL // HEAD_DIM
B_KV_GROUPS = 4
B_REP = B_HEADS // B_KV_GROUPS
B_CMP_LEN = 32
B_CMP_STRIDE = 16
B_SEL_LEN = 64
B_SEL_TOPN = 16
B_WINDOW = 512
B_N_KV = 6

C_PATTERNS = ((128, 1), (512, 4), (2048, 16))
C_HEADS_PER_GROUP = D_MODEL // (2 * HEAD_DIM)

MOE_GROUPS = 4
MOE_EXPERTS_PER_GROUP = 8
MOE_EXPERTS = MOE_GROUPS * MOE_EXPERTS_PER_GROUP
MOE_D_FF = D_MODEL // 4
MOE_ROWS = 256
MOE_CHUNKS = 2

DN_ALPHA = (2 * DEPTH) ** 0.25
NORM_EPS = 1e-5

LANES = 128
VMEM_LIMIT = 56 * 1024 * 1024

BF16 = jnp.bfloat16
F32 = jnp.float32


def _cparams(sem):
    return pltpu.CompilerParams(dimension_semantics=sem, vmem_limit_bytes=VMEM_LIMIT)


def _dot(a, b):
    return jnp.dot(a, b, preferred_element_type=F32)


def _dot_nt(a, b):
    return lax.dot_general(a, b, (((1,), (1,)), ((), ())), preferred_element_type=F32)


HALF_LANES = LANES // 2


def _rope_layout(dim):
    n_sub = LANES // dim
    w = HALF_LANES // n_sub
    half = dim // ROPE_FRACTION // 2
    perm, kind, freq = np.zeros(LANES, np.int32), np.zeros(LANES, np.int32), np.zeros(LANES, np.int32)
    for s in range(n_sub):
        lanes = list(range(s * w, (s + 1) * w)) + list(range(HALF_LANES + s * w, HALF_LANES + (s + 1) * w))
        x1, x2 = lanes[:half], lanes[w:w + half]
        rest = [l for l in lanes if l not in x1 and l not in x2]
        for i, l in enumerate(x1):
            perm[l], kind[l], freq[l] = s * dim + i, 1, i
        for i, l in enumerate(x2):
            perm[l], kind[l], freq[l] = s * dim + half + i, 2, i
        for i, l in enumerate(rest):
            perm[l] = s * dim + 2 * half + i
    return perm, kind, freq


def _rope_tables(pos, dim):
    rot = dim // ROPE_FRACTION
    _, kind, freq = _rope_layout(dim)
    inv_freq = ROPE_THETA ** (-jnp.arange(0, rot, 2, dtype=F32) / rot)
    ang = pos.astype(F32)[:, None] * inv_freq[None, :]
    cos, sin = jnp.cos(ang)[:, freq], jnp.sin(ang)[:, freq]
    c = jnp.where(kind[None, :] > 0, cos, 1.0)
    ss = jnp.where(kind[None, :] == 1, -sin, jnp.where(kind[None, :] == 2, sin, 0.0))
    return c, ss


def _permute_blocks(w, blocks, dim):
    perm, _, _ = _rope_layout(dim)
    idx = np.arange(w.shape[1], dtype=np.int32)
    for b in blocks:
        idx[b * LANES:(b + 1) * LANES] = b * LANES + perm
    return w[:, idx]


def _apply_rope(a, c, ss):
    return a * c + pltpu.roll(a, HALF_LANES, 1) * ss


def _roped(x, c_ref, ss_ref, rows=slice(None)):
    return _apply_rope(x.astype(F32), c_ref[rows, :], ss_ref[rows, :]).astype(BF16)


def _proj_kernel(x_ref, w_ref, o_ref):
    o_ref[...] = _dot(x_ref[...], w_ref[...]).astype(o_ref.dtype)


def _proj(xb, w, tm=1024, tn=512):
    m, k = xb.shape
    n = w.shape[1]
    assert m % tm == 0 and n % tn == 0
    return pl.pallas_call(
        _proj_kernel,
        grid=(m // tm, n // tn),
        in_specs=[pl.BlockSpec((tm, k), lambda i, j: (i, 0)),
                  pl.BlockSpec((k, tn), lambda i, j: (0, j))],
        out_specs=pl.BlockSpec((tm, tn), lambda i, j: (i, j)),
        out_shape=jax.ShapeDtypeStruct((m, n), BF16),
        compiler_params=_cparams(("parallel", "arbitrary")),
        name="proj",
    )(xb, w)


def _layer_norm(z, g, b):
    mu = jnp.mean(z, axis=-1, keepdims=True)
    zc = z - mu
    var = jnp.mean(zc * zc, axis=-1, keepdims=True)
    return zc * lax.rsqrt(var + NORM_EPS) * g + b


def _outproj_ln_kernel(o_ref, w_ref, h_ref, ln_ref, hf_ref, hb_ref):
    half = o_ref.shape[0] // 2
    for s in range(2):
        rows = slice(s * half, (s + 1) * half)
        m = _dot(o_ref[rows, :], w_ref[...])
        y = _layer_norm(DN_ALPHA * h_ref[rows, :] + m, ln_ref[0:1, :], ln_ref[1:2, :])
        hf_ref[rows, :] = y
        hb_ref[rows, :] = y.astype(BF16)


def _outproj_ln(o, w, h, ln, tm=512):
    m, k = o.shape
    d = w.shape[1]
    return pl.pallas_call(
        _outproj_ln_kernel,
        grid=(m // tm,),
        in_specs=[pl.BlockSpec((tm, k), lambda i: (i, 0)),
                  pl.BlockSpec((k, d), lambda i: (0, 0)),
                  pl.BlockSpec((tm, d), lambda i: (i, 0)),
                  pl.BlockSpec((4, d), lambda i: (0, 0))],
        out_specs=[pl.BlockSpec((tm, d), lambda i: (i, 0)),
                   pl.BlockSpec((tm, d), lambda i: (i, 0))],
        out_shape=[jax.ShapeDtypeStruct((m, d), F32), jax.ShapeDtypeStruct((m, d), BF16)],
        compiler_params=_cparams(("parallel",)),
        name="outproj_ln",
    )(o, w, h, ln)


def _causal_block(q, k, v, c, tq):
    s = _dot_nt(q, k)
    rows = q.shape[0]
    assert tq & (tq - 1) == 0
    pos_in_block = jnp.bitwise_and(lax.broadcasted_iota(jnp.int32, (rows, tq), 0), tq - 1)
    causal = lax.broadcasted_iota(jnp.int32, (rows, tq), 1) <= pos_in_block
    s_diag = jnp.where(causal, s[:, c * tq:], -jnp.inf)
    m = jnp.max(s_diag, axis=-1, keepdims=True)
    if c > 0:
        s_off = s[:, :c * tq]
        m = jnp.maximum(m, jnp.max(s_off, axis=-1, keepdims=True))
        e = jnp.concatenate([jnp.exp(s_off - m), jnp.exp(s_diag - m)], axis=1)
    else:
        e = jnp.exp(s_diag - m)
    inv = 1.0 / jnp.sum(e, axis=-1, keepdims=True)
    return _dot(e.astype(BF16), v) * inv


def _attn_a_kernel(lam_ref, q_ref, k_ref, v_ref, g_ref, c_ref, ss_ref, o_ref, k_scr, *, tq, seq, out_scale):
    k_scr[...] = _roped(k_ref[0], c_ref, ss_ref)
    for c in range(seq // tq):
        kv = (c + 1) * tq
        q = _roped(q_ref[0, c * tq:kv, :], c_ref, ss_ref, slice(c * tq, kv))
        q = q * jnp.asarray(A_QK_DIM ** -0.5, BF16)
        lane = lax.broadcasted_iota(jnp.int32, q.shape, 1)
        zero = jnp.zeros_like(q)
        first = jnp.bitwise_and(lane, HALF_LANES // (LANES // A_QK_DIM)) == 0
        q12 = jnp.concatenate([jnp.where(first, q, zero), jnp.where(first, zero, q)], axis=0)
        both = _causal_block(q12, k_scr[:kv, :], v_ref[0, :kv, :], c, tq)
        o = both[:tq] - lam_ref[0] * both[tq:]
        o = o * lax.rsqrt(jnp.mean(o * o, axis=-1, keepdims=True) + NORM_EPS) * (g_ref[...] * out_scale)
        o_ref[0, c * tq:kv, :] = o.astype(o_ref.dtype)


def _attn_a(proj, lam_full, subln, tables, batch, seq, lam_init, tq=256):
    h = A_HEADS
    tab_spec = pl.BlockSpec((seq, LANES), lambda b, hh, s: (0, 0))
    return pl.pallas_call(
        functools.partial(_attn_a_kernel, tq=tq, seq=seq, out_scale=1.0 - lam_init),
        grid_spec=pltpu.PrefetchScalarGridSpec(
            num_scalar_prefetch=1,
            grid=(batch, h),
            in_specs=[pl.BlockSpec((1, seq, HEAD_DIM), lambda b, hh, s: (b, 0, hh)),
                      pl.BlockSpec((1, seq, HEAD_DIM), lambda b, hh, s: (b, 0, h + hh)),
                      pl.BlockSpec((1, seq, HEAD_DIM), lambda b, hh, s: (b, 0, 2 * h + hh)),
                      pl.BlockSpec((1, HEAD_DIM), lambda b, hh, s: (0, 0)),
                      tab_spec, tab_spec],
            out_specs=pl.BlockSpec((1, seq, HEAD_DIM), lambda b, hh, s: (b, 0, hh)),
            scratch_shapes=[pltpu.VMEM((seq, HEAD_DIM), BF16)]),
        out_shape=jax.ShapeDtypeStruct((batch, seq, h * HEAD_DIM), BF16),
        compiler_params=_cparams(("parallel", "parallel")),
        name="attn_a",
    )(lam_full, proj, proj, proj, subln, *tables)


def _mixer_a(hb, h, w_in, lam, subln, w_out, ln, layer_idx, batch, seq):
    tables = _rope_tables(jnp.arange(seq), A_QK_DIM)
    wb = _permute_blocks(w_in, range(2 * A_HEADS), A_QK_DIM).astype(BF16)
    proj = _proj(hb, wb)
    lam_init = 0.8 - 0.6 * math.exp(-0.3 * layer_idx)
    lf = lam.astype(F32)
    lam_full = jnp.exp(jnp.sum(lf[0] * lf[1])) - jnp.exp(jnp.sum(lf[2] * lf[3])) + lam_init
    o = _attn_a(proj.reshape(batch, seq, -1), lam_full.reshape(1), subln.reshape(1, HEAD_DIM), tables, batch, seq,
                lam_init)
    return _outproj_ln(o.reshape(batch * seq, -1), w_out.astype(BF16), h, ln)


DIL_BLOCK = 128


def _band_block(q, k, v, q0, k0, span):
    s = _dot_nt(q, k)
    diff = (q0 + lax.broadcasted_iota(jnp.int32, s.shape, 0)) - (k0 + lax.broadcasted_iota(jnp.int32, s.shape, 1))
    s = jnp.where((diff >= 0) & (diff <= span), s, -jnp.inf)
    m = jnp.max(s, axis=-1, keepdims=True)
    e = jnp.exp(s - m)
    den = jnp.sum(e, axis=-1, keepdims=True)
    return _dot((e * (1.0 / den)).astype(BF16), v), m + jnp.log(den)


def _dil_kernel(*refs, seq):
    n_grp = len(C_PATTERNS)
    qkv = refs[:3 * n_grp]
    c_ref, ss_ref, o_ref = refs[3 * n_grp:3 * n_grp + 3]
    stage, o_scr, l_scr = refs[3 * n_grp + 3:]
    blk = DIL_BLOCK
    for g, (window, dil) in enumerate(C_PATTERNS):
        q_ref, k_ref, v_ref = qkv[3 * g:3 * g + 3]
        length = seq // dil
        span = window // dil
        n_blk = length // blk
        stage[0] = _apply_rope(q_ref[0].astype(F32), c_ref[...], ss_ref[...])
        stage[1] = _apply_rope(k_ref[0].astype(F32), c_ref[...], ss_ref[...])
        if dil > 1:
            stage[2] = v_ref[0].astype(F32)

        def rows(j, ref, r, start, size):
            if dil == 1:
                return ref[0, start:start + size, :] if j == 2 else stage[j, start:start + size, :].astype(BF16)
            return stage[j, pl.ds(r + start * dil, size, stride=dil), :].astype(BF16)

        for r in range(dil):
            for i in range(n_blk):
                k0 = max(i - 1, 0) * blk
                kl = (i + 1) * blk - k0
                o, lse = _band_block(rows(0, q_ref, r, i * blk, blk), rows(1, k_ref, r, k0, kl),
                                     rows(2, v_ref, r, k0, kl), i * blk, k0, span)
                lse = jnp.broadcast_to(lse, (blk, HEAD_DIM))
                if dil == 1:
                    o_scr[g, i * blk:(i + 1) * blk, :] = o
                    l_scr[g, i * blk:(i + 1) * blk, :] = lse
                else:
                    o_scr[g, pl.ds(r + i * blk * dil, blk, stride=dil), :] = o
                    l_scr[g, pl.ds(r + i * blk * dil, blk, stride=dil), :] = lse

    chunk = 256
    for c in range(seq // chunk):
        sl = slice(c * chunk, (c + 1) * chunk)
        ls = [l_scr[g, sl, :] for g in range(n_grp)]
        m = functools.reduce(jnp.maximum, ls)
        es = [jnp.exp(l - m) for l in ls]
        inv = 1.0 / functools.reduce(lambda a, b: a + b, es)
        o = functools.reduce(lambda a, b: a + b, [(es[g] * inv) * o_scr[g, sl, :] for g in range(n_grp)])
        o_ref[0, sl, :] = o.astype(o_ref.dtype)


def _dil_attn(proj, tables, batch, seq):
    hg, p = C_HEADS_PER_GROUP, len(C_PATTERNS)

    def spec(kind, g):
        return pl.BlockSpec((1, seq, HEAD_DIM), lambda b, hh: (b, 0, kind * p * hg + g * hg + hh))

    in_specs = [spec(kind, g) for g in range(p) for kind in range(3)]
    tab_spec = pl.BlockSpec((seq, LANES), lambda b, hh: (0, 0))
    return pl.pallas_call(
        functools.partial(_dil_kernel, seq=seq),
        grid=(batch, hg),
        in_specs=in_specs + [tab_spec, tab_spec],
        out_specs=pl.BlockSpec((1, seq, HEAD_DIM), lambda b, hh: (b, 0, hh)),
        out_shape=jax.ShapeDtypeStruct((batch, seq, hg * HEAD_DIM), BF16),
        scratch_shapes=[pltpu.VMEM((3, seq, HEAD_DIM), F32), pltpu.VMEM((p, seq, HEAD_DIM), F32),
                        pltpu.VMEM((p, seq, HEAD_DIM), F32)],
        compiler_params=_cparams(("parallel", "parallel")),
        name="dil_attn",
    )(*([proj] * len(in_specs)), *tables)


def _mixer_c(hb, h, w_in, w_out, ln, batch, seq):
    tables = _rope_tables(jnp.arange(seq), HEAD_DIM)
    n = w_in.shape[1]
    wb = _scaled_bf16(_permute_blocks(w_in, range(2 * n // 3 // LANES), HEAD_DIM), n // 3, HEAD_DIM ** -0.5)
    proj = _proj(hb, wb)
    o = _dil_attn(proj.reshape(batch, seq, n), tables, batch, seq)
    return _outproj_ln(o.reshape(batch * seq, -1), w_out.astype(BF16), h, ln)


def _gates_kernel(x_ref, w_ref, o_ref):
    z = _dot(x_ref[...], w_ref[...])
    o_ref[...] = 1.0 / (1.0 + jnp.exp(-z))


def _gates(xb, wg, tm=1024):
    m, k = xb.shape
    n = wg.shape[1]
    return pl.pallas_call(
        _gates_kernel,
        grid=(m // tm,),
        in_specs=[pl.BlockSpec((tm, k), lambda i: (i, 0)), pl.BlockSpec((k, n), lambda i: (0, 0))],
        out_specs=pl.BlockSpec((tm, n), lambda i: (i, 0)),
        out_shape=jax.ShapeDtypeStruct((m, n), F32),
        compiler_params=_cparams(("parallel",)),
        name="nsa_gates",
    )(xb, wg)


def _gelu_tanh(x):
    return 0.5 * x * (1.0 + jnp.tanh(math.sqrt(2.0 / math.pi) * (x + 0.044715 * (x * x * x))))


def _compress_kernel(a_ref, pos_ref, w1_ref, w2_ref, c_ref, ss_ref, o_ref, stage):
    kind = pl.program_id(2)
    stage[...] = a_ref[0].astype(F32)
    n_chunk = stage.shape[0] // B_CMP_STRIDE
    first = jnp.zeros((n_chunk, HEAD_DIM), F32)
    second = jnp.zeros((n_chunk, HEAD_DIM), F32)
    for t in range(B_CMP_STRIDE):
        a = stage[pl.ds(t, n_chunk, stride=B_CMP_STRIDE), :]
        u = B_CMP_STRIDE + t
        first = first + _dot((a + pos_ref[0, t:t + 1, :]).astype(BF16), w1_ref[0, t * HEAD_DIM:(t + 1) * HEAD_DIM, :])
        second = second + _dot((a + pos_ref[0, u:u + 1, :]).astype(BF16), w1_ref[0, u * HEAD_DIM:(u + 1) * HEAD_DIM, :])
    hmid = first + pltpu.roll(second, n_chunk - 1, 0)
    out = _dot(_gelu_tanh(hmid).astype(BF16), w2_ref[0])

    @pl.when(kind == 0)
    def _():
        o_ref[0, 0] = _apply_rope(out, c_ref[...], ss_ref[...]).astype(o_ref.dtype)

    @pl.when(kind != 0)
    def _():
        o_ref[0, 0] = out.astype(o_ref.dtype)


def _compress(proj3, cmp_pos, cmp_w1, cmp_w2, batch, seq):
    g = B_KV_GROUPS
    n_chunk = seq // B_CMP_STRIDE
    cmp_end = jnp.arange(n_chunk) * B_CMP_STRIDE + B_CMP_LEN - 1
    tables = _rope_tables(cmp_end, HEAD_DIM)
    tab_spec = pl.BlockSpec((n_chunk, LANES), lambda b, gg, kk: (0, 0))
    perm, _, _ = _rope_layout(HEAD_DIM)
    cmp_w2 = jnp.stack([cmp_w2[0][:, perm], cmp_w2[1]])
    return pl.pallas_call(
        _compress_kernel,
        grid=(batch, g, 2),
        in_specs=[pl.BlockSpec((1, seq, HEAD_DIM), lambda b, gg, kk: (b, 0, B_HEADS + kk * g + gg)),
                  pl.BlockSpec((1, B_CMP_LEN, HEAD_DIM), lambda b, gg, kk: (kk, 0, 0)),
                  pl.BlockSpec((1, B_CMP_LEN * HEAD_DIM, HEAD_DIM), lambda b, gg, kk: (kk, 0, 0)),
                  pl.BlockSpec((1, HEAD_DIM, HEAD_DIM), lambda b, gg, kk: (kk, 0, 0)),
                  tab_spec, tab_spec],
        out_specs=pl.BlockSpec((1, 1, n_chunk, HEAD_DIM), lambda b, gg, kk: (kk * batch + b, gg, 0, 0)),
        out_shape=jax.ShapeDtypeStruct((2 * batch, g, n_chunk, HEAD_DIM), BF16),
        scratch_shapes=[pltpu.VMEM((seq, HEAD_DIM), F32)],
        compiler_params=_cparams(("parallel", "parallel", "arbitrary")),
        name="nsa_compress",
    )(proj3, cmp_pos, cmp_w1.astype(BF16), cmp_w2.astype(BF16), *tables)


def _split3(x):
    hi = x.astype(BF16)
    r = x - hi.astype(F32)
    mid = r.astype(BF16)
    lo = (r - mid.astype(F32)).astype(BF16)
    return hi, mid, lo


def _nsa_kernel(q_ref, kc_ref, vc_ref, ks_ref, vs_ref, kw_ref, vw_ref, gate_ref, ex_ref, c_ref, ss_ref, o_ref,
                vct_scr, ocmp_scr, ks_scr, kw_scr, q_scr, *, tq, seq, n_cmp, n_sel):
    rep = B_REP
    big = 1e30
    vct_scr[...] = vc_ref[0, 0].astype(F32).T.astype(BF16)
    ks_scr[...] = _roped(ks_ref[0], c_ref, ss_ref)
    kw_scr[...] = _roped(kw_ref[0], c_ref, ss_ref)
    kc = kc_ref[0, 0]
    lane = lax.broadcasted_iota(jnp.int32, (tq, LANES), 1)
    n_back = -(-(B_WINDOW - 1) // tq)
    pair_rows = 2 * tq
    wrow = jnp.bitwise_and(lax.broadcasted_iota(jnp.int32, (pair_rows, tq), 0), tq - 1)
    wcol = lax.broadcasted_iota(jnp.int32, (pair_rows, tq), 1)
    win_bias = []
    for back in range(n_back + 1):
        if back * tq - (tq - 1) >= 0 and back * tq + (tq - 1) <= B_WINDOW - 1:
            win_bias.append(None)
        else:
            diff = back * tq + wrow - wcol
            win_bias.append(jnp.where((diff >= 0) & (diff <= B_WINDOW - 1), 0.0, -jnp.inf))

    for c in range(seq // tq):
        lo, hi = c * tq, (c + 1) * tq

        qpos = lo + lax.broadcasted_iota(jnp.int32, (1, tq), 1)
        crow = lax.broadcasted_iota(jnp.int32, (LANES, tq), 0)
        cmask = (crow * B_CMP_STRIDE + (B_CMP_LEN - 1) <= qpos) & (crow < n_cmp)

        p_sum = jnp.zeros((LANES, tq), F32)
        for r in range(rep):
            cols = slice(r * HEAD_DIM, (r + 1) * HEAD_DIM)
            q_scr[:, cols] = _roped(q_ref[0, lo:hi, cols], c_ref, ss_ref, slice(lo, hi))
            s = jnp.where(cmask, _dot_nt(kc, q_scr[:, cols]), NEG_FILL)
            e = jnp.exp(s - jnp.max(s, axis=0, keepdims=True))
            p = jnp.where(cmask, e * (1.0 / jnp.sum(e, axis=0, keepdims=True)), 0.0)
            ocmp_scr[r] = _dot(vct_scr[...], p.astype(BF16)).T
            p_sum = p_sum + p

        ni = lax.broadcasted_iota(jnp.int32, (LANES, LANES), 0)
        ci = lax.broadcasted_iota(jnp.int32, (LANES, LANES), 1)
        per = B_SEL_LEN // B_CMP_STRIDE
        cover = ((ci >= per * ni - (B_CMP_LEN // B_CMP_STRIDE - 1)) & (ci < per * ni + per) & (ci < n_cmp) & (ni < n_sel))
        cover = jnp.where(cover, 1.0, 0.0).astype(BF16)
        importance = sum(_dot(cover, part) for part in _split3(p_sum))[:n_sel]
        nrow = lax.broadcasted_iota(jnp.int32, (n_sel, tq), 0)
        cur = jnp.right_shift(qpos, int(math.log2(B_SEL_LEN)))
        forced = (nrow == 0) | (nrow == cur) | (nrow == cur - 1)
        score = jnp.where(forced, big, importance)
        score = jnp.where(nrow <= cur, score, -big)
        rank = jnp.zeros((n_sel, tq), jnp.int32)
        for mm in range(min(n_sel, (hi - 1) // B_SEL_LEN + 1)):
            other = score[mm:mm + 1, :]
            ahead = (other > score) | ((other == score) & (nrow > mm))
            rank = rank + jnp.where(ahead, 1, 0)
        sel_bias = jnp.where(rank < min(B_SEL_TOPN, n_sel), 0.0, -big)
        sel_bias = jnp.concatenate([sel_bias, jnp.zeros((LANES - n_sel, tq), F32)], axis=0).T.astype(BF16)

        k_aug = jnp.concatenate([ks_scr[:hi, :], ex_ref[:hi, :]], axis=1)
        v_sel = vs_ref[0, :hi, :]
        wlo = max(c - n_back, 0) * tq
        k_win, v_win = kw_scr[wlo:hi, :], vw_ref[0, wlo:hi, :]
        n_wchunks = (hi - wlo) // tq
        gates = gate_ref[0, lo:hi, :]
        bias2 = jnp.concatenate([sel_bias, sel_bias], axis=0)

        def head_pair(pr, carry):
            q2 = q_scr[:, pl.ds(pl.multiple_of(pr * 2 * HEAD_DIM, 2 * HEAD_DIM), 2 * HEAD_DIM)]
            q = jnp.concatenate([q2[:, :HEAD_DIM], q2[:, HEAD_DIM:]], axis=0)
            o_sel = _causal_block(jnp.concatenate([q, bias2], axis=1), k_aug, v_sel, c, tq)
            s = _dot_nt(q, k_win)
            pieces = []
            for j in range(n_wchunks):
                piece, bias = s[:, j * tq:(j + 1) * tq], win_bias[n_wchunks - 1 - j]
                pieces.append(piece if bias is None else piece + bias)
            s = jnp.concatenate(pieces, axis=1)
            e = jnp.exp(s - jnp.max(s, axis=-1, keepdims=True))
            o_win = _dot(e.astype(BF16), v_win) * (1.0 / jnp.sum(e, axis=-1, keepdims=True))
            outs = []
            for j in range(2):
                r = pr * 2 + j

                def gate(b):
                    return jnp.sum(jnp.where(lane == r * 3 + b, gates, 0.0), axis=-1, keepdims=True)

                rows = slice(j * tq, (j + 1) * tq)
                outs.append(gate(0) * ocmp_scr[r] + gate(1) * o_sel[rows] + gate(2) * o_win[rows])
            o_ref[0, lo:hi, pl.ds(pl.multiple_of(pr * 2 * HEAD_DIM, 2 * HEAD_DIM), 2 * HEAD_DIM)] = (
                jnp.concatenate(outs, axis=1).astype(o_ref.dtype))
            return carry

        lax.fori_loop(0, rep // 2, head_pair, 0)


def _nsa_attn(proj3, kvc, gates, tables, batch, seq, tq=256):
    g, rep = B_KV_GROUPS, B_REP
    n_chunk = seq // B_CMP_STRIDE
    n_cmp = n_chunk - B_CMP_LEN // B_CMP_STRIDE + 1
    n_sel = seq // B_SEL_LEN
    assert n_chunk == LANES and n_sel <= LANES and n_sel % 8 == 0
    onehot = (jnp.arange(seq)[:, None] // B_SEL_LEN == jnp.arange(LANES)[None, :]).astype(BF16)

    def kv_spec(j):
        return pl.BlockSpec((1, seq, HEAD_DIM), lambda b, gg: (b, 0, B_HEADS + j * g + gg))

    return pl.pallas_call(
        functools.partial(_nsa_kernel, tq=tq, seq=seq, n_cmp=n_cmp, n_sel=n_sel),
        grid=(batch, g),
        in_specs=[pl.BlockSpec((1, seq, rep * HEAD_DIM), lambda b, gg: (b, 0, gg)),
                  pl.BlockSpec((1, 1, n_chunk, HEAD_DIM), lambda b, gg: (b, gg, 0, 0)),
                  pl.BlockSpec((1, 1, n_chunk, HEAD_DIM), lambda b, gg: (batch + b, gg, 0, 0)),
                  kv_spec(2), kv_spec(3), kv_spec(4), kv_spec(5),
                  pl.BlockSpec((1, seq, LANES), lambda b, gg: (b, 0, gg)),
                  pl.BlockSpec((seq, LANES), lambda b, gg: (0, 0)),
                  pl.BlockSpec((seq, LANES), lambda b, gg: (0, 0)),
                  pl.BlockSpec((seq, LANES), lambda b, gg: (0, 0))],
        out_specs=pl.BlockSpec((1, seq, rep * HEAD_DIM), lambda b, gg: (b, 0, gg)),
        out_shape=jax.ShapeDtypeStruct((batch, seq, B_HEADS * HEAD_DIM), BF16),
        scratch_shapes=[pltpu.VMEM((HEAD_DIM, n_chunk), BF16), pltpu.VMEM((rep, tq, HEAD_DIM), F32),
                        pltpu.VMEM((seq, HEAD_DIM), BF16), pltpu.VMEM((seq, HEAD_DIM), BF16),
                        pltpu.VMEM((tq, rep * HEAD_DIM), BF16)],
        compiler_params=_cparams(("parallel", "parallel")),
        name="nsa_attn",
    )(proj3, kvc, kvc, proj3, proj3, proj3, proj3, gates.reshape(batch, seq, g * LANES), onehot, *tables)


def _scaled_bf16(w, n_cols, scale):
    col_scale = jnp.where(jnp.arange(w.shape[1]) < n_cols, scale, 1.0).astype(F32)
    return (w * col_scale[None, :]).astype(BF16)


def _mixer_b(hb, h, w_in, cmp_pos, cmp_w1, cmp_w2, w_out, ln, batch, seq):
    g, rep = B_KV_GROUPS, B_REP
    n_main = (B_HEADS + B_N_KV * g) * HEAD_DIM
    tables = _rope_tables(jnp.arange(seq), HEAD_DIM)
    roped = (list(range(B_HEADS)) + list(range(B_HEADS + 2 * g, B_HEADS + 3 * g))
             + list(range(B_HEADS + 4 * g, B_HEADS + 5 * g)))
    wb = _scaled_bf16(_permute_blocks(w_in, roped, HEAD_DIM), B_HEADS * HEAD_DIM, HEAD_DIM ** -0.5)
    proj3 = _proj(hb, wb[:, :n_main]).reshape(batch, seq, n_main)
    wg = wb[:, n_main:].reshape(-1, g, rep * 3)
    wg = jnp.pad(wg, ((0, 0), (0, 0), (0, LANES - rep * 3))).reshape(-1, g * LANES)
    gates = _gates(hb, wg)
    kvc = _compress(proj3, cmp_pos, cmp_w1, cmp_w2, batch, seq)
    o = _nsa_attn(proj3, kvc, gates, tables, batch, seq)
    return _outproj_ln(o.reshape(batch * seq, -1), w_out.astype(BF16), h, ln)


def _router_kernel(x_ref, w_ref, b_ref, o_ref):
    x = x_ref[...]
    xh = x.astype(BF16)
    xl = (x - xh.astype(F32)).astype(BF16)
    hi = _dot(xh, w_ref[...])
    logits = hi[:, :LANES] + (hi[:, LANES:] + _dot(xl, w_ref[:, :LANES])) + b_ref[...]
    lane = lax.broadcasted_iota(jnp.int32, logits.shape, 1)
    lane_f = lane.astype(F32)

    def first_max(vals):
        top = jnp.max(vals, axis=-1, keepdims=True)
        first = jnp.min(jnp.where(vals == top, lane_f, float(4 * LANES)), axis=-1, keepdims=True)
        return top, first.astype(jnp.int32)

    gl = jnp.where(lane < MOE_GROUPS, logits, -jnp.inf)
    gmax, gidx = first_max(gl)
    g_w = 1.0 / jnp.sum(jnp.exp(gl - gmax), axis=-1, keepdims=True)
    lo = MOE_GROUPS + MOE_EXPERTS_PER_GROUP * gidx
    ev = jnp.where((lane >= lo) & (lane < lo + MOE_EXPERTS_PER_GROUP), logits, -jnp.inf)
    v1, i1 = first_max(ev)
    v2, i2 = first_max(jnp.where(lane == i1, -jnp.inf, ev))
    e2 = jnp.exp(v2 - v1)
    w1 = g_w / (1.0 + e2)
    w2 = g_w * e2 / (1.0 + e2)
    out = jnp.where(lane == 0, (i1 - MOE_GROUPS).astype(F32),
                    jnp.where(lane == 1, (i2 - MOE_GROUPS).astype(F32),
                              jnp.where(lane == 2, w1, jnp.where(lane == 3, w2, 0.0))))
    o_ref[...] = out


def _router(h, router_w, router_b, tm=1024):
    m, d = h.shape
    n = router_w.shape[1]
    wp = jnp.pad(router_w, ((0, 0), (0, LANES - n)))
    wh = wp.astype(BF16)
    wp = jnp.concatenate([wh, (wp - wh.astype(F32)).astype(BF16)], axis=1)
    bp = jnp.pad(router_b, (0, LANES - n)).reshape(1, LANES)
    return pl.pallas_call(
        _router_kernel,
        grid=(m // tm,),
        in_specs=[pl.BlockSpec((tm, d), lambda i: (i, 0)),
                  pl.BlockSpec((d, 2 * LANES), lambda i: (0, 0)),
                  pl.BlockSpec((1, LANES), lambda i: (0, 0))],
        out_specs=pl.BlockSpec((tm, LANES), lambda i: (i, 0)),
        out_shape=jax.ShapeDtypeStruct((m, LANES), F32),
        compiler_params=_cparams(("parallel",)),
        name="moe_router",
    )(h, wp, bp)


def _experts_kernel(be_ref, nu_ref, nxt_ref, x_ref, wi_hbm, wo_hbm, *rest, blk0):
    y_ref, wi_buf, wo_buf, wi_scr, wo_scr, slot_ref, sems = rest[-7:]
    i = pl.program_id(0)
    blk = i + blk0
    expert = be_ref[blk]

    def weight_copies(e, slot):
        return (pltpu.make_async_copy(wi_hbm.at[e], wi_buf.at[slot], sems.at[0, slot]),
                pltpu.make_async_copy(wo_hbm.at[e], wo_buf.at[slot], sems.at[1, slot]))

    @pl.when(i == 0)
    def _():
        slot_ref[0] = 0
        for cp in weight_copies(expert, 0):
            cp.start()

    @pl.when((i == 0) | (expert != be_ref[jnp.maximum(blk - 1, 0)]))
    def _():
        slot = slot_ref[0]
        for cp in weight_copies(expert, slot):
            cp.wait()
        wi_scr[...] = wi_buf[slot].astype(BF16)
        wo_scr[...] = wo_buf[slot].astype(BF16)
        nxt = nxt_ref[blk]

        @pl.when(nxt >= 0)
        def _():
            for cp in weight_copies(nxt, 1 - slot):
                cp.start()

        slot_ref[0] = 1 - slot

    @pl.when(blk < nu_ref[0])
    def _():
        gu = _dot(x_ref[...], wi_scr[...])
        gate, up = gu[:, :MOE_D_FF], gu[:, MOE_D_FF:]
        act = gate * (1.0 / (1.0 + jnp.exp(-gate))) * up
        y_ref[...] = _dot(act.astype(BF16), wo_scr[...]).astype(y_ref.dtype)

    @pl.when(blk >= nu_ref[0])
    def _():
        y_ref[...] = jnp.zeros_like(y_ref)


def _experts(xb, blk_e, n_used, next_e, w_in, w_out, ybuf, blk0):
    rows, d = xb.shape
    n_total = blk_e.shape[0]
    in_specs = [pl.BlockSpec((MOE_ROWS, d), lambda i, be, nu, nx: (i, 0)),
                pl.BlockSpec(memory_space=pl.ANY),
                pl.BlockSpec(memory_space=pl.ANY)]
    args = [blk_e, n_used, next_e, xb, w_in, w_out]
    aliases = {}
    if ybuf is not None:
        in_specs.append(pl.BlockSpec(memory_space=pl.ANY))
        args.append(ybuf)
        aliases = {len(args) - 1: 0}
    return pl.pallas_call(
        functools.partial(_experts_kernel, blk0=blk0),
        grid_spec=pltpu.PrefetchScalarGridSpec(
            num_scalar_prefetch=3,
            grid=(rows // MOE_ROWS,),
            in_specs=in_specs,
            out_specs=pl.BlockSpec((MOE_ROWS, d), lambda i, be, nu, nx: (i + blk0, 0)),
            scratch_shapes=[pltpu.VMEM((2, d, 2 * MOE_D_FF), F32), pltpu.VMEM((2, MOE_D_FF, d), F32),
                            pltpu.VMEM((d, 2 * MOE_D_FF), BF16), pltpu.VMEM((MOE_D_FF, d), BF16),
                            pltpu.SMEM((1,), jnp.int32), pltpu.SemaphoreType.DMA((2, 2))]),
        out_shape=jax.ShapeDtypeStruct((n_total * MOE_ROWS, d), BF16),
        input_output_aliases=aliases,
        compiler_params=_cparams(("arbitrary",)),
        name="moe_experts",
    )(*args)


def _moe_ln_kernel(h_ref, y0_ref, y1_ref, r_ref, ln_ref, hf_ref, hb_ref):
    r = r_ref[...]
    f = y0_ref[...].astype(F32) * r[:, 2:3] + y1_ref[...].astype(F32) * r[:, 3:4]
    y = _layer_norm(DN_ALPHA * h_ref[...] + f, ln_ref[2:3, :], ln_ref[3:4, :])
    hf_ref[...] = y
    hb_ref[...] = y.astype(BF16)


def _moe_ln(h, y0, y1, route, ln, tm=512):
    m, d = h.shape
    row = lambda i: (i, 0)
    return pl.pallas_call(
        _moe_ln_kernel,
        grid=(m // tm,),
        in_specs=[pl.BlockSpec((tm, d), row), pl.BlockSpec((tm, d), row), pl.BlockSpec((tm, d), row),
                  pl.BlockSpec((tm, LANES), row), pl.BlockSpec((4, d), lambda i: (0, 0))],
        out_specs=[pl.BlockSpec((tm, d), row), pl.BlockSpec((tm, d), row)],
        out_shape=[jax.ShapeDtypeStruct((m, d), F32), jax.ShapeDtypeStruct((m, d), BF16)],
        compiler_params=_cparams(("parallel",)),
        name="moe_ln",
    )(h, y0, y1, route, ln)


def _moe(h, hb, router_w, router_b, w_in, w_out, ln):
    t, d = h.shape
    e, rows = MOE_EXPERTS, MOE_ROWS
    route = _router(h, router_w, router_b)
    flat_e = route[:, 0:2].astype(jnp.int32).reshape(-1)
    n_assign = flat_e.shape[0]
    n_blk = -(-n_assign // rows) + e
    onehot = (flat_e[:, None] == jnp.arange(e, dtype=jnp.int32)[None, :]).astype(jnp.int32)
    csum = jnp.cumsum(onehot, axis=0)
    rank = jnp.take_along_axis(csum, flat_e[:, None], axis=1)[:, 0] - 1
    sizes = csum[-1]
    padded = (sizes + rows - 1) // rows * rows
    pad_end = jnp.cumsum(padded)
    pad_start = pad_end - padded
    dest = pad_start[flat_e] + rank
    flat_tok = jnp.arange(n_assign, dtype=jnp.int32) // 2
    slot_tok = jnp.zeros((n_blk * rows,), jnp.int32).at[dest].set(flat_tok)
    blk_start = jnp.arange(n_blk, dtype=jnp.int32) * rows
    blk_e = jnp.minimum(jnp.sum((pad_end[None, :] <= blk_start[:, None]).astype(jnp.int32), axis=1), e - 1)
    n_used = (pad_end[-1] // rows).astype(jnp.int32).reshape(1)
    assert n_blk % MOE_CHUNKS == 0
    per = n_blk // MOE_CHUNKS
    blk_id = jnp.arange(n_blk, dtype=jnp.int32)
    run_end = pad_end[blk_e] // rows
    after = blk_e[jnp.minimum(run_end, n_blk - 1)]
    in_call = (run_end > blk_id) & (run_end < (blk_id // per + 1) * per) & (after != blk_e)
    next_e = jnp.where(in_call, after, -1).astype(jnp.int32)
    yb = None
    for c in range(MOE_CHUNKS):
        xb = hb[slot_tok[c * per * rows:(c + 1) * per * rows]]
        yb = _experts(xb, blk_e, n_used, next_e, w_in, w_out, yb, c * per)
    dest2 = dest.reshape(t, 2)
    return _moe_ln(h, yb[dest2[:, 0]], yb[dest2[:, 1]], route, ln)


def kernel(x, l0_a_w_in, l0_a_lam, l0_a_subln, l0_a_w_out, l0_ln, l0_router_w, l0_router_b, l0_moe_w_in, l0_moe_w_out, l1_b_w_in, l1_b_cmp_pos, l1_b_cmp_w1, l1_b_cmp_w2, l1_b_w_out, l1_ln, l1_router_w, l1_router_b, l1_moe_w_in, l1_moe_w_out, l2_c_w_in, l2_c_w_out, l2_ln, l2_router_w, l2_router_b, l2_moe_w_in, l2_moe_w_out, l3_a_w_in, l3_a_lam, l3_a_subln, l3_a_w_out, l3_ln, l3_router_w, l3_router_b, l3_moe_w_in, l3_moe_w_out):
    batch, seq, d = x.shape
    h = x.reshape(batch * seq, d)
    hb = h.astype(BF16)
    h, hb = _mixer_a(hb, h, l0_a_w_in, l0_a_lam, l0_a_subln, l0_a_w_out, l0_ln, 0, batch, seq)
    h, hb = _moe(h, hb, l0_router_w, l0_router_b, l0_moe_w_in, l0_moe_w_out, l0_ln)
    h, hb = _mixer_b(hb, h, l1_b_w_in, l1_b_cmp_pos, l1_b_cmp_w1, l1_b_cmp_w2, l1_b_w_out, l1_ln, batch, seq)
    h, hb = _moe(h, hb, l1_router_w, l1_router_b, l1_moe_w_in, l1_moe_w_out, l1_ln)
    h, hb = _mixer_c(hb, h, l2_c_w_in, l2_c_w_out, l2_ln, batch, seq)
    h, hb = _moe(h, hb, l2_router_w, l2_router_b, l2_moe_w_in, l2_moe_w_out, l2_ln)
    h, hb = _mixer_a(hb, h, l3_a_w_in, l3_a_lam, l3_a_subln, l3_a_w_out, l3_ln, 3, batch, seq)
    h, hb = _moe(h, hb, l3_router_w, l3_router_b, l3_moe_w_in, l3_moe_w_out, l3_ln)
    return h.reshape(batch, seq, d)
```

```python
import functools
import math

import jax
import jax.numpy as jnp
import numpy as np
from jax import lax
from jax.experimental import pallas as pl
from jax.experimental.pallas import tpu as pltpu

D_MODEL = 2048
DEPTH = 4
HEAD_DIM = 128
ROPE_THETA = 500000.0
ROPE_FRACTION = 4
NEG_FILL = -1e30

A_HEADS = D_MODEL // HEAD_DIM
A_QK_DIM = HEAD_DIM // 2

B_HEADS = D_MODEL // HEAD_DIM
B_KV_GROUPS = 4
B_REP = B_HEADS // B_KV_GROUPS
B_CMP_LEN = 32
B_CMP_STRIDE = 16
B_SEL_LEN = 64
B_SEL_TOPN = 16
B_WINDOW = 512
B_N_KV = 6

C_PATTERNS = ((128, 1), (512, 4), (2048, 16))
C_HEADS_PER_GROUP = D_MODEL // (2 * HEAD_DIM)

MOE_GROUPS = 4
MOE_EXPERTS_PER_GROUP = 8
MOE_EXPERTS = MOE_GROUPS * MOE_EXPERTS_PER_GROUP
MOE_D_FF = D_MODEL // 4
MOE_ROWS = 256
MOE_CHUNKS = 1

DN_ALPHA = (2 * DEPTH) ** 0.25
NORM_EPS = 1e-5

LANES = 128
VMEM_LIMIT = 56 * 1024 * 1024

BF16 = jnp.bfloat16
F32 = jnp.float32


def _cparams(sem):
    return pltpu.CompilerParams(dimension_semantics=sem, vmem_limit_bytes=VMEM_LIMIT)


def _dot(a, b):
    return jnp.dot(a, b, preferred_element_type=F32)


def _dot_nt(a, b):
    return lax.dot_general(a, b, (((1,), (1,)), ((), ())), preferred_element_type=F32)


HALF_LANES = LANES // 2


def _rope_layout(dim):
    n_sub = LANES // dim
    w = HALF_LANES // n_sub
    half = dim // ROPE_FRACTION // 2
    perm, kind, freq = np.zeros(LANES, np.int32), np.zeros(LANES, np.int32), np.zeros(LANES, np.int32)
    for s in range(n_sub):
        lanes = list(range(s * w, (s + 1) * w)) + list(range(HALF_LANES + s * w, HALF_LANES + (s + 1) * w))
        x1, x2 = lanes[:half], lanes[w:w + half]
        rest = [l for l in lanes if l not in x1 and l not in x2]
        for i, l in enumerate(x1):
            perm[l], kind[l], freq[l] = s * dim + i, 1, i
        for i, l in enumerate(x2):
            perm[l], kind[l], freq[l] = s * dim + half + i, 2, i
        for i, l in enumerate(rest):
            perm[l] = s * dim + 2 * half + i
    return perm, kind, freq


def _rope_tables(pos, dim):
    rot = dim // ROPE_FRACTION
    _, kind, freq = _rope_layout(dim)
    inv_freq = ROPE_THETA ** (-jnp.arange(0, rot, 2, dtype=F32) / rot)
    ang = pos.astype(F32)[:, None] * inv_freq[None, :]
    cos, sin = jnp.cos(ang)[:, freq], jnp.sin(ang)[:, freq]
    c = jnp.where(kind[None, :] > 0, cos, 1.0)
    ss = jnp.where(kind[None, :] == 1, -sin, jnp.where(kind[None, :] == 2, sin, 0.0))
    return c, ss


def _permute_blocks(w, blocks, dim):
    perm, _, _ = _rope_layout(dim)
    idx = np.arange(w.shape[1], dtype=np.int32)
    for b in blocks:
        idx[b * LANES:(b + 1) * LANES] = b * LANES + perm
    return w[:, idx]


def _apply_rope(a, c, ss):
    return a * c + pltpu.roll(a, HALF_LANES, 1) * ss


def _roped(x, c_ref, ss_ref, rows=slice(None)):
    return _apply_rope(x.astype(F32), c_ref[rows, :], ss_ref[rows, :]).astype(BF16)


def _proj_kernel(x_ref, w_ref, o_ref):
    o_ref[...] = _dot(x_ref[...], w_ref[...]).astype(o_ref.dtype)


def _proj(xb, w, tm=1024, tn=1024):
    m, k = xb.shape
    n = w.shape[1]
    assert m % tm == 0 and n % tn == 0
    return pl.pallas_call(
        _proj_kernel,
        grid=(m // tm, n // tn),
        in_specs=[pl.BlockSpec((tm, k), lambda i, j: (i, 0)),
                  pl.BlockSpec((k, tn), lambda i, j: (0, j))],
        out_specs=pl.BlockSpec((tm, tn), lambda i, j: (i, j)),
        out_shape=jax.ShapeDtypeStruct((m, n), BF16),
        compiler_params=_cparams(("parallel", "arbitrary")),
        name="proj",
    )(xb, w)


def _layer_norm(z, g, b):
    mu = jnp.mean(z, axis=-1, keepdims=True)
    zc = z - mu
    var = jnp.mean(zc * zc, axis=-1, keepdims=True)
    return zc * lax.rsqrt(var + NORM_EPS) * g + b


def _outproj_ln_kernel(o_ref, w_ref, h_ref, ln_ref, hf_ref, hb_ref):
    half = o_ref.shape[0] // 2
    for s in range(2):
        rows = slice(s * half, (s + 1) * half)
        m = _dot(o_ref[rows, :], w_ref[...])
        y = _layer_norm(DN_ALPHA * h_ref[rows, :] + m, ln_ref[0:1, :], ln_ref[1:2, :])
        hf_ref[rows, :] = y
        hb_ref[rows, :] = y.astype(BF16)


def _outproj_ln(o, w, h, ln, tm=512):
    m, k = o.shape
    d = w.shape[1]
    return pl.pallas_call(
        _outproj_ln_kernel,
        grid=(m // tm,),
        in_specs=[pl.BlockSpec((tm, k), lambda i: (i, 0)),
                  pl.BlockSpec((k, d), lambda i: (0, 0)),
                  pl.BlockSpec((tm, d), lambda i: (i, 0)),
                  pl.BlockSpec((4, d), lambda i: (0, 0))],
        out_specs=[pl.BlockSpec((tm, d), lambda i: (i, 0)),
                   pl.BlockSpec((tm, d), lambda i: (i, 0))],
        out_shape=[jax.ShapeDtypeStruct((m, d), F32), jax.ShapeDtypeStruct((m, d), BF16)],
        compiler_params=_cparams(("parallel",)),
        name="outproj_ln",
    )(o, w, h, ln)


def _causal_block(q, k, v, c, tq):
    s = _dot_nt(q, k)
    rows = q.shape[0]
    assert tq & (tq - 1) == 0
    pos_in_block = jnp.bitwise_and(lax.broadcasted_iota(jnp.int32, (rows, tq), 0), tq - 1)
    causal = lax.broadcasted_iota(jnp.int32, (rows, tq), 1) <= pos_in_block
    s_diag = jnp.where(causal, s[:, c * tq:], -jnp.inf)
    m = jnp.max(s_diag, axis=-1, keepdims=True)
    if c > 0:
        s_off = s[:, :c * tq]
        m = jnp.maximum(m, jnp.max(s_off, axis=-1, keepdims=True))
        e = jnp.concatenate([jnp.exp(s_off - m), jnp.exp(s_diag - m)], axis=1)
    else:
        e = jnp.exp(s_diag - m)
    inv = 1.0 / jnp.sum(e, axis=-1, keepdims=True)
    return _dot(e.astype(BF16), v) * inv


def _attn_a_kernel(lam_ref, q_ref, k_ref, v_ref, g_ref, c_ref, ss_ref, o_ref, k_scr, *, tq, seq, out_scale):
    k_scr[...] = _roped(k_ref[0], c_ref, ss_ref)
    for c in range(seq // tq):
        kv = (c + 1) * tq
        q = _roped(q_ref[0, c * tq:kv, :], c_ref, ss_ref, slice(c * tq, kv))
        q = q * jnp.asarray(A_QK_DIM ** -0.5, BF16)
        lane = lax.broadcasted_iota(jnp.int32, q.shape, 1)
        zero = jnp.zeros_like(q)
        first = jnp.bitwise_and(lane, HALF_LANES // (LANES // A_QK_DIM)) == 0
        q12 = jnp.concatenate([jnp.where(first, q, zero), jnp.where(first, zero, q)], axis=0)
        both = _causal_block(q12, k_scr[:kv, :], v_ref[0, :kv, :], c, tq)
        o = both[:tq] - lam_ref[0] * both[tq:]
        o = o * lax.rsqrt(jnp.mean(o * o, axis=-1, keepdims=True) + NORM_EPS) * (g_ref[...] * out_scale)
        o_ref[0, c * tq:kv, :] = o.astype(o_ref.dtype)


def _attn_a(proj, lam_full, subln, tables, batch, seq, lam_init, tq=256):
    h = A_HEADS
    tab_spec = pl.BlockSpec((seq, LANES), lambda b, hh, s: (0, 0))
    return pl.pallas_call(
        functools.partial(_attn_a_kernel, tq=tq, seq=seq, out_scale=1.0 - lam_init),
        grid_spec=pltpu.PrefetchScalarGridSpec(
            num_scalar_prefetch=1,
            grid=(batch, h),
            in_specs=[pl.BlockSpec((1, seq, HEAD_DIM), lambda b, hh, s: (b, 0, hh)),
                      pl.BlockSpec((1, seq, HEAD_DIM), lambda b, hh, s: (b, 0, h + hh)),
                      pl.BlockSpec((1, seq, HEAD_DIM), lambda b, hh, s: (b, 0, 2 * h + hh)),
                      pl.BlockSpec((1, HEAD_DIM), lambda b, hh, s: (0, 0)),
                      tab_spec, tab_spec],
            out_specs=pl.BlockSpec((1, seq, HEAD_DIM), lambda b, hh, s: (b, 0, hh)),
            scratch_shapes=[pltpu.VMEM((seq, HEAD_DIM), BF16)]),
        out_shape=jax.ShapeDtypeStruct((batch, seq, h * HEAD_DIM), BF16),
        compiler_params=_cparams(("parallel", "parallel")),
        name="attn_a",
    )(lam_full, proj, proj, proj, subln, *tables)


def _mixer_a(hb, h, w_in, lam, subln, w_out, ln, layer_idx, batch, seq):
    tables = _rope_tables(jnp.arange(seq), A_QK_DIM)
    wb = _permute_blocks(w_in, range(2 * A_HEADS), A_QK_DIM).astype(BF16)
    proj = _proj(hb, wb)
    lam_init = 0.8 - 0.6 * math.exp(-0.3 * layer_idx)
    lf = lam.astype(F32)
    lam_full = jnp.exp(jnp.sum(lf[0] * lf[1])) - jnp.exp(jnp.sum(lf[2] * lf[3])) + lam_init
    o = _attn_a(proj.reshape(batch, seq, -1), lam_full.reshape(1), subln.reshape(1, HEAD_DIM), tables, batch, seq,
                lam_init)
    return _outproj_ln(o.reshape(batch * seq, -1), w_out.astype(BF16), h, ln)


DIL_BLOCK = 128


def _band_block(q, k, v, q0, k0, span):
    s = _dot_nt(q, k)
    diff = (q0 + lax.broadcasted_iota(jnp.int32, s.shape, 0)) - (k0 + lax.broadcasted_iota(jnp.int32, s.shape, 1))
    s = jnp.where((diff >= 0) & (diff <= span), s, -jnp.inf)
    m = jnp.max(s, axis=-1, keepdims=True)
    e = jnp.exp(s - m)
    den = jnp.sum(e, axis=-1, keepdims=True)
    return _dot((e * (1.0 / den)).astype(BF16), v), m + jnp.log(den)


def _dil_kernel(*refs, seq):
    n_grp = len(C_PATTERNS)
    qkv = refs[:3 * n_grp]
    c_ref, ss_ref, o_ref = refs[3 * n_grp:3 * n_grp + 3]
    stage, o_scr, l_scr = refs[3 * n_grp + 3:]
    blk = DIL_BLOCK
    for g, (window, dil) in enumerate(C_PATTERNS):
        q_ref, k_ref, v_ref = qkv[3 * g:3 * g + 3]
        length = seq // dil
        span = window // dil
        n_blk = length // blk
        stage[0] = _apply_rope(q_ref[0].astype(F32), c_ref[...], ss_ref[...])
        stage[1] = _apply_rope(k_ref[0].astype(F32), c_ref[...], ss_ref[...])
        if dil > 1:
            stage[2] = v_ref[0].astype(F32)

        def rows(j, ref, r, start, size):
            if dil == 1:
                return ref[0, start:start + size, :] if j == 2 else stage[j, start:start + size, :].astype(BF16)
            return stage[j, pl.ds(r + start * dil, size, stride=dil), :].astype(BF16)

        for r in range(dil):
            for i in range(n_blk):
                k0 = max(i - 1, 0) * blk
                kl = (i + 1) * blk - k0
                o, lse = _band_block(rows(0, q_ref, r, i * blk, blk), rows(1, k_ref, r, k0, kl),
                                     rows(2, v_ref, r, k0, kl), i * blk, k0, span)
                lse = jnp.broadcast_to(lse, (blk, HEAD_DIM))
                if dil == 1:
                    o_scr[g, i * blk:(i + 1) * blk, :] = o
                    l_scr[g, i * blk:(i + 1) * blk, :] = lse
                else:
                    o_scr[g, pl.ds(r + i * blk * dil, blk, stride=dil), :] = o
                    l_scr[g, pl.ds(r + i * blk * dil, blk, stride=dil), :] = lse

    chunk = 256
    for c in range(seq // chunk):
        sl = slice(c * chunk, (c + 1) * chunk)
        ls = [l_scr[g, sl, :] for g in range(n_grp)]
        m = functools.reduce(jnp.maximum, ls)
        es = [jnp.exp(l - m) for l in ls]
        inv = 1.0 / functools.reduce(lambda a, b: a + b, es)
        o = functools.reduce(lambda a, b: a + b, [(es[g] * inv) * o_scr[g, sl, :] for g in range(n_grp)])
        o_ref[0, sl, :] = o.astype(o_ref.dtype)


def _dil_attn(proj, tables, batch, seq):
    hg, p = C_HEADS_PER_GROUP, len(C_PATTERNS)

    def spec(kind, g):
        return pl.BlockSpec((1, seq, HEAD_DIM), lambda b, hh: (b, 0, kind * p * hg + g * hg + hh))

    in_specs = [spec(kind, g) for g in range(p) for kind in range(3)]
    tab_spec = pl.BlockSpec((seq, LANES), lambda b, hh: (0, 0))
    return pl.pallas_call(
        functools.partial(_dil_kernel, seq=seq),
        grid=(batch, hg),
        in_specs=in_specs + [tab_spec, tab_spec],
        out_specs=pl.BlockSpec((1, seq, HEAD_DIM), lambda b, hh: (b, 0, hh)),
        out_shape=jax.ShapeDtypeStruct((batch, seq, hg * HEAD_DIM), BF16),
        scratch_shapes=[pltpu.VMEM((3, seq, HEAD_DIM), F32), pltpu.VMEM((p, seq, HEAD_DIM), F32),
                        pltpu.VMEM((p, seq, HEAD_DIM), F32)],
        compiler_params=_cparams(("parallel", "parallel")),
        name="dil_attn",
    )(*([proj] * len(in_specs)), *tables)


def _mixer_c(hb, h, w_in, w_out, ln, batch, seq):
    tables = _rope_tables(jnp.arange(seq), HEAD_DIM)
    n = w_in.shape[1]
    wb = _scaled_bf16(_permute_blocks(w_in, range(2 * n // 3 // LANES), HEAD_DIM), n // 3, HEAD_DIM ** -0.5)
    proj = _proj(hb, wb)
    o = _dil_attn(proj.reshape(batch, seq, n), tables, batch, seq)
    return _outproj_ln(o.reshape(batch * seq, -1), w_out.astype(BF16), h, ln)


def _gates_kernel(x_ref, w_ref, o_ref):
    z = _dot(x_ref[...], w_ref[...])
    o_ref[...] = 1.0 / (1.0 + jnp.exp(-z))


def _gates(xb, wg, tm=1024):
    m, k = xb.shape
    n = wg.shape[1]
    return pl.pallas_call(
        _gates_kernel,
        grid=(m // tm,),
        in_specs=[pl.BlockSpec((tm, k), lambda i: (i, 0)), pl.BlockSpec((k, n), lambda i: (0, 0))],
        out_specs=pl.BlockSpec((tm, n), lambda i: (i, 0)),
        out_shape=jax.ShapeDtypeStruct((m, n), F32),
        compiler_params=_cparams(("parallel",)),
        name="nsa_gates",
    )(xb, wg)


def _gelu_tanh(x):
    return 0.5 * x * (1.0 + jnp.tanh(math.sqrt(2.0 / math.pi) * (x + 0.044715 * (x * x * x))))


def _compress_kernel(a_ref, pos_ref, w1_ref, w2_ref, c_ref, ss_ref, o_ref, stage):
    kind = pl.program_id(2)
    stage[...] = a_ref[0].astype(F32)
    n_chunk = stage.shape[0] // B_CMP_STRIDE
    first = jnp.zeros((n_chunk, HEAD_DIM), F32)
    second = jnp.zeros((n_chunk, HEAD_DIM), F32)
    for t in range(B_CMP_STRIDE):
        a = stage[pl.ds(t, n_chunk, stride=B_CMP_STRIDE), :]
        u = B_CMP_STRIDE + t
        first = first + _dot((a + pos_ref[0, t:t + 1, :]).astype(BF16), w1_ref[0, t * HEAD_DIM:(t + 1) * HEAD_DIM, :])
        second = second + _dot((a + pos_ref[0, u:u + 1, :]).astype(BF16), w1_ref[0, u * HEAD_DIM:(u + 1) * HEAD_DIM, :])
    hmid = first + pltpu.roll(second, n_chunk - 1, 0)
    out = _dot(_gelu_tanh(hmid).astype(BF16), w2_ref[0])

    @pl.when(kind == 0)
    def _():
        o_ref[0, 0] = _apply_rope(out, c_ref[...], ss_ref[...]).astype(o_ref.dtype)

    @pl.when(kind != 0)
    def _():
        o_ref[0, 0] = out.astype(o_ref.dtype)


def _compress(proj3, cmp_pos, cmp_w1, cmp_w2, batch, seq):
    g = B_KV_GROUPS
    n_chunk = seq // B_CMP_STRIDE
    cmp_end = jnp.arange(n_chunk) * B_CMP_STRIDE + B_CMP_LEN - 1
    tables = _rope_tables(cmp_end, HEAD_DIM)
    tab_spec = pl.BlockSpec((n_chunk, LANES), lambda b, gg, kk: (0, 0))
    perm, _, _ = _rope_layout(HEAD_DIM)
    cmp_w2 = jnp.stack([cmp_w2[0][:, perm], cmp_w2[1]])
    return pl.pallas_call(
        _compress_kernel,
        grid=(batch, g, 2),
        in_specs=[pl.BlockSpec((1, seq, HEAD_DIM), lambda b, gg, kk: (b, 0, B_HEADS + kk * g + gg)),
                  pl.BlockSpec((1, B_CMP_LEN, HEAD_DIM), lambda b, gg, kk: (kk, 0, 0)),
                  pl.BlockSpec((1, B_CMP_LEN * HEAD_DIM, HEAD_DIM), lambda b, gg, kk: (kk, 0, 0)),
                  pl.BlockSpec((1, HEAD_DIM, HEAD_DIM), lambda b, gg, kk: (kk, 0, 0)),
                  tab_spec, tab_spec],
        out_specs=pl.BlockSpec((1, 1, n_chunk, HEAD_DIM), lambda b, gg, kk: (kk * batch + b, gg, 0, 0)),
        out_shape=jax.ShapeDtypeStruct((2 * batch, g, n_chunk, HEAD_DIM), BF16),
        scratch_shapes=[pltpu.VMEM((seq, HEAD_DIM), F32)],
        compiler_params=_cparams(("parallel", "parallel", "arbitrary")),
        name="nsa_compress",
    )(proj3, cmp_pos, cmp_w1.astype(BF16), cmp_w2.astype(BF16), *tables)


def _split3(x):
    hi = x.astype(BF16)
    r = x - hi.astype(F32)
    mid = r.astype(BF16)
    lo = (r - mid.astype(F32)).astype(BF16)
    return hi, mid, lo


def _nsa_kernel(q_ref, kc_ref, vc_ref, ks_ref, vs_ref, kw_ref, vw_ref, gate_ref, ex_ref, c_ref, ss_ref, o_ref,
                vct_scr, ocmp_scr, ks_scr, kw_scr, q_scr, *, tq, seq, n_cmp, n_sel):
    rep = B_REP
    big = 1e30
    vct_scr[...] = vc_ref[0, 0].astype(F32).T.astype(BF16)
    ks_scr[...] = _roped(ks_ref[0], c_ref, ss_ref)
    kw_scr[...] = _roped(kw_ref[0], c_ref, ss_ref)
    kc = kc_ref[0, 0]
    lane = lax.broadcasted_iota(jnp.int32, (tq, LANES), 1)
    n_back = -(-(B_WINDOW - 1) // tq)
    pair_rows = 2 * tq
    wrow = jnp.bitwise_and(lax.broadcasted_iota(jnp.int32, (pair_rows, tq), 0), tq - 1)
    wcol = lax.broadcasted_iota(jnp.int32, (pair_rows, tq), 1)
    win_bias = []
    for back in range(n_back + 1):
        if back * tq - (tq - 1) >= 0 and back * tq + (tq - 1) <= B_WINDOW - 1:
            win_bias.append(None)
        else:
            diff = back * tq + wrow - wcol
            win_bias.append(jnp.where((diff >= 0) & (diff <= B_WINDOW - 1), 0.0, -jnp.inf))

    for c in range(seq // tq):
        lo, hi = c * tq, (c + 1) * tq

        qpos = lo + lax.broadcasted_iota(jnp.int32, (1, tq), 1)
        crow = lax.broadcasted_iota(jnp.int32, (LANES, tq), 0)
        cmask = (crow * B_CMP_STRIDE + (B_CMP_LEN - 1) <= qpos) & (crow < n_cmp)

        p_sum = jnp.zeros((LANES, tq), F32)
        for r in range(rep):
            cols = slice(r * HEAD_DIM, (r + 1) * HEAD_DIM)
            q_scr[:, cols] = _roped(q_ref[0, lo:hi, cols], c_ref, ss_ref, slice(lo, hi))
            s = jnp.where(cmask, _dot_nt(kc, q_scr[:, cols]), NEG_FILL)
            e = jnp.exp(s - jnp.max(s, axis=0, keepdims=True))
            p = jnp.where(cmask, e * (1.0 / jnp.sum(e, axis=0, keepdims=True)), 0.0)
            ocmp_scr[r] = _dot(vct_scr[...], p.astype(BF16)).T
            p_sum = p_sum + p

        ni = lax.broadcasted_iota(jnp.int32, (LANES, LANES), 0)
        ci = lax.broadcasted_iota(jnp.int32, (LANES, LANES), 1)
        per = B_SEL_LEN // B_CMP_STRIDE
        cover = ((ci >= per * ni - (B_CMP_LEN // B_CMP_STRIDE - 1)) & (ci < per * ni + per) & (ci < n_cmp) & (ni < n_sel))
        cover = jnp.where(cover, 1.0, 0.0).astype(BF16)
        importance = sum(_dot(cover, part) for part in _split3(p_sum))[:n_sel]
        nrow = lax.broadcasted_iota(jnp.int32, (n_sel, tq), 0)
        cur = jnp.right_shift(qpos, int(math.log2(B_SEL_LEN)))
        forced = (nrow == 0) | (nrow == cur) | (nrow == cur - 1)
        score = jnp.where(forced, big, importance)
        score = jnp.where(nrow <= cur, score, -big)
        rank = jnp.zeros((n_sel, tq), jnp.int32)
        for mm in range(min(n_sel, (hi - 1) // B_SEL_LEN + 1)):
            other = score[mm:mm + 1, :]
            ahead = (other > score) | ((other == score) & (nrow > mm))
            rank = rank + jnp.where(ahead, 1, 0)
        sel_bias = jnp.where(rank < min(B_SEL_TOPN, n_sel), 0.0, -big)
        sel_bias = jnp.concatenate([sel_bias, jnp.zeros((LANES - n_sel, tq), F32)], axis=0).T.astype(BF16)

        k_aug = jnp.concatenate([ks_scr[:hi, :], ex_ref[:hi, :]], axis=1)
        v_sel = vs_ref[0, :hi, :]
        wlo = max(c - n_back, 0) * tq
        k_win, v_win = kw_scr[wlo:hi, :], vw_ref[0, wlo:hi, :]
        n_wchunks = (hi - wlo) // tq
        gates = gate_ref[0, lo:hi, :]
        bias2 = jnp.concatenate([sel_bias, sel_bias], axis=0)

        def head_pair(pr, carry):
            q2 = q_scr[:, pl.ds(pl.multiple_of(pr * 2 * HEAD_DIM, 2 * HEAD_DIM), 2 * HEAD_DIM)]
            q = jnp.concatenate([q2[:, :HEAD_DIM], q2[:, HEAD_DIM:]], axis=0)
            o_sel = _causal_block(jnp.concatenate([q, bias2], axis=1), k_aug, v_sel, c, tq)
            s = _dot_nt(q, k_win)
            pieces = []
            for j in range(n_wchunks):
                piece, bias = s[:, j * tq:(j + 1) * tq], win_bias[n_wchunks - 1 - j]
                pieces.append(piece if bias is None else piece + bias)
            s = jnp.concatenate(pieces, axis=1)
            e = jnp.exp(s - jnp.max(s, axis=-1, keepdims=True))
            o_win = _dot(e.astype(BF16), v_win) * (1.0 / jnp.sum(e, axis=-1, keepdims=True))
            outs = []
            for j in range(2):
                r = pr * 2 + j

                def gate(b):
                    return jnp.sum(jnp.where(lane == r * 3 + b, gates, 0.0), axis=-1, keepdims=True)

                rows = slice(j * tq, (j + 1) * tq)
                outs.append(gate(0) * ocmp_scr[r] + gate(1) * o_sel[rows] + gate(2) * o_win[rows])
            o_ref[0, lo:hi, pl.ds(pl.multiple_of(pr * 2 * HEAD_DIM, 2 * HEAD_DIM), 2 * HEAD_DIM)] = (
                jnp.concatenate(outs, axis=1).astype(o_ref.dtype))
            return carry

        lax.fori_loop(0, rep // 2, head_pair, 0)


def _nsa_attn(proj3, kvc, gates, tables, batch, seq, tq=256):
    g, rep = B_KV_GROUPS, B_REP
    n_chunk = seq // B_CMP_STRIDE
    n_cmp = n_chunk - B_CMP_LEN // B_CMP_STRIDE + 1
    n_sel = seq // B_SEL_LEN
    assert n_chunk == LANES and n_sel <= LANES and n_sel % 8 == 0
    onehot = (jnp.arange(seq)[:, None] // B_SEL_LEN == jnp.arange(LANES)[None, :]).astype(BF16)

    def kv_spec(j):
        return pl.BlockSpec((1, seq, HEAD_DIM), lambda b, gg: (b, 0, B_HEADS + j * g + gg))

    return pl.pallas_call(
        functools.partial(_nsa_kernel, tq=tq, seq=seq, n_cmp=n_cmp, n_sel=n_sel),
        grid=(batch, g),
        in_specs=[pl.BlockSpec((1, seq, rep * HEAD_DIM), lambda b, gg: (b, 0, gg)),
                  pl.BlockSpec((1, 1, n_chunk, HEAD_DIM), lambda b, gg: (b, gg, 0, 0)),
                  pl.BlockSpec((1, 1, n_chunk, HEAD_DIM), lambda b, gg: (batch + b, gg, 0, 0)),
                  kv_spec(2), kv_spec(3), kv_spec(4), kv_spec(5),
                  pl.BlockSpec((1, seq, LANES), lambda b, gg: (b, 0, gg)),
                  pl.BlockSpec((seq, LANES), lambda b, gg: (0, 0)),
                  pl.BlockSpec((seq, LANES), lambda b, gg: (0, 0)),
                  pl.BlockSpec((seq, LANES), lambda b, gg: (0, 0))],
        out_specs=pl.BlockSpec((1, seq, rep * HEAD_DIM), lambda b, gg: (b, 0, gg)),
        out_shape=jax.ShapeDtypeStruct((batch, seq, B_HEADS * HEAD_DIM), BF16),
        scratch_shapes=[pltpu.VMEM((HEAD_DIM, n_chunk), BF16), pltpu.VMEM((rep, tq, HEAD_DIM), F32),
                        pltpu.VMEM((seq, HEAD_DIM), BF16), pltpu.VMEM((seq, HEAD_DIM), BF16),
                        pltpu.VMEM((tq, rep * HEAD_DIM), BF16)],
        compiler_params=_cparams(("parallel", "parallel")),
        name="nsa_attn",
    )(proj3, kvc, kvc, proj3, proj3, proj3, proj3, gates.reshape(batch, seq, g * LANES), onehot, *tables)


def _scaled_bf16(w, n_cols, scale):
    col_scale = jnp.where(jnp.arange(w.shape[1]) < n_cols, scale, 1.0).astype(F32)
    return (w * col_scale[None, :]).astype(BF16)


def _mixer_b(hb, h, w_in, cmp_pos, cmp_w1, cmp_w2, w_out, ln, batch, seq):
    g, rep = B_KV_GROUPS, B_REP
    n_main = (B_HEADS + B_N_KV * g) * HEAD_DIM
    tables = _rope_tables(jnp.arange(seq), HEAD_DIM)
    roped = (list(range(B_HEADS)) + list(range(B_HEADS + 2 * g, B_HEADS + 3 * g))
             + list(range(B_HEADS + 4 * g, B_HEADS + 5 * g)))
    wb = _scaled_bf16(_permute_blocks(w_in, roped, HEAD_DIM), B_HEADS * HEAD_DIM, HEAD_DIM ** -0.5)
    proj3 = _proj(hb, wb[:, :n_main]).reshape(batch, seq, n_main)
    wg = wb[:, n_main:].reshape(-1, g, rep * 3)
    wg = jnp.pad(wg, ((0, 0), (0, 0), (0, LANES - rep * 3))).reshape(-1, g * LANES)
    gates = _gates(hb, wg)
    kvc = _compress(proj3, cmp_pos, cmp_w1, cmp_w2, batch, seq)
    o = _nsa_attn(proj3, kvc, gates, tables, batch, seq)
    return _outproj_ln(o.reshape(batch * seq, -1), w_out.astype(BF16), h, ln)


def _router_kernel(x_ref, w_ref, b_ref, o_ref):
    x = x_ref[...]
    xh = x.astype(BF16)
    xl = (x - xh.astype(F32)).astype(BF16)
    hi = _dot(xh, w_ref[...])
    logits = hi[:, :LANES] + (hi[:, LANES:] + _dot(xl, w_ref[:, :LANES])) + b_ref[...]
    lane = lax.broadcasted_iota(jnp.int32, logits.shape, 1)
    lane_f = lane.astype(F32)

    def first_max(vals):
        top = jnp.max(vals, axis=-1, keepdims=True)
        first = jnp.min(jnp.where(vals == top, lane_f, float(4 * LANES)), axis=-1, keepdims=True)
        return top, first.astype(jnp.int32)

    gl = jnp.where(lane < MOE_GROUPS, logits, -jnp.inf)
    gmax, gidx = first_max(gl)
    g_w = 1.0 / jnp.sum(jnp.exp(gl - gmax), axis=-1, keepdims=True)
    lo = MOE_GROUPS + MOE_EXPERTS_PER_GROUP * gidx
    ev = jnp.where((lane >= lo) & (lane < lo + MOE_EXPERTS_PER_GROUP), logits, -jnp.inf)
    v1, i1 = first_max(ev)
    v2, i2 = first_max(jnp.where(lane == i1, -jnp.inf, ev))
    e2 = jnp.exp(v2 - v1)
    w1 = g_w / (1.0 + e2)
    w2 = g_w * e2 / (1.0 + e2)
    out = jnp.where(lane == 0, (i1 - MOE_GROUPS).astype(F32),
                    jnp.where(lane == 1, (i2 - MOE_GROUPS).astype(F32),
                              jnp.where(lane == 2, w1, jnp.where(lane == 3, w2, 0.0))))
    o_ref[...] = out


def _router(h, router_w, router_b, tm=1024):
    m, d = h.shape
    n = router_w.shape[1]
    wp = jnp.pad(router_w, ((0, 0), (0, LANES - n)))
    wh = wp.astype(BF16)
    wp = jnp.concatenate([wh, (wp - wh.astype(F32)).astype(BF16)], axis=1)
    bp = jnp.pad(router_b, (0, LANES - n)).reshape(1, LANES)
    return pl.pallas_call(
        _router_kernel,
        grid=(m // tm,),
        in_specs=[pl.BlockSpec((tm, d), lambda i: (i, 0)),
                  pl.BlockSpec((d, 2 * LANES), lambda i: (0, 0)),
                  pl.BlockSpec((1, LANES), lambda i: (0, 0))],
        out_specs=pl.BlockSpec((tm, LANES), lambda i: (i, 0)),
        out_shape=jax.ShapeDtypeStruct((m, LANES), F32),
        compiler_params=_cparams(("parallel",)),
        name="moe_router",
    )(h, wp, bp)


def _experts_kernel(be_ref, nu_ref, nxt_ref, x_ref, wi_hbm, wo_hbm, *rest, blk0):
    y_ref, wi_buf, wo_buf, wi_scr, wo_scr, slot_ref, sems = rest[-7:]
    i = pl.program_id(0)
    blk = i + blk0
    expert = be_ref[blk]

    def weight_copies(e, slot):
        return (pltpu.make_async_copy(wi_hbm.at[e], wi_buf.at[slot], sems.at[0, slot]),
                pltpu.make_async_copy(wo_hbm.at[e], wo_buf.at[slot], sems.at[1, slot]))

    @pl.when(i == 0)
    def _():
        slot_ref[0] = 0
        for cp in weight_copies(expert, 0):
            cp.start()

    @pl.when((i == 0) | (expert != be_ref[jnp.maximum(blk - 1, 0)]))
    def _():
        slot = slot_ref[0]
        for cp in weight_copies(expert, slot):
            cp.wait()
        wi_scr[...] = wi_buf[slot].astype(BF16)
        wo_scr[...] = wo_buf[slot].astype(BF16)
        nxt = nxt_ref[blk]

        @pl.when(nxt >= 0)
        def _():
            for cp in weight_copies(nxt, 1 - slot):
                cp.start()

        slot_ref[0] = 1 - slot

    @pl.when(blk < nu_ref[0])
    def _():
        gu = _dot(x_ref[...], wi_scr[...])
        gate, up = gu[:, :MOE_D_FF], gu[:, MOE_D_FF:]
        act = gate * (1.0 / (1.0 + jnp.exp(-gate))) * up
        y_ref[...] = _dot(act.astype(BF16), wo_scr[...]).astype(y_ref.dtype)

    @pl.when(blk >= nu_ref[0])
    def _():
        y_ref[...] = jnp.zeros_like(y_ref)


def _experts(xb, blk_e, n_used, next_e, w_in, w_out, ybuf, blk0):
    rows, d = xb.shape
    n_total = blk_e.shape[0]
    in_specs = [pl.BlockSpec((MOE_ROWS, d), lambda i, be, nu, nx: (i, 0)),
                pl.BlockSpec(memory_space=pl.ANY),
                pl.BlockSpec(memory_space=pl.ANY)]
    args = [blk_e, n_used, next_e, xb, w_in, w_out]
    aliases = {}
    if ybuf is not None:
        in_specs.append(pl.BlockSpec(memory_space=pl.ANY))
        args.append(ybuf)
        aliases = {len(args) - 1: 0}
    return pl.pallas_call(
        functools.partial(_experts_kernel, blk0=blk0),
        grid_spec=pltpu.PrefetchScalarGridSpec(
            num_scalar_prefetch=3,
            grid=(rows // MOE_ROWS,),
            in_specs=in_specs,
            out_specs=pl.BlockSpec((MOE_ROWS, d), lambda i, be, nu, nx: (i + blk0, 0)),
            scratch_shapes=[pltpu.VMEM((2, d, 2 * MOE_D_FF), F32), pltpu.VMEM((2, MOE_D_FF, d), F32),
                            pltpu.VMEM((d, 2 * MOE_D_FF), BF16), pltpu.VMEM((MOE_D_FF, d), BF16),
                            pltpu.SMEM((1,), jnp.int32), pltpu.SemaphoreType.DMA((2, 2))]),
        out_shape=jax.ShapeDtypeStruct((n_total * MOE_ROWS, d), BF16),
        input_output_aliases=aliases,
        compiler_params=_cparams(("arbitrary",)),
        name="moe_experts",
    )(*args)


def _moe_ln_kernel(h_ref, y0_ref, y1_ref, r_ref, ln_ref, hf_ref, hb_ref):
    r = r_ref[...]
    f = y0_ref[...].astype(F32) * r[:, 2:3] + y1_ref[...].astype(F32) * r[:, 3:4]
    y = _layer_norm(DN_ALPHA * h_ref[...] + f, ln_ref[2:3, :], ln_ref[3:4, :])
    hf_ref[...] = y
    hb_ref[...] = y.astype(BF16)


def _moe_ln(h, y0, y1, route, ln, tm=512):
    m, d = h.shape
    row = lambda i: (i, 0)
    return pl.pallas_call(
        _moe_ln_kernel,
        grid=(m // tm,),
        in_specs=[pl.BlockSpec((tm, d), row), pl.BlockSpec((tm, d), row), pl.BlockSpec((tm, d), row),
                  pl.BlockSpec((tm, LANES), row), pl.BlockSpec((4, d), lambda i: (0, 0))],
        out_specs=[pl.BlockSpec((tm, d), row), pl.BlockSpec((tm, d), row)],
        out_shape=[jax.ShapeDtypeStruct((m, d), F32), jax.ShapeDtypeStruct((m, d), BF16)],
        compiler_params=_cparams(("parallel",)),
        name="moe_ln",
    )(h, y0, y1, route, ln)


def _moe(h, hb, router_w, router_b, w_in, w_out, ln):
    t, d = h.shape
    e, rows = MOE_EXPERTS, MOE_ROWS
    route = _router(h, router_w, router_b)
    flat_e = route[:, 0:2].astype(jnp.int32).reshape(-1)
    n_assign = flat_e.shape[0]
    n_blk = -(-n_assign // rows) + e
    onehot = (flat_e[:, None] == jnp.arange(e, dtype=jnp.int32)[None, :]).astype(jnp.int32)
    csum = jnp.cumsum(onehot, axis=0)
    rank = jnp.take_along_axis(csum, flat_e[:, None], axis=1)[:, 0] - 1
    sizes = csum[-1]
    padded = (sizes + rows - 1) // rows * rows
    pad_end = jnp.cumsum(padded)
    pad_start = pad_end - padded
    dest = pad_start[flat_e] + rank
    flat_tok = jnp.arange(n_assign, dtype=jnp.int32) // 2
    slot_tok = jnp.zeros((n_blk * rows,), jnp.int32).at[dest].set(flat_tok)
    blk_start = jnp.arange(n_blk, dtype=jnp.int32) * rows
    blk_e = jnp.minimum(jnp.sum((pad_end[None, :] <= blk_start[:, None]).astype(jnp.int32), axis=1), e - 1)
    n_used = (pad_end[-1] // rows).astype(jnp.int32).reshape(1)
    assert n_blk % MOE_CHUNKS == 0
    per = n_blk // MOE_CHUNKS
    blk_id = jnp.arange(n_blk, dtype=jnp.int32)
    run_end = pad_end[blk_e] // rows
    after = blk_e[jnp.minimum(run_end, n_blk - 1)]
    in_call = (run_end > blk_id) & (run_end < (blk_id // per + 1) * per) & (after != blk_e)
    next_e = jnp.where(in_call, after, -1).astype(jnp.int32)
    yb = None
    for c in range(MOE_CHUNKS):
        xb = hb[slot_tok[c * per * rows:(c + 1) * per * rows]]
        yb = _experts(xb, blk_e, n_used, next_e, w_in, w_out, yb, c * per)
    dest2 = dest.reshape(t, 2)
    return _moe_ln(h, yb[dest2[:, 0]], yb[dest2[:, 1]], route, ln)


def kernel(x, l0_a_w_in, l0_a_lam, l0_a_subln, l0_a_w_out, l0_ln, l0_router_w, l0_router_b, l0_moe_w_in, l0_moe_w_out, l1_b_w_in, l1_b_cmp_pos, l1_b_cmp_w1, l1_b_cmp_w2, l1_b_w_out, l1_ln, l1_router_w, l1_router_b, l1_moe_w_in, l1_moe_w_out, l2_c_w_in, l2_c_w_out, l2_ln, l2_router_w, l2_router_b, l2_moe_w_in, l2_moe_w_out, l3_a_w_in, l3_a_lam, l3_a_subln, l3_a_w_out, l3_ln, l3_router_w, l3_router_b, l3_moe_w_in, l3_moe_w_out):
    batch, seq, d = x.shape
    h = x.reshape(batch * seq, d)
    hb = h.astype(BF16)
    h, hb = _mixer_a(hb, h, l0_a_w_in, l0_a_lam, l0_a_subln, l0_a_w_out, l0_ln, 0, batch, seq)
    h, hb = _moe(h, hb, l0_router_w, l0_router_b, l0_moe_w_in, l0_moe_w_out, l0_ln)
    h, hb = _mixer_b(hb, h, l1_b_w_in, l1_b_cmp_pos, l1_b_cmp_w1, l1_b_cmp_w2, l1_b_w_out, l1_ln, batch, seq)
    h, hb = _moe(h, hb, l1_router_w, l1_router_b, l1_moe_w_in, l1_moe_w_out, l1_ln)
    h, hb = _mixer_c(hb, h, l2_c_w_in, l2_c_w_out, l2_ln, batch, seq)
    h, hb = _moe(h, hb, l2_router_w, l2_router_b, l2_moe_w_in, l2_moe_w_out, l2_ln)
    h, hb = _mixer_a(hb, h, l3_a_w_in, l3_a_lam, l3_a_subln, l3_a_w_out, l3_ln, 3, batch, seq)
    h, hb = _moe(h, hb, l3_router_w, l3_router_b, l3_moe_w_in, l3_moe_w_out, l3_ln)
    return h.reshape(batch, seq, d)
```

```python
import functools
import math

import jax
import jax.numpy as jnp
import numpy as np
from jax import lax
from jax.experimental import pallas as pl
from jax.experimental.pallas import tpu as pltpu

D_MODEL = 2048
DEPTH = 4
HEAD_DIM = 128
ROPE_THETA = 500000.0
ROPE_FRACTION = 4
NEG_FILL = -1e30

A_HEADS = D_MODEL // HEAD_DIM
A_QK_DIM = HEAD_DIM // 2

B_HEADS = D_MODEL // HEAD_DIM
B_KV_GROUPS = 4
B_REP = B_HEADS // B_KV_GROUPS
B_CMP_LEN = 32
B_CMP_STRIDE = 16
B_SEL_LEN = 64
B_SEL_TOPN = 16
B_WINDOW = 512
B_N_KV = 6

C_PATTERNS = ((128, 1), (512, 4), (2048, 16))
C_HEADS_PER_GROUP = D_MODEL // (2 * HEAD_DIM)

MOE_GROUPS = 4
MOE_EXPERTS_PER_GROUP = 8
MOE_EXPERTS = MOE_GROUPS * MOE_EXPERTS_PER_GROUP
MOE_D_FF = D_MODEL // 4
MOE_ROWS = 256
MOE_CHUNKS = 2

DN_ALPHA = (2 * DEPTH) ** 0.25
NORM_EPS = 1e-5

LANES = 128
VMEM_LIMIT = 56 * 1024 * 1024

BF16 = jnp.bfloat16
F32 = jnp.float32


def _cparams(sem):
    return pltpu.CompilerParams(dimension_semantics=sem, vmem_limit_bytes=VMEM_LIMIT)


def _dot(a, b):
    return jnp.dot(a, b, preferred_element_type=F32)


def _dot_nt(a, b):
    return lax.dot_general(a, b, (((1,), (1,)), ((), ())), preferred_element_type=F32)


HALF_LANES = LANES // 2


def _rope_layout(dim):
    n_sub = LANES // dim
    w = HALF_LANES // n_sub
    half = dim // ROPE_FRACTION // 2
    perm, kind, freq = np.zeros(LANES, np.int32), np.zeros(LANES, np.int32), np.zeros(LANES, np.int32)
    for s in range(n_sub):
        lanes = list(range(s * w, (s + 1) * w)) + list(range(HALF_LANES + s * w, HALF_LANES + (s + 1) * w))
        x1, x2 = lanes[:half], lanes[w:w + half]
        rest = [l for l in lanes if l not in x1 and l not in x2]
        for i, l in enumerate(x1):
            perm[l], kind[l], freq[l] = s * dim + i, 1, i
        for i, l in enumerate(x2):
            perm[l], kind[l], freq[l] = s * dim + half + i, 2, i
        for i, l in enumerate(rest):
            perm[l] = s * dim + 2 * half + i
    return perm, kind, freq


def _rope_tables(pos, dim):
    rot = dim // ROPE_FRACTION
    _, kind, freq = _rope_layout(dim)
    inv_freq = ROPE_THETA ** (-jnp.arange(0, rot, 2, dtype=F32) / rot)
    ang = pos.astype(F32)[:, None] * inv_freq[None, :]
    cos, sin = jnp.cos(ang)[:, freq], jnp.sin(ang)[:, freq]
    c = jnp.where(kind[None, :] > 0, cos, 1.0)
    ss = jnp.where(kind[None, :] == 1, -sin, jnp.where(kind[None, :] == 2, sin, 0.0))
    return c, ss


def _permute_blocks(w, blocks, dim):
    perm, _, _ = _rope_layout(dim)
    idx = np.arange(w.shape[1], dtype=np.int32)
    for b in blocks:
        idx[b * LANES:(b + 1) * LANES] = b * LANES + perm
    return w[:, idx]


def _apply_rope(a, c, ss):
    return a * c + pltpu.roll(a, HALF_LANES, 1) * ss


def _roped(x, c_ref, ss_ref, rows=slice(None)):
    return _apply_rope(x.astype(F32), c_ref[rows, :], ss_ref[rows, :]).astype(BF16)


def _proj_kernel(x_ref, w_ref, o_ref):
    o_ref[...] = _dot(x_ref[...], w_ref[...]).astype(o_ref.dtype)


def _proj(xb, w, tm=1024, tn=1024):
    m, k = xb.shape
    n = w.shape[1]
    assert m % tm == 0 and n % tn == 0
    return pl.pallas_call(
        _proj_kernel,
        grid=(m // tm, n // tn),
        in_specs=[pl.BlockSpec((tm, k), lambda i, j: (i, 0)),
                  pl.BlockSpec((k, tn), lambda i, j: (0, j))],
        out_specs=pl.BlockSpec((tm, tn), lambda i, j: (i, j)),
        out_shape=jax.ShapeDtypeStruct((m, n), BF16),
        compiler_params=_cparams(("parallel", "arbitrary")),
        name="proj",
    )(xb, w)


def _layer_norm(z, g, b):
    mu = jnp.mean(z, axis=-1, keepdims=True)
    zc = z - mu
    var = jnp.mean(zc * zc, axis=-1, keepdims=True)
    return zc * lax.rsqrt(var + NORM_EPS) * g + b


def _outproj_ln_kernel(o_ref, w_ref, h_ref, ln_ref, hf_ref, hb_ref):
    half = o_ref.shape[0] // 2
    for s in range(2):
        rows = slice(s * half, (s + 1) * half)
        m = _dot(o_ref[rows, :], w_ref[...])
        y = _layer_norm(DN_ALPHA * h_ref[rows, :] + m, ln_ref[0:1, :], ln_ref[1:2, :])
        hf_ref[rows, :] = y
        hb_ref[rows, :] = y.astype(BF16)


def _outproj_ln(o, w, h, ln, tm=512):
    m, k = o.shape
    d = w.shape[1]
    return pl.pallas_call(
        _outproj_ln_kernel,
        grid=(m // tm,),
        in_specs=[pl.BlockSpec((tm, k), lambda i: (i, 0)),
                  pl.BlockSpec((k, d), lambda i: (0, 0)),
                  pl.BlockSpec((tm, d), lambda i: (i, 0)),
                  pl.BlockSpec((4, d), lambda i: (0, 0))],
        out_specs=[pl.BlockSpec((tm, d), lambda i: (i, 0)),
                   pl.BlockSpec((tm, d), lambda i: (i, 0))],
        out_shape=[jax.ShapeDtypeStruct((m, d), F32), jax.ShapeDtypeStruct((m, d), BF16)],
        compiler_params=_cparams(("parallel",)),
        name="outproj_ln",
    )(o, w, h, ln)


def _causal_block(q, k, v, c, tq):
    s = _dot_nt(q, k)
    rows = q.shape[0]
    assert tq & (tq - 1) == 0
    pos_in_block = jnp.bitwise_and(lax.broadcasted_iota(jnp.int32, (rows, tq), 0), tq - 1)
    causal = lax.broadcasted_iota(jnp.int32, (rows, tq), 1) <= pos_in_block
    s_diag = jnp.where(causal, s[:, c * tq:], -jnp.inf)
    m = jnp.max(s_diag, axis=-1, keepdims=True)
    if c > 0:
        s_off = s[:, :c * tq]
        m = jnp.maximum(m, jnp.max(s_off, axis=-1, keepdims=True))
        e = jnp.concatenate([jnp.exp(s_off - m), jnp.exp(s_diag - m)], axis=1)
    else:
        e = jnp.exp(s_diag - m)
    inv = 1.0 / jnp.sum(e, axis=-1, keepdims=True)
    return _dot(e.astype(BF16), v) * inv


def _attn_a_kernel(lam_ref, q_ref, k_ref, v_ref, g_ref, c_ref, ss_ref, o_ref, k_scr, *, tq, seq, out_scale):
    k_scr[...] = _roped(k_ref[0], c_ref, ss_ref)
    for c in range(seq // tq):
        kv = (c + 1) * tq
        q = _roped(q_ref[0, c * tq:kv, :], c_ref, ss_ref, slice(c * tq, kv))
        q = q * jnp.asarray(A_QK_DIM ** -0.5, BF16)
        lane = lax.broadcasted_iota(jnp.int32, q.shape, 1)
        zero = jnp.zeros_like(q)
        first = jnp.bitwise_and(lane, HALF_LANES // (LANES // A_QK_DIM)) == 0
        q12 = jnp.concatenate([jnp.where(first, q, zero), jnp.where(first, zero, q)], axis=0)
        both = _causal_block(q12, k_scr[:kv, :], v_ref[0, :kv, :], c, tq)
        o = both[:tq] - lam_ref[0] * both[tq:]
        o = o * lax.rsqrt(jnp.mean(o * o, axis=-1, keepdims=True) + NORM_EPS) * (g_ref[...] * out_scale)
        o_ref[0, c * tq:kv, :] = o.astype(o_ref.dtype)


def _attn_a(proj, lam_full, subln, tables, batch, seq, lam_init, tq=256):
    h = A_HEADS
    tab_spec = pl.BlockSpec((seq, LANES), lambda b, hh, s: (0, 0))
    return pl.pallas_call(
        functools.partial(_attn_a_kernel, tq=tq, seq=seq, out_scale=1.0 - lam_init),
        grid_spec=pltpu.PrefetchScalarGridSpec(
            num_scalar_prefetch=1,
            grid=(batch, h),
            in_specs=[pl.BlockSpec((1, seq, HEAD_DIM), lambda b, hh, s: (b, 0, hh)),
                      pl.BlockSpec((1, seq, HEAD_DIM), lambda b, hh, s: (b, 0, h + hh)),
                      pl.BlockSpec((1, seq, HEAD_DIM), lambda b, hh, s: (b, 0, 2 * h + hh)),
                      pl.BlockSpec((1, HEAD_DIM), lambda b, hh, s: (0, 0)),
                      tab_spec, tab_spec],
            out_specs=pl.BlockSpec((1, seq, HEAD_DIM), lambda b, hh, s: (b, 0, hh)),
            scratch_shapes=[pltpu.VMEM((seq, HEAD_DIM), BF16)]),
        out_shape=jax.ShapeDtypeStruct((batch, seq, h * HEAD_DIM), BF16),
        compiler_params=_cparams(("parallel", "parallel")),
        name="attn_a",
    )(lam_full, proj, proj, proj, subln, *tables)


def _mixer_a(hb, h, w_in, lam, subln, w_out, ln, layer_idx, batch, seq):
    tables = _rope_tables(jnp.arange(seq), A_QK_DIM)
    wb = _permute_blocks(w_in, range(2 * A_HEADS), A_QK_DIM).astype(BF16)
    proj = _proj(hb, wb)
    lam_init = 0.8 - 0.6 * math.exp(-0.3 * layer_idx)
    lf = lam.astype(F32)
    lam_full = jnp.exp(jnp.sum(lf[0] * lf[1])) - jnp.exp(jnp.sum(lf[2] * lf[3])) + lam_init
    o = _attn_a(proj.reshape(batch, seq, -1), lam_full.reshape(1), subln.reshape(1, HEAD_DIM), tables, batch, seq,
                lam_init)
    return _outproj_ln(o.reshape(batch * seq, -1), w_out.astype(BF16), h, ln)


DIL_BLOCK = 128


def _band_block(q, k, v, q0, k0, span):
    s = _dot_nt(q, k)
    diff = (q0 + lax.broadcasted_iota(jnp.int32, s.shape, 0)) - (k0 + lax.broadcasted_iota(jnp.int32, s.shape, 1))
    s = jnp.where((diff >= 0) & (diff <= span), s, -jnp.inf)
    m = jnp.max(s, axis=-1, keepdims=True)
    e = jnp.exp(s - m)
    den = jnp.sum(e, axis=-1, keepdims=True)
    return _dot((e * (1.0 / den)).astype(BF16), v), m + jnp.log(den)


def _dil_kernel(*refs, seq):
    n_grp = len(C_PATTERNS)
    qkv = refs[:3 * n_grp]
    c_ref, ss_ref, o_ref = refs[3 * n_grp:3 * n_grp + 3]
    stage, o_scr, l_scr = refs[3 * n_grp + 3:]
    blk = DIL_BLOCK
    for g, (window, dil) in enumerate(C_PATTERNS):
        q_ref, k_ref, v_ref = qkv[3 * g:3 * g + 3]
        length = seq // dil
        span = window // dil
        n_blk = length // blk
        stage[0] = _apply_rope(q_ref[0].astype(F32), c_ref[...], ss_ref[...])
        stage[1] = _apply_rope(k_ref[0].astype(F32), c_ref[...], ss_ref[...])
        if dil > 1:
            stage[2] = v_ref[0].astype(F32)

        def rows(j, ref, r, start, size):
            if dil == 1:
                return ref[0, start:start + size, :] if j == 2 else stage[j, start:start + size, :].astype(BF16)
            return stage[j, pl.ds(r + start * dil, size, stride=dil), :].astype(BF16)

        for r in range(dil):
            for i in range(n_blk):
                k0 = max(i - 1, 0) * blk
                kl = (i + 1) * blk - k0
                o, lse = _band_block(rows(0, q_ref, r, i * blk, blk), rows(1, k_ref, r, k0, kl),
                                     rows(2, v_ref, r, k0, kl), i * blk, k0, span)
                lse = jnp.broadcast_to(lse, (blk, HEAD_DIM))
                if dil == 1:
                    o_scr[g, i * blk:(i + 1) * blk, :] = o
                    l_scr[g, i * blk:(i + 1) * blk, :] = lse
                else:
                    o_scr[g, pl.ds(r + i * blk * dil, blk, stride=dil), :] = o
                    l_scr[g, pl.ds(r + i * blk * dil, blk, stride=dil), :] = lse

    chunk = 256
    for c in range(seq // chunk):
        sl = slice(c * chunk, (c + 1) * chunk)
        ls = [l_scr[g, sl, :] for g in range(n_grp)]
        m = functools.reduce(jnp.maximum, ls)
        es = [jnp.exp(l - m) for l in ls]
        inv = 1.0 / functools.reduce(lambda a, b: a + b, es)
        o = functools.reduce(lambda a, b: a + b, [(es[g] * inv) * o_scr[g, sl, :] for g in range(n_grp)])
        o_ref[0, sl, :] = o.astype(o_ref.dtype)


def _dil_attn(proj, tables, batch, seq):
    hg, p = C_HEADS_PER_GROUP, len(C_PATTERNS)

    def spec(kind, g):
        return pl.BlockSpec((1, seq, HEAD_DIM), lambda b, hh: (b, 0, kind * p * hg + g * hg + hh))

    in_specs = [spec(kind, g) for g in range(p) for kind in range(3)]
    tab_spec = pl.BlockSpec((seq, LANES), lambda b, hh: (0, 0))
    return pl.pallas_call(
        functools.partial(_dil_kernel, seq=seq),
        grid=(batch, hg),
        in_specs=in_specs + [tab_spec, tab_spec],
        out_specs=pl.BlockSpec((1, seq, HEAD_DIM), lambda b, hh: (b, 0, hh)),
        out_shape=jax.ShapeDtypeStruct((batch, seq, hg * HEAD_DIM), BF16),
        scratch_shapes=[pltpu.VMEM((3, seq, HEAD_DIM), F32), pltpu.VMEM((p, seq, HEAD_DIM), F32),
                        pltpu.VMEM((p, seq, HEAD_DIM), F32)],
        compiler_params=_cparams(("parallel", "parallel")),
        name="dil_attn",
    )(*([proj] * len(in_specs)), *tables)


def _mixer_c(hb, h, w_in, w_out, ln, batch, seq):
    tables = _rope_tables(jnp.arange(seq), HEAD_DIM)
    n = w_in.shape[1]
    wb = _scaled_bf16(_permute_blocks(w_in, range(2 * n // 3 // LANES), HEAD_DIM), n // 3, HEAD_DIM ** -0.5)
    proj = _proj(hb, wb)
    o = _dil_attn(proj.reshape(batch, seq, n), tables, batch, seq)
    return _outproj_ln(o.reshape(batch * seq, -1), w_out.astype(BF16), h, ln)


def _gates_kernel(x_ref, w_ref, o_ref):
    z = _dot(x_ref[...], w_ref[...])
    o_ref[...] = 1.0 / (1.0 + jnp.exp(-z))


def _gates(xb, wg, tm=1024):
    m, k = xb.shape
    n = wg.shape[1]
    return pl.pallas_call(
        _gates_kernel,
        grid=(m // tm,),
        in_specs=[pl.BlockSpec((tm, k), lambda i: (i, 0)), pl.BlockSpec((k, n), lambda i: (0, 0))],
        out_specs=pl.BlockSpec((tm, n), lambda i: (i, 0)),
        out_shape=jax.ShapeDtypeStruct((m, n), F32),
        compiler_params=_cparams(("parallel",)),
        name="nsa_gates",
    )(xb, wg)


def _gelu_tanh(x):
    return 0.5 * x * (1.0 + jnp.tanh(math.sqrt(2.0 / math.pi) * (x + 0.044715 * (x * x * x))))


def _compress_kernel(a_ref, pos_ref, w1_ref, w2_ref, c_ref, ss_ref, o_ref, stage):
    kind = pl.program_id(2)
    stage[...] = a_ref[0].astype(F32)
    n_chunk = stage.shape[0] // B_CMP_STRIDE
    first = jnp.zeros((n_chunk, HEAD_DIM), F32)
    second = jnp.zeros((n_chunk, HEAD_DIM), F32)
    for t in range(B_CMP_STRIDE):
        a = stage[pl.ds(t, n_chunk, stride=B_CMP_STRIDE), :]
        u = B_CMP_STRIDE + t
        first = first + _dot((a + pos_ref[0, t:t + 1, :]).astype(BF16), w1_ref[0, t * HEAD_DIM:(t + 1) * HEAD_DIM, :])
        second = second + _dot((a + pos_ref[0, u:u + 1, :]).astype(BF16), w1_ref[0, u * HEAD_DIM:(u + 1) * HEAD_DIM, :])
    hmid = first + pltpu.roll(second, n_chunk - 1, 0)
    out = _dot(_gelu_tanh(hmid).astype(BF16), w2_ref[0])

    @pl.when(kind == 0)
    def _():
        o_ref[0, 0] = _apply_rope(out, c_ref[...], ss_ref[...]).astype(o_ref.dtype)

    @pl.when(kind != 0)
    def _():
        o_ref[0, 0] = out.astype(o_ref.dtype)


def _compress(proj3, cmp_pos, cmp_w1, cmp_w2, batch, seq):
    g = B_KV_GROUPS
    n_chunk = seq // B_CMP_STRIDE
    cmp_end = jnp.arange(n_chunk) * B_CMP_STRIDE + B_CMP_LEN - 1
    tables = _rope_tables(cmp_end, HEAD_DIM)
    tab_spec = pl.BlockSpec((n_chunk, LANES), lambda b, gg, kk: (0, 0))
    perm, _, _ = _rope_layout(HEAD_DIM)
    cmp_w2 = jnp.stack([cmp_w2[0][:, perm], cmp_w2[1]])
    return pl.pallas_call(
        _compress_kernel,
        grid=(batch, g, 2),
        in_specs=[pl.BlockSpec((1, seq, HEAD_DIM), lambda b, gg, kk: (b, 0, B_HEADS + kk * g + gg)),
                  pl.BlockSpec((1, B_CMP_LEN, HEAD_DIM), lambda b, gg, kk: (kk, 0, 0)),
                  pl.BlockSpec((1, B_CMP_LEN * HEAD_DIM, HEAD_DIM), lambda b, gg, kk: (kk, 0, 0)),
                  pl.BlockSpec((1, HEAD_DIM, HEAD_DIM), lambda b, gg, kk: (kk, 0, 0)),
                  tab_spec, tab_spec],
        out_specs=pl.BlockSpec((1, 1, n_chunk, HEAD_DIM), lambda b, gg, kk: (kk * batch + b, gg, 0, 0)),
        out_shape=jax.ShapeDtypeStruct((2 * batch, g, n_chunk, HEAD_DIM), BF16),
        scratch_shapes=[pltpu.VMEM((seq, HEAD_DIM), F32)],
        compiler_params=_cparams(("parallel", "parallel", "arbitrary")),
        name="nsa_compress",
    )(proj3, cmp_pos, cmp_w1.astype(BF16), cmp_w2.astype(BF16), *tables)


def _split3(x):
    hi = x.astype(BF16)
    r = x - hi.astype(F32)
    mid = r.astype(BF16)
    lo = (r - mid.astype(F32)).astype(BF16)
    return hi, mid, lo


def _nsa_kernel(q_ref, kc_ref, vc_ref, ks_ref, vs_ref, kw_ref, vw_ref, gate_ref, ex_ref, c_ref, ss_ref, o_ref,
                vct_scr, ocmp_scr, ks_scr, kw_scr, q_scr, *, tq, seq, n_cmp, n_sel):
    rep = B_REP
    big = 1e30
    vct_scr[...] = vc_ref[0, 0].astype(F32).T.astype(BF16)
    ks_scr[...] = _roped(ks_ref[0], c_ref, ss_ref)
    kw_scr[...] = _roped(kw_ref[0], c_ref, ss_ref)
    kc = kc_ref[0, 0]
    lane = lax.broadcasted_iota(jnp.int32, (tq, LANES), 1)
    n_back = -(-(B_WINDOW - 1) // tq)
    pair_rows = 2 * tq
    wrow = jnp.bitwise_and(lax.broadcasted_iota(jnp.int32, (pair_rows, tq), 0), tq - 1)
    wcol = lax.broadcasted_iota(jnp.int32, (pair_rows, tq), 1)
    win_bias = []
    for back in range(n_back + 1):
        if back * tq - (tq - 1) >= 0 and back * tq + (tq - 1) <= B_WINDOW - 1:
            win_bias.append(None)
        else:
            diff = back * tq + wrow - wcol
            win_bias.append(jnp.where((diff >= 0) & (diff <= B_WINDOW - 1), 0.0, -jnp.inf))

    for c in range(seq // tq):
        lo, hi = c * tq, (c + 1) * tq

        qpos = lo + lax.broadcasted_iota(jnp.int32, (1, tq), 1)
        crow = lax.broadcasted_iota(jnp.int32, (LANES, tq), 0)
        cmask = (crow * B_CMP_STRIDE + (B_CMP_LEN - 1) <= qpos) & (crow < n_cmp)

        p_sum = jnp.zeros((LANES, tq), F32)
        for r in range(rep):
            cols = slice(r * HEAD_DIM, (r + 1) * HEAD_DIM)
            q_scr[:, cols] = _roped(q_ref[0, lo:hi, cols], c_ref, ss_ref, slice(lo, hi))
            s = jnp.where(cmask, _dot_nt(kc, q_scr[:, cols]), NEG_FILL)
            e = jnp.exp(s - jnp.max(s, axis=0, keepdims=True))
            p = jnp.where(cmask, e * (1.0 / jnp.sum(e, axis=0, keepdims=True)), 0.0)
            ocmp_scr[r] = _dot(vct_scr[...], p.astype(BF16)).T
            p_sum = p_sum + p

        ni = lax.broadcasted_iota(jnp.int32, (LANES, LANES), 0)
        ci = lax.broadcasted_iota(jnp.int32, (LANES, LANES), 1)
        per = B_SEL_LEN // B_CMP_STRIDE
        cover = ((ci >= per * ni - (B_CMP_LEN // B_CMP_STRIDE - 1)) & (ci < per * ni + per) & (ci < n_cmp) & (ni < n_sel))
        cover = jnp.where(cover, 1.0, 0.0).astype(BF16)
        importance = sum(_dot(cover, part) for part in _split3(p_sum))[:n_sel]
        nrow = lax.broadcasted_iota(jnp.int32, (n_sel, tq), 0)
        cur = jnp.right_shift(qpos, int(math.log2(B_SEL_LEN)))
        forced = (nrow == 0) | (nrow == cur) | (nrow == cur - 1)
        score = jnp.where(forced, big, importance)
        score = jnp.where(nrow <= cur, score, -big)
        rank = jnp.zeros((n_sel, tq), jnp.int32)
        for mm in range(min(n_sel, (hi - 1) // B_SEL_LEN + 1)):
            other = score[mm:mm + 1, :]
            ahead = (other > score) | ((other == score) & (nrow > mm))
            rank = rank + jnp.where(ahead, 1, 0)
        sel_bias = jnp.where(rank < min(B_SEL_TOPN, n_sel), 0.0, -big)
        sel_bias = jnp.concatenate([sel_bias, jnp.zeros((LANES - n_sel, tq), F32)], axis=0).T.astype(BF16)

        k_aug = jnp.concatenate([ks_scr[:hi, :], ex_ref[:hi, :]], axis=1)
        v_sel = vs_ref[0, :hi, :]
        wlo = max(c - n_back, 0) * tq
        k_win, v_win = kw_scr[wlo:hi, :], vw_ref[0, wlo:hi, :]
        n_wchunks = (hi - wlo) // tq
        gates = gate_ref[0, lo:hi, :]
        bias2 = jnp.concatenate([sel_bias, sel_bias], axis=0)

        def head_pair(pr, carry):
            q2 = q_scr[:, pl.ds(pl.multiple_of(pr * 2 * HEAD_DIM, 2 * HEAD_DIM), 2 * HEAD_DIM)]
            q = jnp.concatenate([q2[:, :HEAD_DIM], q2[:, HEAD_DIM:]], axis=0)
            o_sel = _causal_block(jnp.concatenate([q, bias2], axis=1), k_aug, v_sel, c, tq)
            s = _dot_nt(q, k_win)
            pieces = []
            for j in range(n_wchunks):
                piece, bias = s[:, j * tq:(j + 1) * tq], win_bias[n_wchunks - 1 - j]
                pieces.append(piece if bias is None else piece + bias)
            s = jnp.concatenate(pieces, axis=1)
            e = jnp.exp(s - jnp.max(s, axis=-1, keepdims=True))
            o_win = _dot(e.astype(BF16), v_win) * (1.0 / jnp.sum(e, axis=-1, keepdims=True))
            outs = []
            for j in range(2):
                r = pr * 2 + j

                def gate(b):
                    return jnp.sum(jnp.where(lane == r * 3 + b, gates, 0.0), axis=-1, keepdims=True)

                rows = slice(j * tq, (j + 1) * tq)
                outs.append(gate(0) * ocmp_scr[r] + gate(1) * o_sel[rows] + gate(2) * o_win[rows])
            o_ref[0, lo:hi, pl.ds(pl.multiple_of(pr * 2 * HEAD_DIM, 2 * HEAD_DIM), 2 * HEAD_DIM)] = (
                jnp.concatenate(outs, axis=1).astype(o_ref.dtype))
            return carry

        lax.fori_loop(0, rep // 2, head_pair, 0)


def _nsa_attn(proj3, kvc, gates, tables, batch, seq, tq=256):
    g, rep = B_KV_GROUPS, B_REP
    n_chunk = seq // B_CMP_STRIDE
    n_cmp = n_chunk - B_CMP_LEN // B_CMP_STRIDE + 1
    n_sel = seq // B_SEL_LEN
    assert n_chunk == LANES and n_sel <= LANES and n_sel % 8 == 0
    onehot = (jnp.arange(seq)[:, None] // B_SEL_LEN == jnp.arange(LANES)[None, :]).astype(BF16)

    def kv_spec(j):
        return pl.BlockSpec((1, seq, HEAD_DIM), lambda b, gg: (b, 0, B_HEADS + j * g + gg))

    return pl.pallas_call(
        functools.partial(_nsa_kernel, tq=tq, seq=seq, n_cmp=n_cmp, n_sel=n_sel),
        grid=(batch, g),
        in_specs=[pl.BlockSpec((1, seq, rep * HEAD_DIM), lambda b, gg: (b, 0, gg)),
                  pl.BlockSpec((1, 1, n_chunk, HEAD_DIM), lambda b, gg: (b, gg, 0, 0)),
                  pl.BlockSpec((1, 1, n_chunk, HEAD_DIM), lambda b, gg: (batch + b, gg, 0, 0)),
                  kv_spec(2), kv_spec(3), kv_spec(4), kv_spec(5),
                  pl.BlockSpec((1, seq, LANES), lambda b, gg: (b, 0, gg)),
                  pl.BlockSpec((seq, LANES), lambda b, gg: (0, 0)),
                  pl.BlockSpec((seq, LANES), lambda b, gg: (0, 0)),
                  pl.BlockSpec((seq, LANES), lambda b, gg: (0, 0))],
        out_specs=pl.BlockSpec((1, seq, rep * HEAD_DIM), lambda b, gg: (b, 0, gg)),
        out_shape=jax.ShapeDtypeStruct((batch, seq, B_HEADS * HEAD_DIM), BF16),
        scratch_shapes=[pltpu.VMEM((HEAD_DIM, n_chunk), BF16), pltpu.VMEM((rep, tq, HEAD_DIM), F32),
                        pltpu.VMEM((seq, HEAD_DIM), BF16), pltpu.VMEM((seq, HEAD_DIM), BF16),
                        pltpu.VMEM((tq, rep * HEAD_DIM), BF16)],
        compiler_params=_cparams(("parallel", "parallel")),
        name="nsa_attn",
    )(proj3, kvc, kvc, proj3, proj3, proj3, proj3, gates.reshape(batch, seq, g * LANES), onehot, *tables)


def _scaled_bf16(w, n_cols, scale):
    col_scale = jnp.where(jnp.arange(w.shape[1]) < n_cols, scale, 1.0).astype(F32)
    return (w * col_scale[None, :]).astype(BF16)


def _mixer_b(hb, h, w_in, cmp_pos, cmp_w1, cmp_w2, w_out, ln, batch, seq):
    g, rep = B_KV_GROUPS, B_REP
    n_main = (B_HEADS + B_N_KV * g) * HEAD_DIM
    tables = _rope_tables(jnp.arange(seq), HEAD_DIM)
    roped = (list(range(B_HEADS)) + list(range(B_HEADS + 2 * g, B_HEADS + 3 * g))
             + list(range(B_HEADS + 4 * g, B_HEADS + 5 * g)))
    wb = _scaled_bf16(_permute_blocks(w_in, roped, HEAD_DIM), B_HEADS * HEAD_DIM, HEAD_DIM ** -0.5)
    proj3 = _proj(hb, wb[:, :n_main]).reshape(batch, seq, n_main)
    wg = wb[:, n_main:].reshape(-1, g, rep * 3)
    wg = jnp.pad(wg, ((0, 0), (0, 0), (0, LANES - rep * 3))).reshape(-1, g * LANES)
    gates = _gates(hb, wg)
    kvc = _compress(proj3, cmp_pos, cmp_w1, cmp_w2, batch, seq)
    o = _nsa_attn(proj3, kvc, gates, tables, batch, seq)
    return _outproj_ln(o.reshape(batch * seq, -1), w_out.astype(BF16), h, ln)


def _router_kernel(x_ref, w_ref, b_ref, o_ref):
    x = x_ref[...]
    xh = x.astype(BF16)
    xl = (x - xh.astype(F32)).astype(BF16)
    hi = _dot(xh, w_ref[...])
    logits = hi[:, :LANES] + (hi[:, LANES:] + _dot(xl, w_ref[:, :LANES])) + b_ref[...]
    lane = lax.broadcasted_iota(jnp.int32, logits.shape, 1)
    lane_f = lane.astype(F32)

    def first_max(vals):
        top = jnp.max(vals, axis=-1, keepdims=True)
        first = jnp.min(jnp.where(vals == top, lane_f, float(4 * LANES)), axis=-1, keepdims=True)
        return top, first.astype(jnp.int32)

    gl = jnp.where(lane < MOE_GROUPS, logits, -jnp.inf)
    gmax, gidx = first_max(gl)
    g_w = 1.0 / jnp.sum(jnp.exp(gl - gmax), axis=-1, keepdims=True)
    lo = MOE_GROUPS + MOE_EXPERTS_PER_GROUP * gidx
    ev = jnp.where((lane >= lo) & (lane < lo + MOE_EXPERTS_PER_GROUP), logits, -jnp.inf)
    v1, i1 = first_max(ev)
    v2, i2 = first_max(jnp.where(lane == i1, -jnp.inf, ev))
    e2 = jnp.exp(v2 - v1)
    w1 = g_w / (1.0 + e2)
    w2 = g_w * e2 / (1.0 + e2)
    out = jnp.where(lane == 0, (i1 - MOE_GROUPS).astype(F32),
                    jnp.where(lane == 1, (i2 - MOE_GROUPS).astype(F32),
                              jnp.where(lane == 2, w1, jnp.where(lane == 3, w2, 0.0))))
    o_ref[...] = out


def _router(h, router_w, router_b, tm=1024):
    m, d = h.shape
    n = router_w.shape[1]
    wp = jnp.pad(router_w, ((0, 0), (0, LANES - n)))
    wh = wp.astype(BF16)
    wp = jnp.concatenate([wh, (wp - wh.astype(F32)).astype(BF16)], axis=1)
    bp = jnp.pad(router_b, (0, LANES - n)).reshape(1, LANES)
    return pl.pallas_call(
        _router_kernel,
        grid=(m // tm,),
        in_specs=[pl.BlockSpec((tm, d), lambda i: (i, 0)),
                  pl.BlockSpec((d, 2 * LANES), lambda i: (0, 0)),
                  pl.BlockSpec((1, LANES), lambda i: (0, 0))],
        out_specs=pl.BlockSpec((tm, LANES), lambda i: (i, 0)),
        out_shape=jax.ShapeDtypeStruct((m, LANES), F32),
        compiler_params=_cparams(("parallel",)),
        name="moe_router",
    )(h, wp, bp)


def _experts_kernel(be_ref, nu_ref, nxt_ref, x_ref, wi_hbm, wo_hbm, *rest, blk0):
    y_ref, wi_buf, wo_buf, wi_scr, wo_scr, slot_ref, sems = rest[-7:]
    i = pl.program_id(0)
    blk = i + blk0
    expert = be_ref[blk]

    def weight_copies(e, slot):
        return (pltpu.make_async_copy(wi_hbm.at[e], wi_buf.at[slot], sems.at[0, slot]),
                pltpu.make_async_copy(wo_hbm.at[e], wo_buf.at[slot], sems.at[1, slot]))

    @pl.when(i == 0)
    def _():
        slot_ref[0] = 0
        for cp in weight_copies(expert, 0):
            cp.start()

    @pl.when((i == 0) | (expert != be_ref[jnp.maximum(blk - 1, 0)]))
    def _():
        slot = slot_ref[0]
        for cp in weight_copies(expert, slot):
            cp.wait()
        wi_scr[...] = wi_buf[slot].astype(BF16)
        wo_scr[...] = wo_buf[slot].astype(BF16)
        nxt = nxt_ref[blk]

        @pl.when(nxt >= 0)
        def _():
            for cp in weight_copies(nxt, 1 - slot):
                cp.start()

        slot_ref[0] = 1 - slot

    @pl.when(blk < nu_ref[0])
    def _():
        gu = _dot(x_ref[...], wi_scr[...])
        gate, up = gu[:, :MOE_D_FF], gu[:, MOE_D_FF:]
        act = gate * (1.0 / (1.0 + jnp.exp(-gate))) * up
        y_ref[...] = _dot(act.astype(BF16), wo_scr[...]).astype(y_ref.dtype)

    @pl.when(blk >= nu_ref[0])
    def _():
        y_ref[...] = jnp.zeros_like(y_ref)


def _experts(xb, blk_e, n_used, next_e, w_in, w_out, ybuf, blk0):
    rows, d = xb.shape
    n_total = blk_e.shape[0]
    in_specs = [pl.BlockSpec((MOE_ROWS, d), lambda i, be, nu, nx: (i, 0)),
                pl.BlockSpec(memory_space=pl.ANY),
                pl.BlockSpec(memory_space=pl.ANY)]
    args = [blk_e, n_used, next_e, xb, w_in, w_out]
    aliases = {}
    if ybuf is not None:
        in_specs.append(pl.BlockSpec(memory_space=pl.ANY))
        args.append(ybuf)
        aliases = {len(args) - 1: 0}
    return pl.pallas_call(
        functools.partial(_experts_kernel, blk0=blk0),
        grid_spec=pltpu.PrefetchScalarGridSpec(
            num_scalar_prefetch=3,
            grid=(rows // MOE_ROWS,),
            in_specs=in_specs,
            out_specs=pl.BlockSpec((MOE_ROWS, d), lambda i, be, nu, nx: (i + blk0, 0)),
            scratch_shapes=[pltpu.VMEM((2, d, 2 * MOE_D_FF), F32), pltpu.VMEM((2, MOE_D_FF, d), F32),
                            pltpu.VMEM((d, 2 * MOE_D_FF), BF16), pltpu.VMEM((MOE_D_FF, d), BF16),
                            pltpu.SMEM((1,), jnp.int32), pltpu.SemaphoreType.DMA((2, 2))]),
        out_shape=jax.ShapeDtypeStruct((n_total * MOE_ROWS, d), BF16),
        input_output_aliases=aliases,
        compiler_params=_cparams(("arbitrary",)),
        name="moe_experts",
    )(*args)


def _moe_ln_kernel(h_ref, y0_ref, y1_ref, r_ref, ln_ref, hf_ref, hb_ref):
    r = r_ref[...]
    f = y0_ref[...].astype(F32) * r[:, 2:3] + y1_ref[...].astype(F32) * r[:, 3:4]
    y = _layer_norm(DN_ALPHA * h_ref[...] + f, ln_ref[2:3, :], ln_ref[3:4, :])
    hf_ref[...] = y
    hb_ref[...] = y.astype(BF16)


def _moe_ln(h, y0, y1, route, ln, tm=512):
    m, d = h.shape
    row = lambda i: (i, 0)
    return pl.pallas_call(
        _moe_ln_kernel,
        grid=(m // tm,),
        in_specs=[pl.BlockSpec((tm, d), row), pl.BlockSpec((tm, d), row), pl.BlockSpec((tm, d), row),
                  pl.BlockSpec((tm, LANES), row), pl.BlockSpec((4, d), lambda i: (0, 0))],
        out_specs=[pl.BlockSpec((tm, d), row), pl.BlockSpec((tm, d), row)],
        out_shape=[jax.ShapeDtypeStruct((m, d), F32), jax.ShapeDtypeStruct((m, d), BF16)],
        compiler_params=_cparams(("parallel",)),
        name="moe_ln",
    )(h, y0, y1, route, ln)


def _moe(h, hb, router_w, router_b, w_in, w_out, ln):
    t, d = h.shape
    e, rows = MOE_EXPERTS, MOE_ROWS
    route = _router(h, router_w, router_b)
    flat_e = route[:, 0:2].astype(jnp.int32).reshape(-1)
    n_assign = flat_e.shape[0]
    n_blk = -(-n_assign // rows) + e
    onehot = (flat_e[:, None] == jnp.arange(e, dtype=jnp.int32)[None, :]).astype(jnp.int32)
    csum = jnp.cumsum(onehot, axis=0)
    rank = jnp.take_along_axis(csum, flat_e[:, None], axis=1)[:, 0] - 1
    sizes = csum[-1]
    padded = (sizes + rows - 1) // rows * rows
    pad_end = jnp.cumsum(padded)
    pad_start = pad_end - padded
    dest = pad_start[flat_e] + rank
    flat_tok = jnp.arange(n_assign, dtype=jnp.int32) // 2
    slot_tok = jnp.zeros((n_blk * rows,), jnp.int32).at[dest].set(flat_tok)
    blk_start = jnp.arange(n_blk, dtype=jnp.int32) * rows
    blk_e = jnp.minimum(jnp.sum((pad_end[None, :] <= blk_start[:, None]).astype(jnp.int32), axis=1), e - 1)
    n_used = (pad_end[-1] // rows).astype(jnp.int32).reshape(1)
    assert n_blk % MOE_CHUNKS == 0
    per = n_blk // MOE_CHUNKS
    blk_id = jnp.arange(n_blk, dtype=jnp.int32)
    run_end = pad_end[blk_e] // rows
    after = blk_e[jnp.minimum(run_end, n_blk - 1)]
    in_call = (run_end > blk_id) & (run_end < (blk_id // per + 1) * per) & (after != blk_e)
    next_e = jnp.where(in_call, after, -1).astype(jnp.int32)
    yb = None
    for c in range(MOE_CHUNKS):
        xb = hb[slot_tok[c * per * rows:(c + 1) * per * rows]]
        yb = _experts(xb, blk_e, n_used, next_e, w_in, w_out, yb, c * per)
    dest2 = dest.reshape(t, 2)
    return _moe_ln(h, yb[dest2[:, 0]], yb[dest2[:, 1]], route, ln)


def kernel(x, l0_a_w_in, l0_a_lam, l0_a_subln, l0_a_w_out, l0_ln, l0_router_w, l0_router_b, l0_moe_w_in, l0_moe_w_out, l1_b_w_in, l1_b_cmp_pos, l1_b_cmp_w1, l1_b_cmp_w2, l1_b_w_out, l1_ln, l1_router_w, l1_router_b, l1_moe_w_in, l1_moe_w_out, l2_c_w_in, l2_c_w_out, l2_ln, l2_router_w, l2_router_b, l2_moe_w_in, l2_moe_w_out, l3_a_w_in, l3_a_lam, l3_a_subln, l3_a_w_out, l3_ln, l3_router_w, l3_router_b, l3_moe_w_in, l3_moe_w_out):
    batch, seq, d = x.shape
    h = x.reshape(batch * seq, d)
    hb = h.astype(BF16)
    h, hb = _mixer_a(hb, h, l0_a_w_in, l0_a_lam, l0_a_subln, l0_a_w_out, l0_ln, 0, batch, seq)
    h, hb = _moe(h, hb, l0_router_w, l0_router_b, l0_moe_w_in, l0_moe_w_out, l0_ln)
    h, hb = _mixer_b(hb, h, l1_b_w_in, l1_b_cmp_pos, l1_b_cmp_w1, l1_b_cmp_w2, l1_b_w_out, l1_ln, batch, seq)
    h, hb = _moe(h, hb, l1_router_w, l1_router_b, l1_moe_w_in, l1_moe_w_out, l1_ln)
    h, hb = _mixer_c(hb, h, l2_c_w_in, l2_c_w_out, l2_ln, batch, seq)
    h, hb = _moe(h, hb, l2_router_w, l2_router_b, l2_moe_w_in, l2_moe_w_out, l2_ln)
    h, hb = _mixer_a(hb, h, l3_a_w_in, l3_a_lam, l3_a_subln, l3_a_w_out, l3_ln, 3, batch, seq)
    h, hb = _moe(h, hb, l3_router_w, l3_router_b, l3_moe_w_in, l3_moe_w_out, l3_ln)
    return h.reshape(batch, seq, d)
```

```python
import functools
import math

import jax
import jax.numpy as jnp
import numpy as np
from jax import lax
from jax.experimental import pallas as pl
from jax.experimental.pallas import tpu as pltpu

D_MODEL = 2048
DEPTH = 4
HEAD_DIM = 128
ROPE_THETA = 500000.0
ROPE_FRACTION = 4
NEG_FILL = -1e30

A_HEADS = D_MODEL // HEAD_DIM
A_QK_DIM = HEAD_DIM // 2

B_HEADS = D_MODEL // HEAD_DIM
B_KV_GROUPS = 4
B_REP = B_HEADS // B_KV_GROUPS
B_CMP_LEN = 32
B_CMP_STRIDE = 16
B_SEL_LEN = 64
B_SEL_TOPN = 16
B_WINDOW = 512
B_N_KV = 6

C_PATTERNS = ((128, 1), (512, 4), (2048, 16))
C_HEADS_PER_GROUP = D_MODEL // (2 * HEAD_DIM)

MOE_GROUPS = 4
MOE_EXPERTS_PER_GROUP = 8
MOE_EXPERTS = MOE_GROUPS * MOE_EXPERTS_PER_GROUP
MOE_D_FF = D_MODEL // 4
MOE_ROWS = 512
MOE_CHUNKS = 2

DN_ALPHA = (2 * DEPTH) ** 0.25
NORM_EPS = 1e-5

LANES = 128
VMEM_LIMIT = 56 * 1024 * 1024

BF16 = jnp.bfloat16
F32 = jnp.float32


def _cparams(sem):
    return pltpu.CompilerParams(dimension_semantics=sem, vmem_limit_bytes=VMEM_LIMIT)


def _dot(a, b):
    return jnp.dot(a, b, preferred_element_type=F32)


def _dot_nt(a, b):
    return lax.dot_general(a, b, (((1,), (1,)), ((), ())), preferred_element_type=F32)


HALF_LANES = LANES // 2


def _rope_layout(dim):
    n_sub = LANES // dim
    w = HALF_LANES // n_sub
    half = dim // ROPE_FRACTION // 2
    perm, kind, freq = np.zeros(LANES, np.int32), np.zeros(LANES, np.int32), np.zeros(LANES, np.int32)
    for s in range(n_sub):
        lanes = list(range(s * w, (s + 1) * w)) + list(range(HALF_LANES + s * w, HALF_LANES + (s + 1) * w))
        x1, x2 = lanes[:half], lanes[w:w + half]
        rest = [l for l in lanes if l not in x1 and l not in x2]
        for i, l in enumerate(x1):
            perm[l], kind[l], freq[l] = s * dim + i, 1, i
        for i, l in enumerate(x2):
            perm[l], kind[l], freq[l] = s * dim + half + i, 2, i
        for i, l in enumerate(rest):
            perm[l] = s * dim + 2 * half + i
    return perm, kind, freq


def _rope_tables(pos, dim):
    rot = dim // ROPE_FRACTION
    _, kind, freq = _rope_layout(dim)
    inv_freq = ROPE_THETA ** (-jnp.arange(0, rot, 2, dtype=F32) / rot)
    ang = pos.astype(F32)[:, None] * inv_freq[None, :]
    cos, sin = jnp.cos(ang)[:, freq], jnp.sin(ang)[:, freq]
    c = jnp.where(kind[None, :] > 0, cos, 1.0)
    ss = jnp.where(kind[None, :] == 1, -sin, jnp.where(kind[None, :] == 2, sin, 0.0))
    return c, ss


def _permute_blocks(w, blocks, dim):
    perm, _, _ = _rope_layout(dim)
    idx = np.arange(w.shape[1], dtype=np.int32)
    for b in blocks:
        idx[b * LANES:(b + 1) * LANES] = b * LANES + perm
    return w[:, idx]


def _apply_rope(a, c, ss):
    return a * c + pltpu.roll(a, HALF_LANES, 1) * ss


def _roped(x, c_ref, ss_ref, rows=slice(None)):
    return _apply_rope(x.astype(F32), c_ref[rows, :], ss_ref[rows, :]).astype(BF16)


def _proj_kernel(x_ref, w_ref, o_ref):
    o_ref[...] = _dot(x_ref[...], w_ref[...]).astype(o_ref.dtype)


def _proj(xb, w, tm=1024, tn=1024):
    m, k = xb.shape
    n = w.shape[1]
    assert m % tm == 0 and n % tn == 0
    return pl.pallas_call(
        _proj_kernel,
        grid=(m // tm, n // tn),
        in_specs=[pl.BlockSpec((tm, k), lambda i, j: (i, 0)),
                  pl.BlockSpec((k, tn), lambda i, j: (0, j))],
        out_specs=pl.BlockSpec((tm, tn), lambda i, j: (i, j)),
        out_shape=jax.ShapeDtypeStruct((m, n), BF16),
        compiler_params=_cparams(("parallel", "arbitrary")),
        name="proj",
    )(xb, w)


def _layer_norm(z, g, b):
    mu = jnp.mean(z, axis=-1, keepdims=True)
    zc = z - mu
    var = jnp.mean(zc * zc, axis=-1, keepdims=True)
    return zc * lax.rsqrt(var + NORM_EPS) * g + b


def _outproj_ln_kernel(o_ref, w_ref, h_ref, ln_ref, hf_ref, hb_ref):
    half = o_ref.shape[0] // 2
    for s in range(2):
        rows = slice(s * half, (s + 1) * half)
        m = _dot(o_ref[rows, :], w_ref[...])
        y = _layer_norm(DN_ALPHA * h_ref[rows, :] + m, ln_ref[0:1, :], ln_ref[1:2, :])
        hf_ref[rows, :] = y
        hb_ref[rows, :] = y.astype(BF16)


def _outproj_ln(o, w, h, ln, tm=512):
    m, k = o.shape
    d = w.shape[1]
    return pl.pallas_call(
        _outproj_ln_kernel,
        grid=(m // tm,),
        in_specs=[pl.BlockSpec((tm, k), lambda i: (i, 0)),
                  pl.BlockSpec((k, d), lambda i: (0, 0)),
                  pl.BlockSpec((tm, d), lambda i: (i, 0)),
                  pl.BlockSpec((4, d), lambda i: (0, 0))],
        out_specs=[pl.BlockSpec((tm, d), lambda i: (i, 0)),
                   pl.BlockSpec((tm, d), lambda i: (i, 0))],
        out_shape=[jax.ShapeDtypeStruct((m, d), F32), jax.ShapeDtypeStruct((m, d), BF16)],
        compiler_params=_cparams(("parallel",)),
        name="outproj_ln",
    )(o, w, h, ln)


def _causal_block(q, k, v, c, tq):
    s = _dot_nt(q, k)
    rows = q.shape[0]
    assert tq & (tq - 1) == 0
    pos_in_block = jnp.bitwise_and(lax.broadcasted_iota(jnp.int32, (rows, tq), 0), tq - 1)
    causal = lax.broadcasted_iota(jnp.int32, (rows, tq), 1) <= pos_in_block
    s_diag = jnp.where(causal, s[:, c * tq:], -jnp.inf)
    m = jnp.max(s_diag, axis=-1, keepdims=True)
    if c > 0:
        s_off = s[:, :c * tq]
        m = jnp.maximum(m, jnp.max(s_off, axis=-1, keepdims=True))
        e = jnp.concatenate([jnp.exp(s_off - m), jnp.exp(s_diag - m)], axis=1)
    else:
        e = jnp.exp(s_diag - m)
    inv = 1.0 / jnp.sum(e, axis=-1, keepdims=True)
    return _dot(e.astype(BF16), v) * inv


def _attn_a_kernel(lam_ref, q_ref, k_ref, v_ref, g_ref, c_ref, ss_ref, o_ref, k_scr, *, tq, seq, out_scale):
    k_scr[...] = _roped(k_ref[0], c_ref, ss_ref)
    for c in range(seq // tq):
        kv = (c + 1) * tq
        q = _roped(q_ref[0, c * tq:kv, :], c_ref, ss_ref, slice(c * tq, kv))
        q = q * jnp.asarray(A_QK_DIM ** -0.5, BF16)
        lane = lax.broadcasted_iota(jnp.int32, q.shape, 1)
        zero = jnp.zeros_like(q)
        first = jnp.bitwise_and(lane, HALF_LANES // (LANES // A_QK_DIM)) == 0
        q12 = jnp.concatenate([jnp.where(first, q, zero), jnp.where(first, zero, q)], axis=0)
        both = _causal_block(q12, k_scr[:kv, :], v_ref[0, :kv, :], c, tq)
        o = both[:tq] - lam_ref[0] * both[tq:]
        o = o * lax.rsqrt(jnp.mean(o * o, axis=-1, keepdims=True) + NORM_EPS) * (g_ref[...] * out_scale)
        o_ref[0, c * tq:kv, :] = o.astype(o_ref.dtype)


def _attn_a(proj, lam_full, subln, tables, batch, seq, lam_init, tq=256):
    h = A_HEADS
    tab_spec = pl.BlockSpec((seq, LANES), lambda b, hh, s: (0, 0))
    return pl.pallas_call(
        functools.partial(_attn_a_kernel, tq=tq, seq=seq, out_scale=1.0 - lam_init),
        grid_spec=pltpu.PrefetchScalarGridSpec(
            num_scalar_prefetch=1,
            grid=(batch, h),
            in_specs=[pl.BlockSpec((1, seq, HEAD_DIM), lambda b, hh, s: (b, 0, hh)),
                      pl.BlockSpec((1, seq, HEAD_DIM), lambda b, hh, s: (b, 0, h + hh)),
                      pl.BlockSpec((1, seq, HEAD_DIM), lambda b, hh, s: (b, 0, 2 * h + hh)),
                      pl.BlockSpec((1, HEAD_DIM), lambda b, hh, s: (0, 0)),
                      tab_spec, tab_spec],
            out_specs=pl.BlockSpec((1, seq, HEAD_DIM), lambda b, hh, s: (b, 0, hh)),
            scratch_shapes=[pltpu.VMEM((seq, HEAD_DIM), BF16)]),
        out_shape=jax.ShapeDtypeStruct((batch, seq, h * HEAD_DIM), BF16),
        compiler_params=_cparams(("parallel", "parallel")),
        name="attn_a",
    )(lam_full, proj, proj, proj, subln, *tables)


def _mixer_a(hb, h, w_in, lam, subln, w_out, ln, layer_idx, batch, seq):
    tables = _rope_tables(jnp.arange(seq), A_QK_DIM)
    wb = _permute_blocks(w_in, range(2 * A_HEADS), A_QK_DIM).astype(BF16)
    proj = _proj(hb, wb)
    lam_init = 0.8 - 0.6 * math.exp(-0.3 * layer_idx)
    lf = lam.astype(F32)
    lam_full = jnp.exp(jnp.sum(lf[0] * lf[1])) - jnp.exp(jnp.sum(lf[2] * lf[3])) + lam_init
    o = _attn_a(proj.reshape(batch, seq, -1), lam_full.reshape(1), subln.reshape(1, HEAD_DIM), tables, batch, seq,
                lam_init)
    return _outproj_ln(o.reshape(batch * seq, -1), w_out.astype(BF16), h, ln)


DIL_BLOCK = 128


def _band_block(q, k, v, q0, k0, span):
    s = _dot_nt(q, k)
    diff = (q0 + lax.broadcasted_iota(jnp.int32, s.shape, 0)) - (k0 + lax.broadcasted_iota(jnp.int32, s.shape, 1))
    s = jnp.where((diff >= 0) & (diff <= span), s, -jnp.inf)
    m = jnp.max(s, axis=-1, keepdims=True)
    e = jnp.exp(s - m)
    den = jnp.sum(e, axis=-1, keepdims=True)
    return _dot((e * (1.0 / den)).astype(BF16), v), m + jnp.log(den)


def _dil_kernel(*refs, seq):
    n_grp = len(C_PATTERNS)
    qkv = refs[:3 * n_grp]
    c_ref, ss_ref, o_ref = refs[3 * n_grp:3 * n_grp + 3]
    stage, o_scr, l_scr = refs[3 * n_grp + 3:]
    blk = DIL_BLOCK
    for g, (window, dil) in enumerate(C_PATTERNS):
        q_ref, k_ref, v_ref = qkv[3 * g:3 * g + 3]
        length = seq // dil
        span = window // dil
        n_blk = length // blk
        stage[0] = _apply_rope(q_ref[0].astype(F32), c_ref[...], ss_ref[...])
        stage[1] = _apply_rope(k_ref[0].astype(F32), c_ref[...], ss_ref[...])
        if dil > 1:
            stage[2] = v_ref[0].astype(F32)

        def rows(j, ref, r, start, size):
            if dil == 1:
                return ref[0, start:start + size, :] if j == 2 else stage[j, start:start + size, :].astype(BF16)
            return stage[j, pl.ds(r + start * dil, size, stride=dil), :].astype(BF16)

        for r in range(dil):
            for i in range(n_blk):
                k0 = max(i - 1, 0) * blk
                kl = (i + 1) * blk - k0
                o, lse = _band_block(rows(0, q_ref, r, i * blk, blk), rows(1, k_ref, r, k0, kl),
                                     rows(2, v_ref, r, k0, kl), i * blk, k0, span)
                lse = jnp.broadcast_to(lse, (blk, HEAD_DIM))
                if dil == 1:
                    o_scr[g, i * blk:(i + 1) * blk, :] = o
                    l_scr[g, i * blk:(i + 1) * blk, :] = lse
                else:
                    o_scr[g, pl.ds(r + i * blk * dil, blk, stride=dil), :] = o
                    l_scr[g, pl.ds(r + i * blk * dil, blk, stride=dil), :] = lse

    chunk = 256
    for c in range(seq // chunk):
        sl = slice(c * chunk, (c + 1) * chunk)
        ls = [l_scr[g, sl, :] for g in range(n_grp)]
        m = functools.reduce(jnp.maximum, ls)
        es = [jnp.exp(l - m) for l in ls]
        inv = 1.0 / functools.reduce(lambda a, b: a + b, es)
        o = functools.reduce(lambda a, b: a + b, [(es[g] * inv) * o_scr[g, sl, :] for g in range(n_grp)])
        o_ref[0, sl, :] = o.astype(o_ref.dtype)


def _dil_attn(proj, tables, batch, seq):
    hg, p = C_HEADS_PER_GROUP, len(C_PATTERNS)

    def spec(kind, g):
        return pl.BlockSpec((1, seq, HEAD_DIM), lambda b, hh: (b, 0, kind * p * hg + g * hg + hh))

    in_specs = [spec(kind, g) for g in range(p) for kind in range(3)]
    tab_spec = pl.BlockSpec((seq, LANES), lambda b, hh: (0, 0))
    return pl.pallas_call(
        functools.partial(_dil_kernel, seq=seq),
        grid=(batch, hg),
        in_specs=in_specs + [tab_spec, tab_spec],
        out_specs=pl.BlockSpec((1, seq, HEAD_DIM), lambda b, hh: (b, 0, hh)),
        out_shape=jax.ShapeDtypeStruct((batch, seq, hg * HEAD_DIM), BF16),
        scratch_shapes=[pltpu.VMEM((3, seq, HEAD_DIM), F32), pltpu.VMEM((p, seq, HEAD_DIM), F32),
                        pltpu.VMEM((p, seq, HEAD_DIM), F32)],
        compiler_params=_cparams(("parallel", "parallel")),
        name="dil_attn",
    )(*([proj] * len(in_specs)), *tables)


def _mixer_c(hb, h, w_in, w_out, ln, batch, seq):
    tables = _rope_tables(jnp.arange(seq), HEAD_DIM)
    n = w_in.shape[1]
    wb = _scaled_bf16(_permute_blocks(w_in, range(2 * n // 3 // LANES), HEAD_DIM), n // 3, HEAD_DIM ** -0.5)
    proj = _proj(hb, wb)
    o = _dil_attn(proj.reshape(batch, seq, n), tables, batch, seq)
    return _outproj_ln(o.reshape(batch * seq, -1), w_out.astype(BF16), h, ln)


def _gates_kernel(x_ref, w_ref, o_ref):
    z = _dot(x_ref[...], w_ref[...])
    o_ref[...] = 1.0 / (1.0 + jnp.exp(-z))


def _gates(xb, wg, tm=1024):
    m, k = xb.shape
    n = wg.shape[1]
    return pl.pallas_call(
        _gates_kernel,
        grid=(m // tm,),
        in_specs=[pl.BlockSpec((tm, k), lambda i: (i, 0)), pl.BlockSpec((k, n), lambda i: (0, 0))],
        out_specs=pl.BlockSpec((tm, n), lambda i: (i, 0)),
        out_shape=jax.ShapeDtypeStruct((m, n), F32),
        compiler_params=_cparams(("parallel",)),
        name="nsa_gates",
    )(xb, wg)


def _gelu_tanh(x):
    return 0.5 * x * (1.0 + jnp.tanh(math.sqrt(2.0 / math.pi) * (x + 0.044715 * (x * x * x))))


def _compress_kernel(a_ref, pos_ref, w1_ref, w2_ref, c_ref, ss_ref, o_ref, stage):
    kind = pl.program_id(2)
    stage[...] = a_ref[0].astype(F32)
    n_chunk = stage.shape[0] // B_CMP_STRIDE
    first = jnp.zeros((n_chunk, HEAD_DIM), F32)
    second = jnp.zeros((n_chunk, HEAD_DIM), F32)
    for t in range(B_CMP_STRIDE):
        a = stage[pl.ds(t, n_chunk, stride=B_CMP_STRIDE), :]
        u = B_CMP_STRIDE + t
        first = first + _dot((a + pos_ref[0, t:t + 1, :]).astype(BF16), w1_ref[0, t * HEAD_DIM:(t + 1) * HEAD_DIM, :])
        second = second + _dot((a + pos_ref[0, u:u + 1, :]).astype(BF16), w1_ref[0, u * HEAD_DIM:(u + 1) * HEAD_DIM, :])
    hmid = first + pltpu.roll(second, n_chunk - 1, 0)
    out = _dot(_gelu_tanh(hmid).astype(BF16), w2_ref[0])

    @pl.when(kind == 0)
    def _():
        o_ref[0, 0] = _apply_rope(out, c_ref[...], ss_ref[...]).astype(o_ref.dtype)

    @pl.when(kind != 0)
    def _():
        o_ref[0, 0] = out.astype(o_ref.dtype)


def _compress(proj3, cmp_pos, cmp_w1, cmp_w2, batch, seq):
    g = B_KV_GROUPS
    n_chunk = seq // B_CMP_STRIDE
    cmp_end = jnp.arange(n_chunk) * B_CMP_STRIDE + B_CMP_LEN - 1
    tables = _rope_tables(cmp_end, HEAD_DIM)
    tab_spec = pl.BlockSpec((n_chunk, LANES), lambda b, gg, kk: (0, 0))
    perm, _, _ = _rope_layout(HEAD_DIM)
    cmp_w2 = jnp.stack([cmp_w2[0][:, perm], cmp_w2[1]])
    return pl.pallas_call(
        _compress_kernel,
        grid=(batch, g, 2),
        in_specs=[pl.BlockSpec((1, seq, HEAD_DIM), lambda b, gg, kk: (b, 0, B_HEADS + kk * g + gg)),
                  pl.BlockSpec((1, B_CMP_LEN, HEAD_DIM), lambda b, gg, kk: (kk, 0, 0)),
                  pl.BlockSpec((1, B_CMP_LEN * HEAD_DIM, HEAD_DIM), lambda b, gg, kk: (kk, 0, 0)),
                  pl.BlockSpec((1, HEAD_DIM, HEAD_DIM), lambda b, gg, kk: (kk, 0, 0)),
                  tab_spec, tab_spec],
        out_specs=pl.BlockSpec((1, 1, n_chunk, HEAD_DIM), lambda b, gg, kk: (kk * batch + b, gg, 0, 0)),
        out_shape=jax.ShapeDtypeStruct((2 * batch, g, n_chunk, HEAD_DIM), BF16),
        scratch_shapes=[pltpu.VMEM((seq, HEAD_DIM), F32)],
        compiler_params=_cparams(("parallel", "parallel", "arbitrary")),
        name="nsa_compress",
    )(proj3, cmp_pos, cmp_w1.astype(BF16), cmp_w2.astype(BF16), *tables)


def _split3(x):
    hi = x.astype(BF16)
    r = x - hi.astype(F32)
    mid = r.astype(BF16)
    lo = (r - mid.astype(F32)).astype(BF16)
    return hi, mid, lo


def _nsa_kernel(q_ref, kc_ref, vc_ref, ks_ref, vs_ref, kw_ref, vw_ref, gate_ref, ex_ref, c_ref, ss_ref, o_ref,
                vct_scr, ocmp_scr, ks_scr, kw_scr, q_scr, *, tq, seq, n_cmp, n_sel):
    rep = B_REP
    big = 1e30
    vct_scr[...] = vc_ref[0, 0].astype(F32).T.astype(BF16)
    ks_scr[...] = _roped(ks_ref[0], c_ref, ss_ref)
    kw_scr[...] = _roped(kw_ref[0], c_ref, ss_ref)
    kc = kc_ref[0, 0]
    lane = lax.broadcasted_iota(jnp.int32, (tq, LANES), 1)
    n_back = -(-(B_WINDOW - 1) // tq)
    pair_rows = 2 * tq
    wrow = jnp.bitwise_and(lax.broadcasted_iota(jnp.int32, (pair_rows, tq), 0), tq - 1)
    wcol = lax.broadcasted_iota(jnp.int32, (pair_rows, tq), 1)
    win_bias = []
    for back in range(n_back + 1):
        if back * tq - (tq - 1) >= 0 and back * tq + (tq - 1) <= B_WINDOW - 1:
            win_bias.append(None)
        else:
            diff = back * tq + wrow - wcol
            win_bias.append(jnp.where((diff >= 0) & (diff <= B_WINDOW - 1), 0.0, -jnp.inf))

    for c in range(seq // tq):
        lo, hi = c * tq, (c + 1) * tq

        qpos = lo + lax.broadcasted_iota(jnp.int32, (1, tq), 1)
        crow = lax.broadcasted_iota(jnp.int32, (LANES, tq), 0)
        cmask = (crow * B_CMP_STRIDE + (B_CMP_LEN - 1) <= qpos) & (crow < n_cmp)

        p_sum = jnp.zeros((LANES, tq), F32)
        for r in range(rep):
            cols = slice(r * HEAD_DIM, (r + 1) * HEAD_DIM)
            q_scr[:, cols] = _roped(q_ref[0, lo:hi, cols], c_ref, ss_ref, slice(lo, hi))
            s = jnp.where(cmask, _dot_nt(kc, q_scr[:, cols]), NEG_FILL)
            e = jnp.exp(s - jnp.max(s, axis=0, keepdims=True))
            p = jnp.where(cmask, e * (1.0 / jnp.sum(e, axis=0, keepdims=True)), 0.0)
            ocmp_scr[r] = _dot(vct_scr[...], p.astype(BF16)).T
            p_sum = p_sum + p

        ni = lax.broadcasted_iota(jnp.int32, (LANES, LANES), 0)
        ci = lax.broadcasted_iota(jnp.int32, (LANES, LANES), 1)
        per = B_SEL_LEN // B_CMP_STRIDE
        cover = ((ci >= per * ni - (B_CMP_LEN // B_CMP_STRIDE - 1)) & (ci < per * ni + per) & (ci < n_cmp) & (ni < n_sel))
        cover = jnp.where(cover, 1.0, 0.0).astype(BF16)
        importance = sum(_dot(cover, part) for part in _split3(p_sum))[:n_sel]
        nrow = lax.broadcasted_iota(jnp.int32, (n_sel, tq), 0)
        cur = jnp.right_shift(qpos, int(math.log2(B_SEL_LEN)))
        forced = (nrow == 0) | (nrow == cur) | (nrow == cur - 1)
        score = jnp.where(forced, big, importance)
        score = jnp.where(nrow <= cur, score, -big)
        rank = jnp.zeros((n_sel, tq), jnp.int32)
        for mm in range(min(n_sel, (hi - 1) // B_SEL_LEN + 1)):
            other = score[mm:mm + 1, :]
            ahead = (other > score) | ((other == score) & (nrow > mm))
            rank = rank + jnp.where(ahead, 1, 0)
        sel_bias = jnp.where(rank < min(B_SEL_TOPN, n_sel), 0.0, -big)
        sel_bias = jnp.concatenate([sel_bias, jnp.zeros((LANES - n_sel, tq), F32)], axis=0).T.astype(BF16)

        k_aug = jnp.concatenate([ks_scr[:hi, :], ex_ref[:hi, :]], axis=1)
        v_sel = vs_ref[0, :hi, :]
        wlo = max(c - n_back, 0) * tq
        k_win, v_win = kw_scr[wlo:hi, :], vw_ref[0, wlo:hi, :]
        n_wchunks = (hi - wlo) // tq
        gates = gate_ref[0, lo:hi, :]
        bias2 = jnp.concatenate([sel_bias, sel_bias], axis=0)

        def head_pair(pr, carry):
            q2 = q_scr[:, pl.ds(pl.multiple_of(pr * 2 * HEAD_DIM, 2 * HEAD_DIM), 2 * HEAD_DIM)]
            q = jnp.concatenate([q2[:, :HEAD_DIM], q2[:, HEAD_DIM:]], axis=0)
            o_sel = _causal_block(jnp.concatenate([q, bias2], axis=1), k_aug, v_sel, c, tq)
            s = _dot_nt(q, k_win)
            pieces = []
            for j in range(n_wchunks):
                piece, bias = s[:, j * tq:(j + 1) * tq], win_bias[n_wchunks - 1 - j]
                pieces.append(piece if bias is None else piece + bias)
            s = jnp.concatenate(pieces, axis=1)
            e = jnp.exp(s - jnp.max(s, axis=-1, keepdims=True))
            o_win = _dot(e.astype(BF16), v_win) * (1.0 / jnp.sum(e, axis=-1, keepdims=True))
            outs = []
            for j in range(2):
                r = pr * 2 + j

                def gate(b):
                    return jnp.sum(jnp.where(lane == r * 3 + b, gates, 0.0), axis=-1, keepdims=True)

                rows = slice(j * tq, (j + 1) * tq)
                outs.append(gate(0) * ocmp_scr[r] + gate(1) * o_sel[rows] + gate(2) * o_win[rows])
            o_ref[0, lo:hi, pl.ds(pl.multiple_of(pr * 2 * HEAD_DIM, 2 * HEAD_DIM), 2 * HEAD_DIM)] = (
                jnp.concatenate(outs, axis=1).astype(o_ref.dtype))
            return carry

        lax.fori_loop(0, rep // 2, head_pair, 0)


def _nsa_attn(proj3, kvc, gates, tables, batch, seq, tq=256):
    g, rep = B_KV_GROUPS, B_REP
    n_chunk = seq // B_CMP_STRIDE
    n_cmp = n_chunk - B_CMP_LEN // B_CMP_STRIDE + 1
    n_sel = seq // B_SEL_LEN
    assert n_chunk == LANES and n_sel <= LANES and n_sel % 8 == 0
    onehot = (jnp.arange(seq)[:, None] // B_SEL_LEN == jnp.arange(LANES)[None, :]).astype(BF16)

    def kv_spec(j):
        return pl.BlockSpec((1, seq, HEAD_DIM), lambda b, gg: (b, 0, B_HEADS + j * g + gg))

    return pl.pallas_call(
        functools.partial(_nsa_kernel, tq=tq, seq=seq, n_cmp=n_cmp, n_sel=n_sel),
        grid=(batch, g),
        in_specs=[pl.BlockSpec((1, seq, rep * HEAD_DIM), lambda b, gg: (b, 0, gg)),
                  pl.BlockSpec((1, 1, n_chunk, HEAD_DIM), lambda b, gg: (b, gg, 0, 0)),
                  pl.BlockSpec((1, 1, n_chunk, HEAD_DIM), lambda b, gg: (batch + b, gg, 0, 0)),
                  kv_spec(2), kv_spec(3), kv_spec(4), kv_spec(5),
                  pl.BlockSpec((1, seq, LANES), lambda b, gg: (b, 0, gg)),
                  pl.BlockSpec((seq, LANES), lambda b, gg: (0, 0)),
                  pl.BlockSpec((seq, LANES), lambda b, gg: (0, 0)),
                  pl.BlockSpec((seq, LANES), lambda b, gg: (0, 0))],
        out_specs=pl.BlockSpec((1, seq, rep * HEAD_DIM), lambda b, gg: (b, 0, gg)),
        out_shape=jax.ShapeDtypeStruct((batch, seq, B_HEADS * HEAD_DIM), BF16),
        scratch_shapes=[pltpu.VMEM((HEAD_DIM, n_chunk), BF16), pltpu.VMEM((rep, tq, HEAD_DIM), F32),
                        pltpu.VMEM((seq, HEAD_DIM), BF16), pltpu.VMEM((seq, HEAD_DIM), BF16),
                        pltpu.VMEM((tq, rep * HEAD_DIM), BF16)],
        compiler_params=_cparams(("parallel", "parallel")),
        name="nsa_attn",
    )(proj3, kvc, kvc, proj3, proj3, proj3, proj3, gates.reshape(batch, seq, g * LANES), onehot, *tables)


def _scaled_bf16(w, n_cols, scale):
    col_scale = jnp.where(jnp.arange(w.shape[1]) < n_cols, scale, 1.0).astype(F32)
    return (w * col_scale[None, :]).astype(BF16)


def _mixer_b(hb, h, w_in, cmp_pos, cmp_w1, cmp_w2, w_out, ln, batch, seq):
    g, rep = B_KV_GROUPS, B_REP
    n_main = (B_HEADS + B_N_KV * g) * HEAD_DIM
    tables = _rope_tables(jnp.arange(seq), HEAD_DIM)
    roped = (list(range(B_HEADS)) + list(range(B_HEADS + 2 * g, B_HEADS + 3 * g))
             + list(range(B_HEADS + 4 * g, B_HEADS + 5 * g)))
    wb = _scaled_bf16(_permute_blocks(w_in, roped, HEAD_DIM), B_HEADS * HEAD_DIM, HEAD_DIM ** -0.5)
    proj3 = _proj(hb, wb[:, :n_main]).reshape(batch, seq, n_main)
    wg = wb[:, n_main:].reshape(-1, g, rep * 3)
    wg = jnp.pad(wg, ((0, 0), (0, 0), (0, LANES - rep * 3))).reshape(-1, g * LANES)
    gates = _gates(hb, wg)
    kvc = _compress(proj3, cmp_pos, cmp_w1, cmp_w2, batch, seq)
    o = _nsa_attn(proj3, kvc, gates, tables, batch, seq)
    return _outproj_ln(o.reshape(batch * seq, -1), w_out.astype(BF16), h, ln)


def _router_kernel(x_ref, w_ref, b_ref, o_ref):
    x = x_ref[...]
    xh = x.astype(BF16)
    xl = (x - xh.astype(F32)).astype(BF16)
    hi = _dot(xh, w_ref[...])
    logits = hi[:, :LANES] + (hi[:, LANES:] + _dot(xl, w_ref[:, :LANES])) + b_ref[...]
    lane = lax.broadcasted_iota(jnp.int32, logits.shape, 1)
    lane_f = lane.astype(F32)

    def first_max(vals):
        top = jnp.max(vals, axis=-1, keepdims=True)
        first = jnp.min(jnp.where(vals == top, lane_f, float(4 * LANES)), axis=-1, keepdims=True)
        return top, first.astype(jnp.int32)

    gl = jnp.where(lane < MOE_GROUPS, logits, -jnp.inf)
    gmax, gidx = first_max(gl)
    g_w = 1.0 / jnp.sum(jnp.exp(gl - gmax), axis=-1, keepdims=True)
    lo = MOE_GROUPS + MOE_EXPERTS_PER_GROUP * gidx
    ev = jnp.where((lane >= lo) & (lane < lo + MOE_EXPERTS_PER_GROUP), logits, -jnp.inf)
    v1, i1 = first_max(ev)
    v2, i2 = first_max(jnp.where(lane == i1, -jnp.inf, ev))
    e2 = jnp.exp(v2 - v1)
    w1 = g_w / (1.0 + e2)
    w2 = g_w * e2 / (1.0 + e2)
    out = jnp.where(lane == 0, (i1 - MOE_GROUPS).astype(F32),
                    jnp.where(lane == 1, (i2 - MOE_GROUPS).astype(F32),
                              jnp.where(lane == 2, w1, jnp.where(lane == 3, w2, 0.0))))
    o_ref[...] = out


def _router(h, router_w, router_b, tm=1024):
    m, d = h.shape
    n = router_w.shape[1]
    wp = jnp.pad(router_w, ((0, 0), (0, LANES - n)))
    wh = wp.astype(BF16)
    wp = jnp.concatenate([wh, (wp - wh.astype(F32)).astype(BF16)], axis=1)
    bp = jnp.pad(router_b, (0, LANES - n)).reshape(1, LANES)
    return pl.pallas_call(
        _router_kernel,
        grid=(m // tm,),
        in_specs=[pl.BlockSpec((tm, d), lambda i: (i, 0)),
                  pl.BlockSpec((d, 2 * LANES), lambda i: (0, 0)),
                  pl.BlockSpec((1, LANES), lambda i: (0, 0))],
        out_specs=pl.BlockSpec((tm, LANES), lambda i: (i, 0)),
        out_shape=jax.ShapeDtypeStruct((m, LANES), F32),
        compiler_params=_cparams(("parallel",)),
        name="moe_router",
    )(h, wp, bp)


def _experts_kernel(be_ref, nu_ref, nxt_ref, x_ref, wi_hbm, wo_hbm, *rest, blk0):
    y_ref, wi_buf, wo_buf, wi_scr, wo_scr, slot_ref, sems = rest[-7:]
    i = pl.program_id(0)
    blk = i + blk0
    expert = be_ref[blk]

    def weight_copies(e, slot):
        return (pltpu.make_async_copy(wi_hbm.at[e], wi_buf.at[slot], sems.at[0, slot]),
                pltpu.make_async_copy(wo_hbm.at[e], wo_buf.at[slot], sems.at[1, slot]))

    @pl.when(i == 0)
    def _():
        slot_ref[0] = 0
        for cp in weight_copies(expert, 0):
            cp.start()

    @pl.when((i == 0) | (expert != be_ref[jnp.maximum(blk - 1, 0)]))
    def _():
        slot = slot_ref[0]
        for cp in weight_copies(expert, slot):
            cp.wait()
        wi_scr[...] = wi_buf[slot].astype(BF16)
        wo_scr[...] = wo_buf[slot].astype(BF16)
        nxt = nxt_ref[blk]

        @pl.when(nxt >= 0)
        def _():
            for cp in weight_copies(nxt, 1 - slot):
                cp.start()

        slot_ref[0] = 1 - slot

    @pl.when(blk < nu_ref[0])
    def _():
        gu = _dot(x_ref[...], wi_scr[...])
        gate, up = gu[:, :MOE_D_FF], gu[:, MOE_D_FF:]
        act = gate * (1.0 / (1.0 + jnp.exp(-gate))) * up
        y_ref[...] = _dot(act.astype(BF16), wo_scr[...]).astype(y_ref.dtype)

    @pl.when(blk >= nu_ref[0])
    def _():
        y_ref[...] = jnp.zeros_like(y_ref)


def _experts(xb, blk_e, n_used, next_e, w_in, w_out, ybuf, blk0):
    rows, d = xb.shape
    n_total = blk_e.shape[0]
    in_specs = [pl.BlockSpec((MOE_ROWS, d), lambda i, be, nu, nx: (i, 0)),
                pl.BlockSpec(memory_space=pl.ANY),
                pl.BlockSpec(memory_space=pl.ANY)]
    args = [blk_e, n_used, next_e, xb, w_in, w_out]
    aliases = {}
    if ybuf is not None:
        in_specs.append(pl.BlockSpec(memory_space=pl.ANY))
        args.append(ybuf)
        aliases = {len(args) - 1: 0}
    return pl.pallas_call(
        functools.partial(_experts_kernel, blk0=blk0),
        grid_spec=pltpu.PrefetchScalarGridSpec(
            num_scalar_prefetch=3,
            grid=(rows // MOE_ROWS,),
            in_specs=in_specs,
            out_specs=pl.BlockSpec((MOE_ROWS, d), lambda i, be, nu, nx: (i + blk0, 0)),
            scratch_shapes=[pltpu.VMEM((2, d, 2 * MOE_D_FF), F32), pltpu.VMEM((2, MOE_D_FF, d), F32),
                            pltpu.VMEM((d, 2 * MOE_D_FF), BF16), pltpu.VMEM((MOE_D_FF, d), BF16),
                            pltpu.SMEM((1,), jnp.int32), pltpu.SemaphoreType.DMA((2, 2))]),
        out_shape=jax.ShapeDtypeStruct((n_total * MOE_ROWS, d), BF16),
        input_output_aliases=aliases,
        compiler_params=_cparams(("arbitrary",)),
        name="moe_experts",
    )(*args)


def _moe_ln_kernel(h_ref, y0_ref, y1_ref, r_ref, ln_ref, hf_ref, hb_ref):
    r = r_ref[...]
    f = y0_ref[...].astype(F32) * r[:, 2:3] + y1_ref[...].astype(F32) * r[:, 3:4]
    y = _layer_norm(DN_ALPHA * h_ref[...] + f, ln_ref[2:3, :], ln_ref[3:4, :])
    hf_ref[...] = y
    hb_ref[...] = y.astype(BF16)


def _moe_ln(h, y0, y1, route, ln, tm=512):
    m, d = h.shape
    row = lambda i: (i, 0)
    return pl.pallas_call(
        _moe_ln_kernel,
        grid=(m // tm,),
        in_specs=[pl.BlockSpec((tm, d), row), pl.BlockSpec((tm, d), row), pl.BlockSpec((tm, d), row),
                  pl.BlockSpec((tm, LANES), row), pl.BlockSpec((4, d), lambda i: (0, 0))],
        out_specs=[pl.BlockSpec((tm, d), row), pl.BlockSpec((tm, d), row)],
        out_shape=[jax.ShapeDtypeStruct((m, d), F32), jax.ShapeDtypeStruct((m, d), BF16)],
        compiler_params=_cparams(("parallel",)),
        name="moe_ln",
    )(h, y0, y1, route, ln)


def _moe(h, hb, router_w, router_b, w_in, w_out, ln):
    t, d = h.shape
    e, rows = MOE_EXPERTS, MOE_ROWS
    route = _router(h, router_w, router_b)
    flat_e = route[:, 0:2].astype(jnp.int32).reshape(-1)
    n_assign = flat_e.shape[0]
    n_blk = -(-n_assign // rows) + e
    onehot = (flat_e[:, None] == jnp.arange(e, dtype=jnp.int32)[None, :]).astype(jnp.int32)
    csum = jnp.cumsum(onehot, axis=0)
    rank = jnp.take_along_axis(csum, flat_e[:, None], axis=1)[:, 0] - 1
    sizes = csum[-1]
    padded = (sizes + rows - 1) // rows * rows
    pad_end = jnp.cumsum(padded)
    pad_start = pad_end - padded
    dest = pad_start[flat_e] + rank
    flat_tok = jnp.arange(n_assign, dtype=jnp.int32) // 2
    slot_tok = jnp.zeros((n_blk * rows,), jnp.int32).at[dest].set(flat_tok)
    blk_start = jnp.arange(n_blk, dtype=jnp.int32) * rows
    blk_e = jnp.minimum(jnp.sum((pad_end[None, :] <= blk_start[:, None]).astype(jnp.int32), axis=1), e - 1)
    n_used = (pad_end[-1] // rows).astype(jnp.int32).reshape(1)
    assert n_blk % MOE_CHUNKS == 0
    per = n_blk // MOE_CHUNKS
    blk_id = jnp.arange(n_blk, dtype=jnp.int32)
    run_end = pad_end[blk_e] // rows
    after = blk_e[jnp.minimum(run_end, n_blk - 1)]
    in_call = (run_end > blk_id) & (run_end < (blk_id // per + 1) * per) & (after != blk_e)
    next_e = jnp.where(in_call, after, -1).astype(jnp.int32)
    yb = None
    for c in range(MOE_CHUNKS):
        xb = hb[slot_tok[c * per * rows:(c + 1) * per * rows]]
        yb = _experts(xb, blk_e, n_used, next_e, w_in, w_out, yb, c * per)
    dest2 = dest.reshape(t, 2)
    return _moe_ln(h, yb[dest2[:, 0]], yb[dest2[:, 1]], route, ln)


def kernel(x, l0_a_w_in, l0_a_lam, l0_a_subln, l0_a_w_out, l0_ln, l0_router_w, l0_router_b, l0_moe_w_in, l0_moe_w_out, l1_b_w_in, l1_b_cmp_pos, l1_b_cmp_w1, l1_b_cmp_w2, l1_b_w_out, l1_ln, l1_router_w, l1_router_b, l1_moe_w_in, l1_moe_w_out, l2_c_w_in, l2_c_w_out, l2_ln, l2_router_w, l2_router_b, l2_moe_w_in, l2_moe_w_out, l3_a_w_in, l3_a_lam, l3_a_subln, l3_a_w_out, l3_ln, l3_router_w, l3_router_b, l3_moe_w_in, l3_moe_w_out):
    batch, seq, d = x.shape
    h = x.reshape(batch * seq, d)
    hb = h.astype(BF16)
    h, hb = _mixer_a(hb, h, l0_a_w_in, l0_a_lam, l0_a_subln, l0_a_w_out, l0_ln, 0, batch, seq)
    h, hb = _moe(h, hb, l0_router_w, l0_router_b, l0_moe_w_in, l0_moe_w_out, l0_ln)
    h, hb = _mixer_b(hb, h, l1_b_w_in, l1_b_cmp_pos, l1_b_cmp_w1, l1_b_cmp_w2, l1_b_w_out, l1_ln, batch, seq)
    h, hb = _moe(h, hb, l1_router_w, l1_router_b, l1_moe_w_in, l1_moe_w_out, l1_ln)
    h, hb = _mixer_c(hb, h, l2_c_w_in, l2_c_w_out, l2_ln, batch, seq)
    h, hb = _moe(h, hb, l2_router_w, l2_router_b, l2_moe_w_in, l2_moe_w_out, l2_ln)
    h, hb = _mixer_a(hb, h, l3_a_w_in, l3_a_lam, l3_a_subln, l3_a_w_out, l3_ln, 3, batch, seq)
    h, hb = _moe(h, hb, l3_router_w, l3_router_b, l3_moe_w_in, l3_moe_w_out, l3_ln)
    return h.reshape(batch, seq, d)
```
